```python
import jax, jax.numpy as jnp
from jax import lax
import numpy as np

D_MODEL = 1024
BATCH = 4
SEQ = 4096
DEPTH = 1

GDN_HEADS = 8
GDN_HEAD_DIM = 128
GDN_WIDTH = GDN_HEADS * GDN_HEAD_DIM
CONV_WIDTH = 4
GDN_CHUNK = 64
FOX_HEADS = 16
FOX_HEAD_DIM = 64
FOX_WIDTH = FOX_HEADS * FOX_HEAD_DIM
FOX_BLOCK = 128
D_FF = -(-8 * D_MODEL // (3 * 256)) * 256
EPS = 1e-6

IN_SPLITS = (
    GDN_WIDTH, GDN_WIDTH, GDN_WIDTH,
    GDN_WIDTH,
    GDN_HEADS, GDN_HEADS,
    FOX_WIDTH, FOX_WIDTH, FOX_WIDTH,
    FOX_HEADS,
    FOX_WIDTH,
    D_MODEL, D_MODEL,
)
D_IN = sum(IN_SPLITS)

kernel_name = "hybrid_gdn_fox_gated_merge"


def rms_norm(x, w):
    xf = x.astype(jnp.float32)
    y = xf * lax.rsqrt(jnp.mean(xf * xf, axis=-1, keepdims=True) + EPS)
    return (y * w.astype(jnp.float32)).astype(x.dtype)


def l2_normalize(x):
    xf = x.astype(jnp.float32)
    return xf * lax.rsqrt(jnp.sum(xf * xf, axis=-1, keepdims=True) + EPS)


def causal_depthwise_conv(x, w):
    c = x.shape[-1]
    return lax.conv_general_dilated(
        x, w[:, None, :].astype(x.dtype), window_strides=(1,),
        padding=[(CONV_WIDTH - 1, 0)],
        dimension_numbers=("NWC", "WIO", "NWC"), feature_group_count=c)


def gated_delta_rule_chunked(q, k, v, g, beta):
    b, s, h, dk = q.shape
    dv = v.shape[-1]
    n = s // GDN_CHUNK
    c = GDN_CHUNK

    def chunks(t):
        return t.astype(jnp.float32).reshape(b, n, c, h, -1).transpose(0, 3, 1, 2, 4)

    q = chunks(q) * (dk ** -0.5)
    k = chunks(k)
    v = chunks(v)
    g = g.astype(jnp.float32).reshape(b, n, c, h).transpose(0, 3, 1, 2)
    beta = beta.astype(jnp.float32).reshape(b, n, c, h).transpose(0, 3, 1, 2)
    gc = jnp.cumsum(g, axis=-1)

    tri = jnp.tril(jnp.ones((c, c), dtype=bool))
    strict = jnp.tril(jnp.ones((c, c), dtype=bool), -1)
    diff = gc[..., :, None] - gc[..., None, :]
    decay = jnp.where(tri, jnp.exp(jnp.where(tri, diff, 0.0)), 0.0)

    kb = k * beta[..., None]
    vb = v * beta[..., None]
    lower = jnp.where(strict, jnp.einsum('bhncd,bhnsd->bhncs', kb, k) * decay, 0.0)
    a_mat = lower + jnp.eye(c, dtype=jnp.float32)
    rhs = jnp.concatenate([vb, kb * jnp.exp(gc)[..., None]], axis=-1)
    sol = lax.linalg.triangular_solve(a_mat, rhs, left_side=True, lower=True,
                                      unit_diagonal=True)
    u = sol[..., :dv]
    w = sol[..., dv:]
    qk = jnp.where(tri, jnp.einsum('bhncd,bhnsd->bhncs', q, k) * decay, 0.0)

    def step(state, inp):
        q_n, k_n, u_n, w_n, qk_n, gc_n = inp
        v_new = u_n - jnp.einsum('bhcd,bhde->bhce', w_n, state)
        o = (jnp.einsum('bhcd,bhde->bhce', q_n * jnp.exp(gc_n)[..., None], state)
             + jnp.einsum('bhcs,bhse->bhce', qk_n, v_new))
        g_last = gc_n[..., -1]
        k_dec = k_n * jnp.exp(g_last[..., None] - gc_n)[..., None]
        state = (state * jnp.exp(g_last)[..., None, None]
                 + jnp.einsum('bhcd,bhce->bhde', k_dec, v_new))
        return state, o

    xs = tuple(jnp.moveaxis(t, 2, 0) for t in (q, k, u, w, qk, gc))
    state0 = jnp.zeros((b, h, dk, dv), jnp.float32)
    _, o = lax.scan(step, state0, xs)
    o = jnp.moveaxis(o, 0, 2)
    return o.transpose(0, 2, 3, 1, 4).reshape(b, s, h, dv)


def forgetting_attention(q, k, v, log_f):
    b, s, h, d = q.shape
    nb = s // FOX_BLOCK
    scale = d ** -0.5
    cum = jnp.cumsum(log_f.astype(jnp.float32), axis=1)
    cum_k = cum.transpose(0, 2, 1)
    q_blocks = q.reshape(b, nb, FOX_BLOCK, h, d).transpose(1, 0, 2, 3, 4)
    c_blocks = cum.reshape(b, nb, FOX_BLOCK, h).transpose(1, 0, 3, 2)
    kpos = jnp.arange(s)

    def one_block(args):
        q_i, c_i, i = args
        sc = jnp.einsum('bqhd,bkhd->bhqk', q_i, k).astype(jnp.float32) * scale
        sc = sc + c_i[..., :, None] - cum_k[:, :, None, :]
        qpos = i * FOX_BLOCK + jnp.arange(FOX_BLOCK)
        mask = kpos[None, :] <= qpos[:, None]
        sc = jnp.where(mask, sc, -jnp.inf)
        p = jax.nn.softmax(sc, axis=-1).astype(v.dtype)
        return jnp.einsum('bhqk,bkhd->bqhd', p, v)

    out = lax.map(one_block, (q_blocks, c_blocks, jnp.arange(nb)))
    return out.transpose(1, 0, 2, 3, 4).reshape(b, s, h, d)


def hybrid_mixer(h, w_in, conv_w, a_log, dt_bias, gdn_norm_w, fox_f_bias,
                 w_branch_a, w_branch_b, w_out):
    b, s, _ = h.shape
    proj = h @ w_in
    offs = np.cumsum(np.array(IN_SPLITS))[:-1].tolist()
    (gq, gk, gv, gz, ga, gb, fq, fk, fv, ff, fo, mga, mgb) = jnp.split(proj, offs, axis=-1)

    qkv = jax.nn.silu(causal_depthwise_conv(jnp.concatenate([gq, gk, gv], axis=-1), conv_w))
    cq, ck, cv = jnp.split(qkv, 3, axis=-1)
    cq = l2_normalize(cq.reshape(b, s, GDN_HEADS, GDN_HEAD_DIM))
    ck = l2_normalize(ck.reshape(b, s, GDN_HEADS, GDN_HEAD_DIM))
    cv = cv.reshape(b, s, GDN_HEADS, GDN_HEAD_DIM)
    g = -jnp.exp(a_log.astype(jnp.float32)) * jax.nn.softplus(
        ga.astype(jnp.float32) + dt_bias.astype(jnp.float32))
    beta = jax.nn.sigmoid(gb.astype(jnp.float32))
    o_a = gated_delta_rule_chunked(cq, ck, cv, g, beta)
    o_a = o_a * lax.rsqrt(jnp.mean(o_a * o_a, axis=-1, keepdims=True) + EPS)
    o_a = o_a * gdn_norm_w.astype(jnp.float32) * jax.nn.silu(
        gz.astype(jnp.float32).reshape(b, s, GDN_HEADS, GDN_HEAD_DIM))
    o_a = o_a.reshape(b, s, GDN_WIDTH).astype(h.dtype)

    log_f = jax.nn.log_sigmoid(ff.astype(jnp.float32) + fox_f_bias.astype(jnp.float32))
    o_b = forgetting_attention(fq.reshape(b, s, FOX_HEADS, FOX_HEAD_DIM),
                               fk.reshape(b, s, FOX_HEADS, FOX_HEAD_DIM),
                               fv.reshape(b, s, FOX_HEADS, FOX_HEAD_DIM), log_f)
    o_b = o_b.reshape(b, s, FOX_WIDTH) * jax.nn.sigmoid(fo)

    y = (jax.nn.sigmoid(mga) * (o_a @ w_branch_a)
         + jax.nn.sigmoid(mgb) * (o_b @ w_branch_b))
    return y @ w_out


def swiglu(h, w_gate, w_up, w_down):
    return (jax.nn.silu(h @ w_gate) * (h @ w_up)) @ w_down


def setup_inputs(seed: int = 0) -> dict:
    key = jax.random.key(seed)
    ks = jax.random.split(key, 20)
    f32 = jnp.float32

    def dense(k, fan_in, fan_out):
        return jax.random.normal(k, (DEPTH, fan_in, fan_out), f32) * fan_in ** -0.5

    def gain(k, n):
        return 1.0 + 0.02 * jax.random.normal(k, (DEPTH, n), f32)

    x = jax.random.normal(ks[0], (BATCH, SEQ, D_MODEL), f32)
    dt = jnp.exp(jax.random.uniform(ks[5], (DEPTH, GDN_HEADS), f32,
                                    minval=np.log(1e-3), maxval=np.log(1e-1)))
    return {
        "x": x,
        "norm_mix_w": gain(ks[1], D_MODEL),
        "w_in": dense(ks[2], D_MODEL, D_IN),
        "conv_w": jax.random.normal(ks[3], (DEPTH, CONV_WIDTH, 3 * GDN_WIDTH), f32) * CONV_WIDTH ** -0.5,
        "a_log": jnp.log(jax.random.uniform(ks[4], (DEPTH, GDN_HEADS), f32, minval=1.0, maxval=16.0)),
        "dt_bias": dt + jnp.log(-jnp.expm1(-dt)),
        "gdn_norm_w": gain(ks[6], GDN_HEAD_DIM),
        "fox_f_bias": jax.random.uniform(ks[7], (DEPTH, FOX_HEADS), f32, minval=1.0, maxval=4.0),
        "w_branch_a": dense(ks[8], GDN_WIDTH, D_MODEL),
        "w_branch_b": dense(ks[9], FOX_WIDTH, D_MODEL),
        "w_out": dense(ks[10], D_MODEL, D_MODEL),
        "norm_ffn_w": gain(ks[11], D_MODEL),
        "w_gate": dense(ks[12], D_MODEL, D_FF),
        "w_up": dense(ks[13], D_MODEL, D_FF),
        "w_down": dense(ks[14], D_FF, D_MODEL),
        "norm_final_w": 1.0 + 0.02 * jax.random.normal(ks[15], (D_MODEL,), f32),
    }


def reference(x, norm_mix_w, w_in, conv_w, a_log, dt_bias, gdn_norm_w, fox_f_bias,
              w_branch_a, w_branch_b, w_out, norm_ffn_w, w_gate, w_up, w_down,
              norm_final_w):
    h = x
    for l in range(DEPTH):
        h = h + hybrid_mixer(rms_norm(h, norm_mix_w[l]), w_in[l], conv_w[l], a_log[l],
                             dt_bias[l], gdn_norm_w[l], fox_f_bias[l],
                             w_branch_a[l], w_branch_b[l], w_out[l])
        h = h + swiglu(rms_norm(h, norm_ffn_w[l]), w_gate[l], w_up[l], w_down[l])
    return rms_norm(h, norm_final_w)
```

```python
import functools

import jax
import jax.numpy as jnp
import numpy as np
from jax import lax
from jax.experimental import pallas as pl
from jax.experimental.pallas import tpu as pltpu

F32 = jnp.float32
BF16 = jnp.bfloat16

EPS = 1e-6
GDN_HEADS = 8
GDN_HEAD_DIM = 128
GDN_CHUNK = 64
FOX_HEADS = 16
FOX_HEAD_DIM = 64
LANES = 128
VMEM_LIMIT_BYTES = 56 * 1024 * 1024
NEG_BIG = -1e30


def _cparams(*semantics):
    return pltpu.CompilerParams(dimension_semantics=semantics,
                                vmem_limit_bytes=VMEM_LIMIT_BYTES)


def _split2(x):
    hi = x.astype(BF16)
    lo = (x - hi.astype(F32)).astype(BF16)
    return hi, lo


def _split3(x):
    hi = x.astype(BF16)
    r = x - hi.astype(F32)
    mid = r.astype(BF16)
    lo = (r - mid.astype(F32)).astype(BF16)
    return hi, mid, lo


def _dot(a, b):
    return jnp.dot(a, b, preferred_element_type=F32)


def _dot_exact_rhs(x, m_bf16):
    hi, mid, lo = _split3(x)
    return _dot(hi, m_bf16) + _dot(mid, m_bf16) + _dot(lo, m_bf16)


def _dot_exact_lhs(m_bf16, x):
    hi, mid, lo = _split3(x)
    return _dot(m_bf16, hi) + _dot(m_bf16, mid) + _dot(m_bf16, lo)


def _div_pow2(x, n):
    assert n & (n - 1) == 0
    return jnp.right_shift(x, n.bit_length() - 1)


def _mod_pow2(x, n):
    assert n & (n - 1) == 0
    return jnp.bitwise_and(x, n - 1)


def _softplus(y):
    return jnp.maximum(y, 0.0) + jnp.log(1.0 + jnp.exp(-jnp.abs(y)))


def _sigmoid(y):
    return 1.0 / (1.0 + jnp.exp(-y))


def _silu(y):
    return y * _sigmoid(y)


def _rms_norm(x, w):
    ms = jnp.mean(x * x, axis=-1, keepdims=True)
    return x * lax.rsqrt(ms + EPS) * w


def _norm_matmul_kernel(x_ref, nw_ref, w_ref, o_ref, *, n_chunk):
    h = _rms_norm(x_ref[...], nw_ref[...]).astype(BF16)
    n = o_ref.shape[1]
    for c in range(0, n, n_chunk):
        o_ref[:, c:c + n_chunk] = _dot(h, w_ref[:, c:c + n_chunk]).astype(o_ref.dtype)


def _norm_matmul(x, nw, w, out_dtype, tm):
    t, d = x.shape
    n = w.shape[1]
    return pl.pallas_call(
        functools.partial(_norm_matmul_kernel, n_chunk=min(n, 1024)),
        grid=(t // tm,),
        in_specs=[pl.BlockSpec((tm, d), lambda i: (i, 0)),
                  pl.BlockSpec((1, d), lambda i: (0, 0)),
                  pl.BlockSpec((d, n), lambda i: (0, 0))],
        out_specs=pl.BlockSpec((tm, n), lambda i: (i, 0)),
        out_shape=jax.ShapeDtypeStruct((t, n), out_dtype),
        compiler_params=_cparams("parallel"),
        name="norm_matmul",
    )(x, nw, w)


SMALL_GA = 0
SMALL_GB = GDN_HEADS
SMALL_FF = 2 * GDN_HEADS
BIAS_TERMS = 3
BIAS_HEAD_STRIDE = 8


def _bias_placements():
    n_out = (FOX_HEADS // 2) * LANES
    pq = np.zeros((BIAS_TERMS, LANES, n_out), np.float32)
    pk = np.zeros((BIAS_TERMS, LANES, n_out), np.float32)
    ones_q = np.zeros((1, n_out), np.float32)
    ones_k = np.zeros((1, n_out), np.float32)
    for h in range(FOX_HEADS):
        base = (h // 2) * LANES + (h % 2) * BIAS_HEAD_STRIDE
        for t in range(BIAS_TERMS):
            pq[t, SMALL_FF + h, base + t] = 1.0
            pk[t, SMALL_FF + h, base + BIAS_TERMS + t] = -1.0
            ones_q[0, base + BIAS_TERMS + t] = 1.0
            ones_k[0, base + t] = 1.0
    return pq, pk, ones_q, ones_k


def _fox_bias_kernel(small_ref, fb_ref, pq_ref, pk_ref, oq_ref, ok_ref, eq_ref, ek_ref,
                     carry_ref):
    tm = small_ref.shape[0]

    @pl.when(pl.program_id(1) == 0)
    def _():
        carry_ref[...] = jnp.zeros_like(carry_ref)

    z = small_ref[...] + fb_ref[...]
    log_f = -_softplus(-z)
    row = lax.broadcasted_iota(jnp.int32, (tm, tm), 0)
    col = lax.broadcasted_iota(jnp.int32, (tm, tm), 1)
    tril = jnp.where(row >= col, 1.0, 0.0).astype(BF16)
    cum = _dot_exact_lhs(tril, log_f) + carry_ref[0:1, :]
    carry_ref[...] = jnp.broadcast_to(cum[tm - 1:tm, :], carry_ref.shape)
    terms = _split3(cum)
    eq = oq_ref[...]
    ek = ok_ref[...]
    for t in range(BIAS_TERMS):
        eq = eq + _dot(terms[t], pq_ref[t])
        ek = ek + _dot(terms[t], pk_ref[t])
    eq_ref[...] = eq.astype(BF16)
    ek_ref[...] = ek.astype(BF16)


def _fox_bias(small, fb_row, batch, seq, tm):
    t = small.shape[0]
    pq, pk, ones_q, ones_k = _bias_placements()
    n_out = pq.shape[2]
    nt = seq // tm
    const2 = lambda b, i: (0, 0)
    const3 = lambda b, i: (0, 0, 0)
    return pl.pallas_call(
        _fox_bias_kernel,
        grid=(batch, nt),
        in_specs=[pl.BlockSpec((tm, LANES), lambda b, i: (b * nt + i, 0)),
                  pl.BlockSpec((1, LANES), const2),
                  pl.BlockSpec((BIAS_TERMS, LANES, n_out), const3),
                  pl.BlockSpec((BIAS_TERMS, LANES, n_out), const3),
                  pl.BlockSpec((1, n_out), const2),
                  pl.BlockSpec((1, n_out), const2)],
        out_specs=[pl.BlockSpec((tm, n_out), lambda b, i: (b * nt + i, 0)),
                   pl.BlockSpec((tm, n_out), lambda b, i: (b * nt + i, 0))],
        out_shape=[jax.ShapeDtypeStruct((t, n_out), BF16),
                   jax.ShapeDtypeStruct((t, n_out), BF16)],
        scratch_shapes=[pltpu.VMEM((8, LANES), F32)],
        compiler_params=_cparams("parallel", "arbitrary"),
        name="fox_bias",
    )(small, fb_row, jnp.asarray(pq, BF16), jnp.asarray(pk, BF16),
      jnp.asarray(ones_q), jnp.asarray(ones_k))


def _fox_attention_kernel(q_ref, eq_ref, k_ref, ek_ref, v_ref, fo_ref, o_ref, *, tq):
    i = pl.program_id(2)
    lane = lax.broadcasted_iota(jnp.int32, (1, LANES), 1)
    head_a = lane < FOX_HEAD_DIM
    bias_a = lane < BIAS_HEAD_STRIDE
    zero = jnp.zeros((), BF16)
    q = q_ref[...] * jnp.asarray(FOX_HEAD_DIM ** -0.5, BF16)
    eq = eq_ref[...]
    lhs = (jnp.concatenate([jnp.where(head_a, q, zero), jnp.where(bias_a, eq, zero)], axis=1),
           jnp.concatenate([jnp.where(head_a, zero, q), jnp.where(bias_a, zero, eq)], axis=1))
    one = jnp.ones((), BF16)

    def tile(j, carry, masked):
        start = pl.multiple_of(j * tq, tq)
        kk = jnp.concatenate([k_ref[pl.ds(start, tq), :], ek_ref[pl.ds(start, tq), :]], axis=1)
        v = v_ref[pl.ds(start, tq), :]
        vs = (jnp.where(head_a, v, one), jnp.where(head_a, one, v))
        out = []
        for h in range(2):
            m, acc = carry[h]
            s = lax.dot_general(lhs[h], kk, (((1,), (1,)), ((), ())),
                                preferred_element_type=F32)
            if masked:
                r = lax.broadcasted_iota(jnp.int32, (tq, tq), 0)
                c = lax.broadcasted_iota(jnp.int32, (tq, tq), 1)
                s = jnp.where(c <= r, s, NEG_BIG)
            m_new = jnp.maximum(m, jnp.max(s, axis=-1, keepdims=True))
            alpha = jnp.exp(m - m_new)
            p = jnp.exp(s - m_new).astype(BF16)
            acc = alpha * acc + _dot(p, vs[h])
            out.append((m_new, acc))
        return tuple(out)

    init = tuple((jnp.full((tq, 1), NEG_BIG, F32), jnp.zeros((tq, LANES), F32))
                 for _ in range(2))
    carry = lax.fori_loop(0, i, lambda j, c: tile(j, c, False), init)
    (_, acc_a), (_, acc_b) = tile(i, carry, True)
    num = jnp.where(head_a, acc_a, acc_b)
    den = pltpu.roll(jnp.where(head_a, acc_b, acc_a), FOX_HEAD_DIM, axis=1)
    gate = _sigmoid(fo_ref[...].astype(F32))
    o_ref[...] = (num / den * gate).astype(o_ref.dtype)


def _fox_attention(fox, eq, ek, batch, seq, tq):
    t = fox.shape[0]
    pairs = FOX_HEADS // 2
    nq = seq // tq
    return pl.pallas_call(
        functools.partial(_fox_attention_kernel, tq=tq),
        grid=(batch, pairs, nq),
        in_specs=[pl.BlockSpec((tq, LANES), lambda b, p, i: (b * nq + i, p)),
                  pl.BlockSpec((tq, LANES), lambda b, p, i: (b * nq + i, p)),
                  pl.BlockSpec((seq, LANES), lambda b, p, i: (b, pairs + p)),
                  pl.BlockSpec((seq, LANES), lambda b, p, i: (b, p)),
                  pl.BlockSpec((seq, LANES), lambda b, p, i: (b, 2 * pairs + p)),
                  pl.BlockSpec((tq, LANES), lambda b, p, i: (b * nq + i, 3 * pairs + p))],
        out_specs=pl.BlockSpec((tq, LANES), lambda b, p, i: (b * nq + i, p)),
        out_shape=jax.ShapeDtypeStruct((t, pairs * LANES), BF16),
        compiler_params=_cparams("parallel", "parallel", "arbitrary"),
        name="fox_attention",
    )(fox, eq, fox, ek, fox, fox)


GDN_WIDTH = GDN_HEADS * GDN_HEAD_DIM
GDN_GROUP = 4
GDN_CAT = GDN_HEADS * GDN_CHUNK
CONV_HALO = 8


def _gdn_expanders():
    e_g128 = np.zeros((LANES, GDN_WIDTH), np.float32)
    e_b128 = np.zeros((LANES, GDN_WIDTH), np.float32)
    e_g64 = np.zeros((LANES, GDN_CAT), np.float32)
    for h in range(GDN_HEADS):
        e_g128[SMALL_GA + h, h * GDN_HEAD_DIM:(h + 1) * GDN_HEAD_DIM] = 1.0
        e_b128[SMALL_GB + h, h * GDN_HEAD_DIM:(h + 1) * GDN_HEAD_DIM] = 1.0
        e_g64[SMALL_GA + h, h * GDN_CHUNK:(h + 1) * GDN_CHUNK] = 1.0
    return e_g128, e_b128, e_g64


def _block_diag_rows(x, n_blocks, rows_per_block, cols_per_block):
    tiled = jnp.concatenate([x] * n_blocks, axis=0)
    r = _div_pow2(lax.broadcasted_iota(jnp.int32, tiled.shape, 0), rows_per_block)
    c = _div_pow2(lax.broadcasted_iota(jnp.int32, tiled.shape, 1), cols_per_block)
    return jnp.where(r == c, tiled, jnp.zeros((), tiled.dtype))


def _headwise_matmul(a, b):
    a_hi, a_lo = _split2(a)
    b_hi, b_lo = _split2(b)
    bd_hi = _block_diag_rows(b_hi, GDN_GROUP, GDN_CHUNK, GDN_CHUNK)
    bd_lo = _block_diag_rows(b_lo, GDN_GROUP, GDN_CHUNK, GDN_CHUNK)
    return _dot(a_hi, bd_hi) + _dot(a_lo, bd_hi) + _dot(a_hi, bd_lo)


def _unit_lower_inverse(l_cat):
    c, n = l_cat.shape
    r = lax.broadcasted_iota(jnp.int32, (c, n), 0)
    j = _mod_pow2(lax.broadcasted_iota(jnp.int32, (c, n), 1), GDN_CHUNK)
    inv = jnp.where(r == j, 1.0, 0.0) - l_cat
    power = l_cat
    k = 1
    while k < GDN_CHUNK // 2:
        power = _headwise_matmul(power, power)
        inv = inv + _headwise_matmul(inv, power)
        k *= 2
    return inv


def _gdn_kernel(q_ref, k_ref, v_ref, z_ref, small_ref, cw_ref, alog_ref, dtb_ref, nw_ref,
                eg128_ref, eb128_ref, eg64_ref, o_ref, state_ref, win_ref):
    c = GDN_CHUNK
    n_conv = cw_ref.shape[1]

    @pl.when(pl.program_id(1) == 0)
    def _():
        state_ref[...] = jnp.zeros_like(state_ref)
        win_ref[...] = jnp.zeros_like(win_ref)

    conv = []
    for s, ref in enumerate((q_ref, k_ref, v_ref)):
        win_ref[s, CONV_HALO:CONV_HALO + c, :] = ref[...]
        acc = None
        for j in range(n_conv):
            shift = n_conv - 1 - j
            term = win_ref[s, pl.ds(CONV_HALO - shift, c), :] * cw_ref[s, j:j + 1, :]
            acc = term if acc is None else acc + term
        win_ref[s, 0:CONV_HALO, :] = win_ref[s, c:c + CONV_HALO, :]
        conv.append(_silu(acc))
    cq, ck, cv = conv

    def l2n(x):
        parts = []
        for h in range(GDN_HEADS):
            xh = x[:, h * GDN_HEAD_DIM:(h + 1) * GDN_HEAD_DIM]
            ss = jnp.sum(xh * xh, axis=-1, keepdims=True)
            parts.append(xh * lax.rsqrt(ss + EPS))
        return jnp.concatenate(parts, axis=1)

    qn = l2n(cq)
    kn = l2n(ck)

    small = small_ref[...]
    g_tok = -jnp.exp(alog_ref[...]) * _softplus(small + dtb_ref[...])
    beta_tok = _sigmoid(small)
    row = lax.broadcasted_iota(jnp.int32, (c, c), 0)
    col = lax.broadcasted_iota(jnp.int32, (c, c), 1)
    tril = jnp.where(row >= col, 1.0, 0.0).astype(BF16)
    gc_tok = _dot_exact_lhs(tril, g_tok)
    gc128 = _dot_exact_rhs(gc_tok, eg128_ref[...])
    gc64 = _dot_exact_rhs(gc_tok, eg64_ref[...])
    beta128 = _dot_exact_rhs(beta_tok, eb128_ref[...])

    r_cat = lax.broadcasted_iota(jnp.int32, (c, GDN_CAT), 0)
    j_cat = _mod_pow2(lax.broadcasted_iota(jnp.int32, (c, GDN_CAT), 1), c)
    gc_row = jnp.sum(jnp.where(r_cat == j_cat, gc64, 0.0), axis=0, keepdims=True)
    tri_cat = r_cat >= j_cat
    decay = jnp.where(tri_cat, jnp.exp(jnp.where(tri_cat, gc64 - gc_row, 0.0)), 0.0)

    g_last = gc128[c - 1:c, :]
    exp_gc = jnp.exp(gc128)
    kb = kn * beta128
    vb = cv * beta128
    kbg = kb * exp_gc
    qs = qn * (GDN_HEAD_DIM ** -0.5)
    qg = (qs * exp_gc).astype(BF16)
    k_dec = (kn * jnp.exp(g_last - gc128)).astype(BF16)
    kb16 = kb.astype(BF16)
    qs16 = qs.astype(BF16)
    kn16 = kn.astype(BF16)

    group_w = GDN_GROUP * GDN_HEAD_DIM
    group_c = GDN_GROUP * c
    inv_parts, qk_parts = [], []
    for gidx in range(GDN_HEADS // GDN_GROUP):
        ksl = slice(gidx * group_w, (gidx + 1) * group_w)
        csl = slice(gidx * group_c, (gidx + 1) * group_c)
        bd_k = _block_diag_rows(kn16[:, ksl], GDN_GROUP, c, GDN_HEAD_DIM)
        contract_last = (((1,), (1,)), ((), ()))
        a_cat = lax.dot_general(kb16[:, ksl], bd_k, contract_last, preferred_element_type=F32)
        qk_cat = lax.dot_general(qs16[:, ksl], bd_k, contract_last, preferred_element_type=F32)
        dec = decay[:, csl]
        strict = (r_cat > j_cat)[:, csl]
        l_cat = jnp.where(strict, a_cat * dec, 0.0)
        inv_parts.append(_unit_lower_inverse(l_cat))
        qk_parts.append((qk_cat * dec).astype(BF16))
    inv_cat = jnp.concatenate(inv_parts, axis=1)
    qk_cat = jnp.concatenate(qk_parts, axis=1)

    pair_w = 2 * GDN_HEAD_DIM
    lane_p = lax.broadcasted_iota(jnp.int32, (1, pair_w), 1)
    first = lane_p < GDN_HEAD_DIM
    bm_r = _div_pow2(lax.broadcasted_iota(jnp.int32, (pair_w, pair_w), 0), GDN_HEAD_DIM)
    bm_c = _div_pow2(lax.broadcasted_iota(jnp.int32, (pair_w, pair_w), 1), GDN_HEAD_DIM)
    same_head = bm_r == bm_c
    nw = nw_ref[...]
    for p in range(GDN_HEADS // 2):
        wsl = slice(p * pair_w, (p + 1) * pair_w)
        inv_p = inv_cat[:, p * 2 * c:(p + 1) * 2 * c]
        vb_p, kbg_p = vb[:, wsl], kbg[:, wsl]
        rhs = jnp.concatenate(
            [jnp.concatenate([jnp.where(first, vb_p, 0.0), jnp.where(first, kbg_p, 0.0)], axis=1),
             jnp.concatenate([jnp.where(first, 0.0, vb_p), jnp.where(first, 0.0, kbg_p)], axis=1)],
            axis=0)
        i_hi, i_lo = _split2(inv_p)
        r_hi, r_lo = _split2(rhs)
        sol = _dot(i_hi, r_hi) + _dot(i_lo, r_hi) + _dot(i_hi, r_lo)
        u_p = sol[:, :pair_w]
        w_p = sol[:, pair_w:]

        state = state_ref[p]
        s16 = state.astype(BF16)
        v_new = u_p - _dot(w_p.astype(BF16), s16)
        v16 = v_new.astype(BF16)
        v_bd = jnp.concatenate([jnp.where(first, v16, jnp.zeros((), BF16)),
                                jnp.where(first, jnp.zeros((), BF16), v16)], axis=0)
        o = _dot(qg[:, wsl], s16) + _dot(qk_cat[:, p * 2 * c:(p + 1) * 2 * c], v_bd)
        upd = lax.dot_general(k_dec[:, wsl], v16, (((0,), (0,)), ((), ())),
                              preferred_element_type=F32)
        state_ref[p] = state * jnp.exp(g_last[:, wsl]) + jnp.where(same_head, upd, 0.0)

        outs = []
        for h in range(2):
            oh = o[:, h * GDN_HEAD_DIM:(h + 1) * GDN_HEAD_DIM]
            ms = jnp.mean(oh * oh, axis=-1, keepdims=True)
            outs.append(oh * lax.rsqrt(ms + EPS) * nw)
        o_ref[:, wsl] = (jnp.concatenate(outs, axis=1) * _silu(z_ref[:, wsl])).astype(o_ref.dtype)


def _gdn(gdn, small, conv_w3, alog_row, dtb_row, nw_row, batch, seq):
    t = gdn.shape[0]
    c = GDN_CHUNK
    nc = seq // c
    w = GDN_WIDTH
    e_g128, e_b128, e_g64 = _gdn_expanders()
    n_conv = conv_w3.shape[1]
    tok = lambda col: (lambda b, n: (b * nc + n, col))
    const2 = lambda b, n: (0, 0)
    return pl.pallas_call(
        _gdn_kernel,
        grid=(batch, nc),
        in_specs=[pl.BlockSpec((c, w), tok(0)),
                  pl.BlockSpec((c, w), tok(1)),
                  pl.BlockSpec((c, w), tok(2)),
                  pl.BlockSpec((c, w), tok(3)),
                  pl.BlockSpec((c, LANES), tok(0)),
                  pl.BlockSpec((3, n_conv, w), lambda b, n: (0, 0, 0)),
                  pl.BlockSpec((1, LANES), const2),
                  pl.BlockSpec((1, LANES), const2),
                  pl.BlockSpec((1, GDN_HEAD_DIM), const2),
                  pl.BlockSpec((LANES, w), const2),
                  pl.BlockSpec((LANES, w), const2),
                  pl.BlockSpec((LANES, GDN_CAT), const2)],
        out_specs=pl.BlockSpec((c, w), tok(0)),
        out_shape=jax.ShapeDtypeStruct((t, w), BF16),
        scratch_shapes=[pltpu.VMEM((GDN_HEADS // 2, 2 * GDN_HEAD_DIM, 2 * GDN_HEAD_DIM), F32),
                        pltpu.VMEM((3, c + CONV_HALO, w), F32)],
        compiler_params=_cparams("parallel", "arbitrary"),
        name="gdn",
    )(gdn, gdn, gdn, gdn, small, conv_w3, alog_row, dtb_row, nw_row,
      jnp.asarray(e_g128, BF16), jnp.asarray(e_b128, BF16), jnp.asarray(e_g64, BF16))


def _merge_out_kernel(x_ref, oa_ref, ob_ref, ga_ref, gb_ref, wa_ref, wb_ref, wo_ref, o_ref):
    ya = _dot(oa_ref[...], wa_ref[...])
    yb = _dot(ob_ref[...], wb_ref[...])
    y = (_sigmoid(ga_ref[...].astype(F32)) * ya + _sigmoid(gb_ref[...].astype(F32)) * yb)
    o_ref[...] = x_ref[...] + _dot(y.astype(BF16), wo_ref[...])


def _merge_out(x, o_a, o_b, mg, w_a, w_b, w_o, tm):
    t, d = x.shape
    tokd = pl.BlockSpec((tm, d), lambda i: (i, 0))
    wspec = pl.BlockSpec((d, d), lambda i: (0, 0))
    return pl.pallas_call(
        _merge_out_kernel,
        grid=(t // tm,),
        in_specs=[tokd, tokd, tokd,
                  pl.BlockSpec((tm, d), lambda i: (i, 0)),
                  pl.BlockSpec((tm, d), lambda i: (i, 1)),
                  wspec, wspec, wspec],
        out_specs=tokd,
        out_shape=jax.ShapeDtypeStruct((t, d), F32),
        compiler_params=_cparams("parallel"),
        name="merge_out",
    )(x, o_a, o_b, mg, mg, w_a, w_b, w_o)


def _ffn_kernel(x_ref, nw_ref, wg_ref, wu_ref, wd_ref, fw_ref, o_ref, *, final_norm):
    x = x_ref[...]
    h = _rms_norm(x, nw_ref[...]).astype(BF16)
    a = (_silu(_dot(h, wg_ref[...])) * _dot(h, wu_ref[...])).astype(BF16)
    y = x + _dot(a, wd_ref[...])
    if final_norm:
        y = _rms_norm(y, fw_ref[...])
    o_ref[...] = y


def _ffn(x, nw, w_g, w_u, w_d, fw, final_norm, tm):
    t, d = x.shape
    f = w_g.shape[1]
    tokd = pl.BlockSpec((tm, d), lambda i: (i, 0))
    rowd = pl.BlockSpec((1, d), lambda i: (0, 0))
    return pl.pallas_call(
        functools.partial(_ffn_kernel, final_norm=final_norm),
        grid=(t // tm,),
        in_specs=[tokd, rowd,
                  pl.BlockSpec((d, f), lambda i: (0, 0)),
                  pl.BlockSpec((d, f), lambda i: (0, 0)),
                  pl.BlockSpec((f, d), lambda i: (0, 0)),
                  rowd],
        out_specs=tokd,
        out_shape=jax.ShapeDtypeStruct((t, d), F32),
        compiler_params=_cparams("parallel"),
        name="ffn",
    )(x, nw, w_g, w_u, w_d, fw)


def _pad_row(v, offset):
    return jnp.zeros((1, LANES), F32).at[0, offset:offset + v.shape[0]].set(v.astype(F32))


def _layer(x, batch, seq, norm_mix_w, w_in, conv_w, a_log, dt_bias, gdn_norm_w, fox_f_bias,
           w_branch_a, w_branch_b, w_out, norm_ffn_w, w_gate, w_up, w_down, final_w, final_norm):
    d = x.shape[1]
    gw, fw = GDN_WIDTH, FOX_HEADS * FOX_HEAD_DIM
    sizes = (gw, gw, gw, gw, GDN_HEADS, GDN_HEADS, fw, fw, fw, FOX_HEADS, fw, d, d)
    offs = np.concatenate([[0], np.cumsum(sizes)])
    col = lambda i: w_in[:, offs[i]:offs[i + 1]]
    w_gdn = jnp.concatenate([col(0), col(1), col(2), col(3)], axis=1).astype(BF16)
    w_fox = jnp.concatenate([col(6), col(7), col(8), col(10)], axis=1).astype(BF16)
    w_mg = jnp.concatenate([col(11), col(12)], axis=1).astype(BF16)
    n_small = 2 * GDN_HEADS + FOX_HEADS
    w_small = jnp.concatenate([col(4), col(5), col(9), jnp.zeros((d, LANES - n_small), F32)],
                              axis=1).astype(BF16)
    nw = norm_mix_w.reshape(1, d)

    gdn = _norm_matmul(x, nw, w_gdn, F32, tm=512)
    fox = _norm_matmul(x, nw, w_fox, BF16, tm=512)
    mg = _norm_matmul(x, nw, w_mg, BF16, tm=512)
    small = _norm_matmul(x, nw, w_small, F32, tm=512)

    eq, ek = _fox_bias(small, _pad_row(fox_f_bias, SMALL_FF), batch, seq, tm=min(seq, 512))
    o_b = _fox_attention(fox, eq, ek, batch, seq, tq=min(seq, 512))

    conv_w3 = conv_w.reshape(conv_w.shape[0], 3, gw).transpose(1, 0, 2)
    o_a = _gdn(gdn, small, conv_w3, _pad_row(a_log, SMALL_GA), _pad_row(dt_bias, SMALL_GA),
               gdn_norm_w.reshape(1, GDN_HEAD_DIM), batch, seq)

    x1 = _merge_out(x, o_a, o_b, mg, w_branch_a.astype(BF16), w_branch_b.astype(BF16),
                    w_out.astype(BF16), tm=512)
    return _ffn(x1, norm_ffn_w.reshape(1, d), w_gate.astype(BF16), w_up.astype(BF16),
                w_down.astype(BF16), final_w.reshape(1, d), final_norm, tm=256)


def kernel(x, norm_mix_w, w_in, conv_w, a_log, dt_bias, gdn_norm_w, fox_f_bias, w_branch_a,
           w_branch_b, w_out, norm_ffn_w, w_gate, w_up, w_down, norm_final_w):
    batch, seq, d = x.shape
    depth = w_in.shape[0]
    h = x.reshape(batch * seq, d)
    for l in range(depth):
        h = _layer(h, batch, seq, norm_mix_w[l], w_in[l], conv_w[l], a_log[l], dt_bias[l],
                   gdn_norm_w[l], fox_f_bias[l], w_branch_a[l], w_branch_b[l], w_out[l],
                   norm_ffn_w[l], w_gate[l], w_up[l], w_down[l], norm_final_w,
                   final_norm=(l == depth - 1))
    return h.reshape(batch, seq, d)
```

```python
import functools

import jax
import jax.numpy as jnp
import numpy as np
from jax import lax
from jax.experimental import pallas as pl
from jax.experimental.pallas import tpu as pltpu

F32 = jnp.float32
BF16 = jnp.bfloat16

EPS = 1e-6
GDN_HEADS = 8
GDN_HEAD_DIM = 128
GDN_CHUNK = 64
FOX_HEADS = 16
FOX_HEAD_DIM = 64
LANES = 128
VMEM_LIMIT_BYTES = 56 * 1024 * 1024
NEG_BIG = -1e30


def _cparams(*semantics):
    return pltpu.CompilerParams(dimension_semantics=semantics,
                                vmem_limit_bytes=VMEM_LIMIT_BYTES)


def _split2(x):
    hi = x.astype(BF16)
    lo = (x - hi.astype(F32)).astype(BF16)
    return hi, lo


def _split3(x):
    hi = x.astype(BF16)
    r = x - hi.astype(F32)
    mid = r.astype(BF16)
    lo = (r - mid.astype(F32)).astype(BF16)
    return hi, mid, lo


def _dot(a, b):
    return jnp.dot(a, b, preferred_element_type=F32)


def _dot_exact_rhs(x, m_bf16):
    n = x.shape[0]
    y = _dot(jnp.concatenate(_split3(x), axis=0), m_bf16)
    return y[:n] + y[n:2 * n] + y[2 * n:]


def _dot_exact_lhs(m_bf16, x):
    n = x.shape[1]
    y = _dot(m_bf16, jnp.concatenate(_split3(x), axis=1))
    return y[:, :n] + y[:, n:2 * n] + y[:, 2 * n:]


def _div_pow2(x, n):
    assert n & (n - 1) == 0
    return jnp.right_shift(x, n.bit_length() - 1)


def _mod_pow2(x, n):
    assert n & (n - 1) == 0
    return jnp.bitwise_and(x, n - 1)


def _softplus(y):
    return jnp.maximum(y, 0.0) + jnp.log(1.0 + jnp.exp(-jnp.abs(y)))


def _sigmoid(y):
    return 1.0 / (1.0 + jnp.exp(-y))


def _silu(y):
    return y * _sigmoid(y)


def _rms_norm(x, w):
    ms = jnp.mean(x * x, axis=-1, keepdims=True)
    return x * lax.rsqrt(ms + EPS) * w


def _norm_matmul_kernel(x_ref, nw_ref, w_ref, o_ref, *, n_chunk):
    h = _rms_norm(x_ref[...], nw_ref[...]).astype(BF16)
    n = o_ref.shape[1]
    for c in range(0, n, n_chunk):
        o_ref[:, c:c + n_chunk] = _dot(h, w_ref[:, c:c + n_chunk]).astype(o_ref.dtype)


def _norm_matmul(x, nw, w, out_dtype, tm):
    t, d = x.shape
    n = w.shape[1]
    return pl.pallas_call(
        functools.partial(_norm_matmul_kernel, n_chunk=min(n, 1024)),
        grid=(t // tm,),
        in_specs=[pl.BlockSpec((tm, d), lambda i: (i, 0)),
                  pl.BlockSpec((1, d), lambda i: (0, 0)),
                  pl.BlockSpec((d, n), lambda i: (0, 0))],
        out_specs=pl.BlockSpec((tm, n), lambda i: (i, 0)),
        out_shape=jax.ShapeDtypeStruct((t, n), out_dtype),
        compiler_params=_cparams("parallel"),
        name="norm_matmul",
    )(x, nw, w)


SMALL_GA = 0
SMALL_GB = GDN_HEADS
SMALL_FF = 2 * GDN_HEADS
BIAS_TERMS = 3
BIAS_HEAD_STRIDE = 8


def _bias_placements():
    n_out = (FOX_HEADS // 2) * LANES
    pq = np.zeros((BIAS_TERMS, LANES, n_out), np.float32)
    pk = np.zeros((BIAS_TERMS, LANES, n_out), np.float32)
    ones_q = np.zeros((1, n_out), np.float32)
    ones_k = np.zeros((1, n_out), np.float32)
    for h in range(FOX_HEADS):
        base = (h // 2) * LANES + (h % 2) * BIAS_HEAD_STRIDE
        for t in range(BIAS_TERMS):
            pq[t, SMALL_FF + h, base + t] = 1.0
            pk[t, SMALL_FF + h, base + BIAS_TERMS + t] = -1.0
            ones_q[0, base + BIAS_TERMS + t] = 1.0
            ones_k[0, base + t] = 1.0
    return pq, pk, ones_q, ones_k


def _fox_bias_kernel(small_ref, fb_ref, pq_ref, pk_ref, oq_ref, ok_ref, eq_ref, ek_ref,
                     carry_ref):
    tm = small_ref.shape[0]

    @pl.when(pl.program_id(1) == 0)
    def _():
        carry_ref[...] = jnp.zeros_like(carry_ref)

    z = small_ref[...] + fb_ref[...]
    log_f = -_softplus(-z)
    row = lax.broadcasted_iota(jnp.int32, (tm, tm), 0)
    col = lax.broadcasted_iota(jnp.int32, (tm, tm), 1)
    tril = jnp.where(row >= col, 1.0, 0.0).astype(BF16)
    cum = _dot_exact_lhs(tril, log_f) + carry_ref[0:1, :]
    carry_ref[...] = jnp.broadcast_to(cum[tm - 1:tm, :], carry_ref.shape)
    terms = _split3(cum)
    eq = oq_ref[...]
    ek = ok_ref[...]
    for t in range(BIAS_TERMS):
        eq = eq + _dot(terms[t], pq_ref[t])
        ek = ek + _dot(terms[t], pk_ref[t])
    eq_ref[...] = eq.astype(BF16)
    ek_ref[...] = ek.astype(BF16)


def _fox_bias(small, fb_row, batch, seq, tm):
    t = small.shape[0]
    pq, pk, ones_q, ones_k = _bias_placements()
    n_out = pq.shape[2]
    nt = seq // tm
    const2 = lambda b, i: (0, 0)
    const3 = lambda b, i: (0, 0, 0)
    return pl.pallas_call(
        _fox_bias_kernel,
        grid=(batch, nt),
        in_specs=[pl.BlockSpec((tm, LANES), lambda b, i: (b * nt + i, 0)),
                  pl.BlockSpec((1, LANES), const2),
                  pl.BlockSpec((BIAS_TERMS, LANES, n_out), const3),
                  pl.BlockSpec((BIAS_TERMS, LANES, n_out), const3),
                  pl.BlockSpec((1, n_out), const2),
                  pl.BlockSpec((1, n_out), const2)],
        out_specs=[pl.BlockSpec((tm, n_out), lambda b, i: (b * nt + i, 0)),
                   pl.BlockSpec((tm, n_out), lambda b, i: (b * nt + i, 0))],
        out_shape=[jax.ShapeDtypeStruct((t, n_out), BF16),
                   jax.ShapeDtypeStruct((t, n_out), BF16)],
        scratch_shapes=[pltpu.VMEM((8, LANES), F32)],
        compiler_params=_cparams("parallel", "arbitrary"),
        name="fox_bias",
    )(small, fb_row, jnp.asarray(pq, BF16), jnp.asarray(pk, BF16),
      jnp.asarray(ones_q), jnp.asarray(ones_k))


FOX_PAIRS_PER_STEP = 2


def _fox_attention_kernel(q_ref, eq_ref, k_ref, ek_ref, v_ref, fo_ref, o_ref, *, tq):
    i = pl.program_id(2)
    n_pairs = q_ref.shape[1] // LANES
    lane = lax.broadcasted_iota(jnp.int32, (1, LANES), 1)
    head_a = lane < FOX_HEAD_DIM
    bias_a = lane < BIAS_HEAD_STRIDE
    zero = jnp.zeros((), BF16)
    one = jnp.ones((), BF16)
    lhs = []
    for p in range(n_pairs):
        psl = slice(p * LANES, (p + 1) * LANES)
        q = q_ref[:, psl] * jnp.asarray(FOX_HEAD_DIM ** -0.5, BF16)
        eq = eq_ref[:, psl]
        lhs.append(jnp.concatenate([jnp.where(head_a, q, zero), jnp.where(bias_a, eq, zero)], axis=1))
        lhs.append(jnp.concatenate([jnp.where(head_a, zero, q), jnp.where(bias_a, zero, eq)], axis=1))

    def tile(j, carry, masked):
        start = pl.multiple_of(j * tq, tq)
        out = []
        for p in range(n_pairs):
            psl = slice(p * LANES, (p + 1) * LANES)
            kk = jnp.concatenate([k_ref[pl.ds(start, tq), psl], ek_ref[pl.ds(start, tq), psl]], axis=1)
            v = v_ref[pl.ds(start, tq), psl]
            vs = (jnp.where(head_a, v, one), jnp.where(head_a, one, v))
            for h in range(2):
                m, acc = carry[2 * p + h]
                s = lax.dot_general(lhs[2 * p + h], kk, (((1,), (1,)), ((), ())),
                                    preferred_element_type=F32)
                if masked:
                    r = lax.broadcasted_iota(jnp.int32, (tq, tq), 0)
                    c = lax.broadcasted_iota(jnp.int32, (tq, tq), 1)
                    s = jnp.where(c <= r, s, NEG_BIG)
                m_new = jnp.maximum(m, jnp.max(s, axis=-1, keepdims=True))
                alpha = jnp.exp(m - m_new)
                prob = jnp.exp(s - m_new).astype(BF16)
                acc = alpha * acc + _dot(prob, vs[h])
                out.append((m_new, acc))
        return tuple(out)

    init = tuple((jnp.full((tq, 1), NEG_BIG, F32), jnp.zeros((tq, LANES), F32))
                 for _ in range(2 * n_pairs))
    carry = lax.fori_loop(0, i, lambda j, c: tile(j, c, False), init)
    final = tile(i, carry, True)
    for p in range(n_pairs):
        psl = slice(p * LANES, (p + 1) * LANES)
        acc_a, acc_b = final[2 * p][1], final[2 * p + 1][1]
        num = jnp.where(head_a, acc_a, acc_b)
        den = pltpu.roll(jnp.where(head_a, acc_b, acc_a), FOX_HEAD_DIM, axis=1)
        gate = _sigmoid(fo_ref[:, psl].astype(F32))
        o_ref[:, psl] = (num / den * gate).astype(o_ref.dtype)


def _fox_attention(fox, eq, ek, batch, seq, tq):
    t = fox.shape[0]
    pairs = FOX_HEADS // 2
    steps = pairs // FOX_PAIRS_PER_STEP
    width = FOX_PAIRS_PER_STEP * LANES
    nq = seq // tq
    return pl.pallas_call(
        functools.partial(_fox_attention_kernel, tq=tq),
        grid=(batch, steps, nq),
        in_specs=[pl.BlockSpec((tq, width), lambda b, p, i: (b * nq + i, p)),
                  pl.BlockSpec((tq, width), lambda b, p, i: (b * nq + i, p)),
                  pl.BlockSpec((seq, width), lambda b, p, i: (b, steps + p)),
                  pl.BlockSpec((seq, width), lambda b, p, i: (b, p)),
                  pl.BlockSpec((seq, width), lambda b, p, i: (b, 2 * steps + p)),
                  pl.BlockSpec((tq, width), lambda b, p, i: (b * nq + i, 3 * steps + p))],
        out_specs=pl.BlockSpec((tq, width), lambda b, p, i: (b * nq + i, p)),
        out_shape=jax.ShapeDtypeStruct((t, pairs * LANES), BF16),
        compiler_params=_cparams("parallel", "parallel", "arbitrary"),
        name="fox_attention",
    )(fox, eq, fox, ek, fox, fox)


GDN_WIDTH = GDN_HEADS * GDN_HEAD_DIM
GDN_GROUP = 4
GDN_CAT = GDN_HEADS * GDN_CHUNK
GDN_PAIRS = GDN_HEADS // 2
GDN_PAIR = 2 * GDN_HEAD_DIM
GDN_PREP_CHUNKS = 2
CONV_HALO = 8


def _gdn_expanders():
    e_g128 = np.zeros((LANES, GDN_WIDTH), np.float32)
    e_b128 = np.zeros((LANES, GDN_WIDTH), np.float32)
    e_g64 = np.zeros((LANES, GDN_CAT), np.float32)
    for h in range(GDN_HEADS):
        e_g128[SMALL_GA + h, h * GDN_HEAD_DIM:(h + 1) * GDN_HEAD_DIM] = 1.0
        e_b128[SMALL_GB + h, h * GDN_HEAD_DIM:(h + 1) * GDN_HEAD_DIM] = 1.0
        e_g64[SMALL_GA + h, h * GDN_CHUNK:(h + 1) * GDN_CHUNK] = 1.0
    return e_g128, e_b128, e_g64


def _block_diag_rows(x, n_blocks, rows_per_block, cols_per_block):
    tiled = jnp.concatenate([x] * n_blocks, axis=0)
    r = _div_pow2(lax.broadcasted_iota(jnp.int32, tiled.shape, 0), rows_per_block)
    c = _div_pow2(lax.broadcasted_iota(jnp.int32, tiled.shape, 1), cols_per_block)
    return jnp.where(r == c, tiled, jnp.zeros((), tiled.dtype))


def _rows(x, i, n):
    return x[i * n:(i + 1) * n]


def _headwise_products(lhs, b):
    c = b.shape[0]
    n = len(lhs)
    b_hi, b_lo = _split2(b)
    bd_hi = _block_diag_rows(b_hi, GDN_GROUP, c, c)
    bd_lo = _block_diag_rows(b_lo, GDN_GROUP, c, c)
    parts = [_split2(x) for x in lhs]
    his = [p[0] for p in parts]
    los = [p[1] for p in parts]
    top = _dot(jnp.concatenate(his + los, axis=0), bd_hi)
    bot = _dot(jnp.concatenate(his, axis=0), bd_lo)
    return [_rows(top, i, c) + _rows(top, n + i, c) + _rows(bot, i, c) for i in range(n)]


def _unit_lower_inverse(l_cat):
    c, n = l_cat.shape
    r = lax.broadcasted_iota(jnp.int32, (c, n), 0)
    j = _mod_pow2(lax.broadcasted_iota(jnp.int32, (c, n), 1), c)
    s = jnp.where(r == j, 1.0, 0.0) - l_cat
    (p,) = _headwise_products([l_cat], l_cat)
    k = 2
    while 2 * k < c:
        p_next, t = _headwise_products([p, s], p)
        s = s + t
        p = p_next
        k *= 2
    (t,) = _headwise_products([s], p)
    return s + t


def _gdn_prep_kernel(q_ref, k_ref, v_ref, hq_ref, hk_ref, hv_ref, small_ref, cw_ref, alog_ref,
                     dtb_ref, eg128_ref, eb128_ref, eg64_ref,
                     u_ref, w_ref, qg_ref, kd_ref, qk_ref, gl_ref, win_ref, *, blocks_per_seq):
    c = GDN_CHUNK
    rows = q_ref.shape[0]
    n_conv = cw_ref.shape[1]
    seq_start = lax.rem(pl.program_id(0), blocks_per_seq) == 0

    conv = []
    for s, (ref, halo_ref) in enumerate(((q_ref, hq_ref), (k_ref, hk_ref), (v_ref, hv_ref))):
        win_ref[s, 0:CONV_HALO, :] = jnp.where(seq_start, 0.0, halo_ref[...])
        win_ref[s, CONV_HALO:CONV_HALO + rows, :] = ref[...]
        acc = None
        for j in range(n_conv):
            shift = n_conv - 1 - j
            term = win_ref[s, pl.ds(CONV_HALO - shift, rows), :] * cw_ref[s, j:j + 1, :]
            acc = term if acc is None else acc + term
        conv.append(_silu(acc))
    cq, ck, cv = conv

    def l2n(x):
        parts = []
        for h in range(GDN_HEADS):
            xh = x[:, h * GDN_HEAD_DIM:(h + 1) * GDN_HEAD_DIM]
            ss = jnp.sum(xh * xh, axis=-1, keepdims=True)
            parts.append(xh * lax.rsqrt(ss + EPS))
        return jnp.concatenate(parts, axis=1)

    qn = l2n(cq)
    kn = l2n(ck)

    small = small_ref[...]
    g_tok = -jnp.exp(alog_ref[...]) * _softplus(small + dtb_ref[...])
    beta_tok = _sigmoid(small)
    row = lax.broadcasted_iota(jnp.int32, (rows, rows), 0)
    col = lax.broadcasted_iota(jnp.int32, (rows, rows), 1)
    same_chunk = _div_pow2(row, c) == _div_pow2(col, c)
    tril = jnp.where(row >= col, jnp.where(same_chunk, 1.0, 0.0), 0.0).astype(BF16)
    gc_tok = _dot_exact_lhs(tril, g_tok)
    gc128 = _dot_exact_rhs(gc_tok, eg128_ref[...])
    gc64 = _dot_exact_rhs(gc_tok, eg64_ref[...])
    beta128 = _dot_exact_rhs(beta_tok, eb128_ref[...])

    exp_gc = jnp.exp(gc128)
    kb = kn * beta128
    vb = cv * beta128
    kbg = kb * exp_gc
    qs = qn * (GDN_HEAD_DIM ** -0.5)
    qg_ref[...] = (qs * exp_gc).astype(BF16)
    kb16 = kb.astype(BF16)
    qs16 = qs.astype(BF16)
    kn16 = kn.astype(BF16)

    r_cat = lax.broadcasted_iota(jnp.int32, (c, GDN_CAT), 0)
    j_cat = _mod_pow2(lax.broadcasted_iota(jnp.int32, (c, GDN_CAT), 1), c)
    tri_cat = r_cat >= j_cat
    strict_cat = r_cat > j_cat
    lane_p = lax.broadcasted_iota(jnp.int32, (1, GDN_PAIR), 1)
    first = lane_p < GDN_HEAD_DIM
    group_w = GDN_GROUP * GDN_HEAD_DIM
    group_c = GDN_GROUP * c
    contract_last = (((1,), (1,)), ((), ()))

    for ch in range(rows // c):
        rs = slice(ch * c, (ch + 1) * c)
        gc64_c = gc64[rs]
        gc_row = jnp.sum(jnp.where(r_cat == j_cat, gc64_c, 0.0), axis=0, keepdims=True)
        decay = jnp.where(tri_cat, jnp.exp(jnp.where(tri_cat, gc64_c - gc_row, 0.0)), 0.0)
        g_last = gc128[ch * c + c - 1:ch * c + c, :]
        gl_ref[ch] = jnp.exp(g_last)
        kd_ref[rs, :] = (kn[rs] * jnp.exp(g_last - gc128[rs])).astype(BF16)

        inv_parts = []
        for gidx in range(GDN_HEADS // GDN_GROUP):
            ksl = slice(gidx * group_w, (gidx + 1) * group_w)
            csl = slice(gidx * group_c, (gidx + 1) * group_c)
            bd_k = _block_diag_rows(kn16[rs, ksl], GDN_GROUP, c, GDN_HEAD_DIM)
            both = lax.dot_general(jnp.concatenate([kb16[rs, ksl], qs16[rs, ksl]], axis=0), bd_k,
                                   contract_last, preferred_element_type=F32)
            dec = decay[:, csl]
            l_cat = jnp.where(strict_cat[:, csl], both[:c] * dec, 0.0)
            qk_ref[rs, csl] = (both[c:] * dec).astype(BF16)
            inv_parts.append(_unit_lower_inverse(l_cat))
        inv_cat = jnp.concatenate(inv_parts, axis=1)

        for p in range(GDN_PAIRS):
            wsl = slice(p * GDN_PAIR, (p + 1) * GDN_PAIR)
            inv_p = inv_cat[:, p * 2 * c:(p + 1) * 2 * c]
            vb_p, kbg_p = vb[rs, wsl], kbg[rs, wsl]
            rhs = jnp.concatenate(
                [jnp.concatenate([jnp.where(first, vb_p, 0.0), jnp.where(first, kbg_p, 0.0)], axis=1),
                 jnp.concatenate([jnp.where(first, 0.0, vb_p), jnp.where(first, 0.0, kbg_p)], axis=1)],
                axis=0)
            i_hi, i_lo = _split2(inv_p)
            r_hi, r_lo = _split2(rhs)
            top = _dot(jnp.concatenate([i_hi, i_lo], axis=0), r_hi)
            sol = top[:c] + top[c:] + _dot(i_hi, r_lo)
            u_ref[rs, wsl] = sol[:, :GDN_PAIR]
            w_ref[rs, wsl] = sol[:, GDN_PAIR:].astype(BF16)


def _gdn_scan_kernel(u_ref, w_ref, qg_ref, kd_ref, qk_ref, gl_ref, z_ref, nw_ref, o_ref, state_ref):
    c = GDN_CHUNK

    @pl.when(pl.program_id(0) == 0)
    def _():
        state_ref[...] = jnp.zeros_like(state_ref)

    lane_p = lax.broadcasted_iota(jnp.int32, (1, GDN_PAIR), 1)
    first = lane_p < GDN_HEAD_DIM
    bm_r = _div_pow2(lax.broadcasted_iota(jnp.int32, (GDN_PAIR, GDN_PAIR), 0), GDN_HEAD_DIM)
    bm_c = _div_pow2(lax.broadcasted_iota(jnp.int32, (GDN_PAIR, GDN_PAIR), 1), GDN_HEAD_DIM)
    same_head = bm_r == bm_c
    nw = nw_ref[...]
    zero = jnp.zeros((), BF16)
    for b in range(u_ref.shape[0]):
        for p in range(GDN_PAIRS):
            wsl = slice(p * GDN_PAIR, (p + 1) * GDN_PAIR)
            state = state_ref[b, p]
            s16 = state.astype(BF16)
            both = _dot(jnp.concatenate([w_ref[b, :, wsl], qg_ref[b, :, wsl]], axis=0), s16)
            v_new = u_ref[b, :, wsl] - both[:c]
            v16 = v_new.astype(BF16)
            v_bd = jnp.concatenate([jnp.where(first, v16, zero), jnp.where(first, zero, v16)], axis=0)
            o = both[c:] + _dot(qk_ref[b, :, p * 2 * c:(p + 1) * 2 * c], v_bd)
            upd = lax.dot_general(kd_ref[b, :, wsl], v16, (((0,), (0,)), ((), ())),
                                  preferred_element_type=F32)
            state_ref[b, p] = state * gl_ref[b, 0][:, wsl] + jnp.where(same_head, upd, 0.0)

            outs = []
            for h in range(2):
                oh = o[:, h * GDN_HEAD_DIM:(h + 1) * GDN_HEAD_DIM]
                ms = jnp.mean(oh * oh, axis=-1, keepdims=True)
                outs.append(oh * lax.rsqrt(ms + EPS) * nw)
            o_ref[b, :, wsl] = (jnp.concatenate(outs, axis=1)
                                * _silu(z_ref[b, :, wsl])).astype(o_ref.dtype)


def _gdn(gdn, small, conv_w3, alog_row, dtb_row, nw_row, batch, seq):
    t = gdn.shape[0]
    c = GDN_CHUNK
    w = GDN_WIDTH
    rows = GDN_PREP_CHUNKS * c
    e_g128, e_b128, e_g64 = _gdn_expanders()
    n_conv = conv_w3.shape[1]
    tok = lambda col: (lambda i: (i, col))
    halo = lambda col: (lambda i: (jnp.maximum(i * (rows // CONV_HALO) - 1, 0), col))
    const2 = lambda i: (0, 0)
    u, wk, qg, kd, qk, gl = pl.pallas_call(
        functools.partial(_gdn_prep_kernel, blocks_per_seq=seq // rows),
        grid=(t // rows,),
        in_specs=[pl.BlockSpec((rows, w), tok(0)),
                  pl.BlockSpec((rows, w), tok(1)),
                  pl.BlockSpec((rows, w), tok(2)),
                  pl.BlockSpec((CONV_HALO, w), halo(0)),
                  pl.BlockSpec((CONV_HALO, w), halo(1)),
                  pl.BlockSpec((CONV_HALO, w), halo(2)),
                  pl.BlockSpec((rows, LANES), tok(0)),
                  pl.BlockSpec((3, n_conv, w), lambda i: (0, 0, 0)),
                  pl.BlockSpec((1, LANES), const2),
                  pl.BlockSpec((1, LANES), const2),
                  pl.BlockSpec((LANES, w), const2),
                  pl.BlockSpec((LANES, w), const2),
                  pl.BlockSpec((LANES, GDN_CAT), const2)],
        out_specs=[pl.BlockSpec((rows, w), tok(0)),
                   pl.BlockSpec((rows, w), tok(0)),
                   pl.BlockSpec((rows, w), tok(0)),
                   pl.BlockSpec((rows, w), tok(0)),
                   pl.BlockSpec((rows, GDN_CAT), tok(0)),
                   pl.BlockSpec((GDN_PREP_CHUNKS, 1, w), lambda i: (i, 0, 0))],
        out_shape=[jax.ShapeDtypeStruct((t, w), F32),
                   jax.ShapeDtypeStruct((t, w), BF16),
                   jax.ShapeDtypeStruct((t, w), BF16),
                   jax.ShapeDtypeStruct((t, w), BF16),
                   jax.ShapeDtypeStruct((t, GDN_CAT), BF16),
                   jax.ShapeDtypeStruct((t // c, 1, w), F32)],
        scratch_shapes=[pltpu.VMEM((3, rows + CONV_HALO, w), F32)],
        compiler_params=_cparams("parallel"),
        name="gdn_prep",
    )(gdn, gdn, gdn, gdn, gdn, gdn, small, conv_w3, alog_row, dtb_row,
      jnp.asarray(e_g128, BF16), jnp.asarray(e_b128, BF16), jnp.asarray(e_g64, BF16))

    seq3 = lambda a: a.reshape(batch, seq, a.shape[-1])
    blk = lambda width, col=0: pl.BlockSpec((batch, c, width), lambda n: (0, n, col))
    o = pl.pallas_call(
        _gdn_scan_kernel,
        grid=(seq // c,),
        in_specs=[blk(w), blk(w), blk(w), blk(w), blk(GDN_CAT),
                  pl.BlockSpec((batch, 1, 1, w), lambda n: (0, n, 0, 0)),
                  blk(w, 3),
                  pl.BlockSpec((1, GDN_HEAD_DIM), lambda n: (0, 0))],
        out_specs=blk(w),
        out_shape=jax.ShapeDtypeStruct((batch, seq, w), BF16),
        scratch_shapes=[pltpu.VMEM((batch, GDN_PAIRS, GDN_PAIR, GDN_PAIR), F32)],
        compiler_params=_cparams("arbitrary"),
        name="gdn_scan",
    )(seq3(u), seq3(wk), seq3(qg), seq3(kd), seq3(qk), gl.reshape(batch, seq // c, 1, w),
      seq3(gdn), nw_row)
    return o.reshape(t, w)


def _merge_out_kernel(x_ref, oa_ref, ob_ref, ga_ref, gb_ref, wa_ref, wb_ref, wo_ref, o_ref):
    ya = _dot(oa_ref[...], wa_ref[...])
    yb = _dot(ob_ref[...], wb_ref[...])
    y = (_sigmoid(ga_ref[...].astype(F32)) * ya + _sigmoid(gb_ref[...].astype(F32)) * yb)
    o_ref[...] = x_ref[...] + _dot(y.astype(BF16), wo_ref[...])


def _merge_out(x, o_a, o_b, mg, w_a, w_b, w_o, tm):
    t, d = x.shape
    tokd = pl.BlockSpec((tm, d), lambda i: (i, 0))
    wspec = pl.BlockSpec((d, d), lambda i: (0, 0))
    return pl.pallas_call(
        _merge_out_kernel,
        grid=(t // tm,),
        in_specs=[tokd, tokd, tokd,
                  pl.BlockSpec((tm, d), lambda i: (i, 0)),
                  pl.BlockSpec((tm, d), lambda i: (i, 1)),
                  wspec, wspec, wspec],
        out_specs=tokd,
        out_shape=jax.ShapeDtypeStruct((t, d), F32),
        compiler_params=_cparams("parallel"),
        name="merge_out",
    )(x, o_a, o_b, mg, mg, w_a, w_b, w_o)


def _ffn_kernel(x_ref, nw_ref, wg_ref, wu_ref, wd_ref, fw_ref, o_ref, *, final_norm):
    x = x_ref[...]
    h = _rms_norm(x, nw_ref[...]).astype(BF16)
    a = (_silu(_dot(h, wg_ref[...])) * _dot(h, wu_ref[...])).astype(BF16)
    y = x + _dot(a, wd_ref[...])
    if final_norm:
        y = _rms_norm(y, fw_ref[...])
    o_ref[...] = y


def _ffn(x, nw, w_g, w_u, w_d, fw, final_norm, tm):
    t, d = x.shape
    f = w_g.shape[1]
    tokd = pl.BlockSpec((tm, d), lambda i: (i, 0))
    rowd = pl.BlockSpec((1, d), lambda i: (0, 0))
    return pl.pallas_call(
        functools.partial(_ffn_kernel, final_norm=final_norm),
        grid=(t // tm,),
        in_specs=[tokd, rowd,
                  pl.BlockSpec((d, f), lambda i: (0, 0)),
                  pl.BlockSpec((d, f), lambda i: (0, 0)),
                  pl.BlockSpec((f, d), lambda i: (0, 0)),
                  rowd],
        out_specs=tokd,
        out_shape=jax.ShapeDtypeStruct((t, d), F32),
        compiler_params=_cparams("parallel"),
        name="ffn",
    )(x, nw, w_g, w_u, w_d, fw)


def _pad_row(v, offset):
    return jnp.zeros((1, LANES), F32).at[0, offset:offset + v.shape[0]].set(v.astype(F32))


def _layer(x, batch, seq, norm_mix_w, w_in, conv_w, a_log, dt_bias, gdn_norm_w, fox_f_bias,
           w_branch_a, w_branch_b, w_out, norm_ffn_w, w_gate, w_up, w_down, final_w, final_norm):
    d = x.shape[1]
    gw, fw = GDN_WIDTH, FOX_HEADS * FOX_HEAD_DIM
    sizes = (gw, gw, gw, gw, GDN_HEADS, GDN_HEADS, fw, fw, fw, FOX_HEADS, fw, d, d)
    offs = np.concatenate([[0], np.cumsum(sizes)])
    col = lambda i: w_in[:, offs[i]:offs[i + 1]]
    w_gdn = jnp.concatenate([col(0), col(1), col(2), col(3)], axis=1).astype(BF16)
    w_fox = jnp.concatenate([col(6), col(7), col(8), col(10)], axis=1).astype(BF16)
    w_mg = jnp.concatenate([col(11), col(12)], axis=1).astype(BF16)
    n_small = 2 * GDN_HEADS + FOX_HEADS
    w_small = jnp.concatenate([col(4), col(5), col(9), jnp.zeros((d, LANES - n_small), F32)],
                              axis=1).astype(BF16)
    nw = norm_mix_w.reshape(1, d)

    gdn = _norm_matmul(x, nw, w_gdn, F32, tm=512)
    fox = _norm_matmul(x, nw, w_fox, BF16, tm=512)
    mg = _norm_matmul(x, nw, w_mg, BF16, tm=512)
    small = _norm_matmul(x, nw, w_small, F32, tm=512)

    eq, ek = _fox_bias(small, _pad_row(fox_f_bias, SMALL_FF), batch, seq, tm=min(seq, 512))
    o_b = _fox_attention(fox, eq, ek, batch, seq, tq=min(seq, 512))

    conv_w3 = conv_w.reshape(conv_w.shape[0], 3, gw).transpose(1, 0, 2)
    o_a = _gdn(gdn, small, conv_w3, _pad_row(a_log, SMALL_GA), _pad_row(dt_bias, SMALL_GA),
               gdn_norm_w.reshape(1, GDN_HEAD_DIM), batch, seq)

    x1 = _merge_out(x, o_a, o_b, mg, w_branch_a.astype(BF16), w_branch_b.astype(BF16),
                    w_out.astype(BF16), tm=512)
    return _ffn(x1, norm_ffn_w.reshape(1, d), w_gate.astype(BF16), w_up.astype(BF16),
                w_down.astype(BF16), final_w.reshape(1, d), final_norm, tm=256)


def kernel(x, norm_mix_w, w_in, conv_w, a_log, dt_bias, gdn_norm_w, fox_f_bias, w_branch_a,
           w_branch_b, w_out, norm_ffn_w, w_gate, w_up, w_down, norm_final_w):
    batch, seq, d = x.shape
    depth = w_in.shape[0]
    h = x.reshape(batch * seq, d)
    for l in range(depth):
        h = _layer(h, batch, seq, norm_mix_w[l], w_in[l], conv_w[l], a_log[l], dt_bias[l],
                   gdn_norm_w[l], fox_f_bias[l], w_branch_a[l], w_branch_b[l], w_out[l],
                   norm_ffn_w[l], w_gate[l], w_up[l], w_down[l], norm_final_w,
                   final_norm=(l == depth - 1))
    return h.reshape(batch, seq, d)
```

```python
import functools

import jax
import jax.numpy as jnp
import numpy as np
from jax import lax
from jax.experimental import pallas as pl
from jax.experimental.pallas import tpu as pltpu

F32 = jnp.float32
BF16 = jnp.bfloat16

EPS = 1e-6
GDN_HEADS = 8
GDN_HEAD_DIM = 128
GDN_CHUNK = 64
FOX_HEADS = 16
FOX_HEAD_DIM = 64
LANES = 128
VMEM_LIMIT_BYTES = 56 * 1024 * 1024
NEG_BIG = -1e30


def _cparams(*semantics):
    return pltpu.CompilerParams(dimension_semantics=semantics,
                                vmem_limit_bytes=VMEM_LIMIT_BYTES)


def _split2(x):
    hi = x.astype(BF16)
    lo = (x - hi.astype(F32)).astype(BF16)
    return hi, lo


def _split3(x):
    hi = x.astype(BF16)
    r = x - hi.astype(F32)
    mid = r.astype(BF16)
    lo = (r - mid.astype(F32)).astype(BF16)
    return hi, mid, lo


def _dot(a, b):
    return jnp.dot(a, b, preferred_element_type=F32)


def _dot_exact_rhs(x, m_bf16):
    n = x.shape[0]
    y = _dot(jnp.concatenate(_split3(x), axis=0), m_bf16)
    return y[:n] + y[n:2 * n] + y[2 * n:]


def _dot_exact_lhs(m_bf16, x):
    n = x.shape[1]
    y = _dot(m_bf16, jnp.concatenate(_split3(x), axis=1))
    return y[:, :n] + y[:, n:2 * n] + y[:, 2 * n:]


def _div_pow2(x, n):
    assert n & (n - 1) == 0
    return jnp.right_shift(x, n.bit_length() - 1)


def _mod_pow2(x, n):
    assert n & (n - 1) == 0
    return jnp.bitwise_and(x, n - 1)


def _softplus(y):
    return jnp.maximum(y, 0.0) + jnp.log(1.0 + jnp.exp(-jnp.abs(y)))


def _sigmoid(y):
    return 1.0 / (1.0 + jnp.exp(-y))


def _silu(y):
    return y * _sigmoid(y)


def _rms_norm(x, w):
    ms = jnp.mean(x * x, axis=-1, keepdims=True)
    return x * lax.rsqrt(ms + EPS) * w


def _norm_matmul_kernel(x_ref, nw_ref, w_ref, o_ref, *, n_chunk):
    h = _rms_norm(x_ref[...], nw_ref[...]).astype(BF16)
    n = o_ref.shape[1]
    for c in range(0, n, n_chunk):
        o_ref[:, c:c + n_chunk] = _dot(h, w_ref[:, c:c + n_chunk]).astype(o_ref.dtype)


def _norm_matmul(x, nw, w, out_dtype, tm):
    t, d = x.shape
    n = w.shape[1]
    return pl.pallas_call(
        functools.partial(_norm_matmul_kernel, n_chunk=min(n, 1024)),
        grid=(t // tm,),
        in_specs=[pl.BlockSpec((tm, d), lambda i: (i, 0)),
                  pl.BlockSpec((1, d), lambda i: (0, 0)),
                  pl.BlockSpec((d, n), lambda i: (0, 0))],
        out_specs=pl.BlockSpec((tm, n), lambda i: (i, 0)),
        out_shape=jax.ShapeDtypeStruct((t, n), out_dtype),
        compiler_params=_cparams("parallel"),
        name="norm_matmul",
    )(x, nw, w)


SMALL_GA = 0
SMALL_GB = GDN_HEADS
SMALL_FF = 2 * GDN_HEADS
BIAS_TERMS = 3
BIAS_HEAD_STRIDE = 8


def _bias_placements():
    n_out = (FOX_HEADS // 2) * LANES
    pq = np.zeros((BIAS_TERMS, LANES, n_out), np.float32)
    pk = np.zeros((BIAS_TERMS, LANES, n_out), np.float32)
    ones_q = np.zeros((1, n_out), np.float32)
    ones_k = np.zeros((1, n_out), np.float32)
    for h in range(FOX_HEADS):
        base = (h // 2) * LANES + (h % 2) * BIAS_HEAD_STRIDE
        for t in range(BIAS_TERMS):
            pq[t, SMALL_FF + h, base + t] = 1.0
            pk[t, SMALL_FF + h, base + BIAS_TERMS + t] = -1.0
            ones_q[0, base + BIAS_TERMS + t] = 1.0
            ones_k[0, base + t] = 1.0
    return pq, pk, ones_q, ones_k


def _fox_bias_kernel(small_ref, fb_ref, pq_ref, pk_ref, oq_ref, ok_ref, eq_ref, ek_ref,
                     carry_ref):
    tm = small_ref.shape[0]

    @pl.when(pl.program_id(1) == 0)
    def _():
        carry_ref[...] = jnp.zeros_like(carry_ref)

    z = small_ref[...] + fb_ref[...]
    log_f = -_softplus(-z)
    row = lax.broadcasted_iota(jnp.int32, (tm, tm), 0)
    col = lax.broadcasted_iota(jnp.int32, (tm, tm), 1)
    tril = jnp.where(row >= col, 1.0, 0.0).astype(BF16)
    cum = _dot_exact_lhs(tril, log_f) + carry_ref[0:1, :]
    carry_ref[...] = jnp.broadcast_to(cum[tm - 1:tm, :], carry_ref.shape)
    terms = _split3(cum)
    eq = oq_ref[...]
    ek = ok_ref[...]
    for t in range(BIAS_TERMS):
        eq = eq + _dot(terms[t], pq_ref[t])
        ek = ek + _dot(terms[t], pk_ref[t])
    eq_ref[...] = eq.astype(BF16)
    ek_ref[...] = ek.astype(BF16)


def _fox_bias(small, fb_row, batch, seq, tm):
    t = small.shape[0]
    pq, pk, ones_q, ones_k = _bias_placements()
    n_out = pq.shape[2]
    nt = seq // tm
    const2 = lambda b, i: (0, 0)
    const3 = lambda b, i: (0, 0, 0)
    return pl.pallas_call(
        _fox_bias_kernel,
        grid=(batch, nt),
        in_specs=[pl.BlockSpec((tm, LANES), lambda b, i: (b * nt + i, 0)),
                  pl.BlockSpec((1, LANES), const2),
                  pl.BlockSpec((BIAS_TERMS, LANES, n_out), const3),
                  pl.BlockSpec((BIAS_TERMS, LANES, n_out), const3),
                  pl.BlockSpec((1, n_out), const2),
                  pl.BlockSpec((1, n_out), const2)],
        out_specs=[pl.BlockSpec((tm, n_out), lambda b, i: (b * nt + i, 0)),
                   pl.BlockSpec((tm, n_out), lambda b, i: (b * nt + i, 0))],
        out_shape=[jax.ShapeDtypeStruct((t, n_out), BF16),
                   jax.ShapeDtypeStruct((t, n_out), BF16)],
        scratch_shapes=[pltpu.VMEM((8, LANES), F32)],
        compiler_params=_cparams("parallel", "arbitrary"),
        name="fox_bias",
    )(small, fb_row, jnp.asarray(pq, BF16), jnp.asarray(pk, BF16),
      jnp.asarray(ones_q), jnp.asarray(ones_k))


FOX_PAIRS_PER_STEP = 2


def _fox_attention_kernel(q_ref, eq_ref, k_ref, ek_ref, v_ref, fo_ref, o_ref, *, tq):
    i = pl.program_id(2)
    n_pairs = q_ref.shape[1] // LANES
    lane = lax.broadcasted_iota(jnp.int32, (1, LANES), 1)
    head_a = lane < FOX_HEAD_DIM
    bias_a = lane < BIAS_HEAD_STRIDE
    zero = jnp.zeros((), BF16)
    one = jnp.ones((), BF16)
    lhs = []
    for p in range(n_pairs):
        psl = slice(p * LANES, (p + 1) * LANES)
        q = q_ref[:, psl] * jnp.asarray(FOX_HEAD_DIM ** -0.5, BF16)
        eq = eq_ref[:, psl]
        lhs.append(jnp.concatenate([jnp.where(head_a, q, zero), jnp.where(bias_a, eq, zero)], axis=1))
        lhs.append(jnp.concatenate([jnp.where(head_a, zero, q), jnp.where(bias_a, zero, eq)], axis=1))

    def tile(j, carry, masked):
        start = pl.multiple_of(j * tq, tq)
        n_heads = 2 * n_pairs

        def score(idx):
            p, h = divmod(idx, 2)
            psl = slice(p * LANES, (p + 1) * LANES)
            kk = jnp.concatenate([k_ref[pl.ds(start, tq), psl], ek_ref[pl.ds(start, tq), psl]], axis=1)
            return lax.dot_general(lhs[idx], kk, (((1,), (1,)), ((), ())),
                                   preferred_element_type=F32)

        def value(idx):
            p, h = divmod(idx, 2)
            v = v_ref[pl.ds(start, tq), p * LANES:(p + 1) * LANES]
            return jnp.where(head_a, v, one) if h == 0 else jnp.where(head_a, one, v)

        out = []
        s_next = score(0)
        for idx in range(n_heads):
            s = s_next
            if idx + 1 < n_heads:
                s_next = score(idx + 1)
            m, acc = carry[idx]
            if masked:
                r = lax.broadcasted_iota(jnp.int32, (tq, tq), 0)
                c = lax.broadcasted_iota(jnp.int32, (tq, tq), 1)
                s = jnp.where(c <= r, s, NEG_BIG)
            m_new = jnp.maximum(m, jnp.max(s, axis=-1, keepdims=True))
            alpha = jnp.exp(m - m_new)
            prob = jnp.exp(s - m_new).astype(BF16)
            acc = alpha * acc + _dot(prob, value(idx))
            out.append((m_new, acc))
        return tuple(out)

    init = tuple((jnp.full((tq, 1), NEG_BIG, F32), jnp.zeros((tq, LANES), F32))
                 for _ in range(2 * n_pairs))
    carry = lax.fori_loop(0, i, lambda j, c: tile(j, c, False), init)
    final = tile(i, carry, True)
    for p in range(n_pairs):
        psl = slice(p * LANES, (p + 1) * LANES)
        acc_a, acc_b = final[2 * p][1], final[2 * p + 1][1]
        num = jnp.where(head_a, acc_a, acc_b)
        den = pltpu.roll(jnp.where(head_a, acc_b, acc_a), FOX_HEAD_DIM, axis=1)
        gate = _sigmoid(fo_ref[:, psl].astype(F32))
        o_ref[:, psl] = (num / den * gate).astype(o_ref.dtype)


def _fox_attention(fox, eq, ek, batch, seq, tq):
    t = fox.shape[0]
    pairs = FOX_HEADS // 2
    steps = pairs // FOX_PAIRS_PER_STEP
    width = FOX_PAIRS_PER_STEP * LANES
    nq = seq // tq
    return pl.pallas_call(
        functools.partial(_fox_attention_kernel, tq=tq),
        grid=(batch, steps, nq),
        in_specs=[pl.BlockSpec((tq, width), lambda b, p, i: (b * nq + i, p)),
                  pl.BlockSpec((tq, width), lambda b, p, i: (b * nq + i, p)),
                  pl.BlockSpec((seq, width), lambda b, p, i: (b, steps + p)),
                  pl.BlockSpec((seq, width), lambda b, p, i: (b, p)),
                  pl.BlockSpec((seq, width), lambda b, p, i: (b, 2 * steps + p)),
                  pl.BlockSpec((tq, width), lambda b, p, i: (b * nq + i, 3 * steps + p))],
        out_specs=pl.BlockSpec((tq, width), lambda b, p, i: (b * nq + i, p)),
        out_shape=jax.ShapeDtypeStruct((t, pairs * LANES), BF16),
        compiler_params=_cparams("parallel", "parallel", "arbitrary"),
        name="fox_attention",
    )(fox, eq, fox, ek, fox, fox)


GDN_WIDTH = GDN_HEADS * GDN_HEAD_DIM
GDN_GROUP = 4
GDN_CAT = GDN_HEADS * GDN_CHUNK
GDN_PAIRS = GDN_HEADS // 2
GDN_PAIR = 2 * GDN_HEAD_DIM
GDN_PREP_CHUNKS = 4
CONV_HALO = 8


def _gdn_expanders():
    e_g128 = np.zeros((LANES, GDN_WIDTH), np.float32)
    e_b128 = np.zeros((LANES, GDN_WIDTH), np.float32)
    e_g64 = np.zeros((LANES, GDN_CAT), np.float32)
    for h in range(GDN_HEADS):
        e_g128[SMALL_GA + h, h * GDN_HEAD_DIM:(h + 1) * GDN_HEAD_DIM] = 1.0
        e_b128[SMALL_GB + h, h * GDN_HEAD_DIM:(h + 1) * GDN_HEAD_DIM] = 1.0
        e_g64[SMALL_GA + h, h * GDN_CHUNK:(h + 1) * GDN_CHUNK] = 1.0
    return e_g128, e_b128, e_g64


def _block_diag_rows(x, n_blocks, rows_per_block, cols_per_block):
    tiled = jnp.concatenate([x] * n_blocks, axis=0)
    r = _div_pow2(lax.broadcasted_iota(jnp.int32, tiled.shape, 0), rows_per_block)
    c = _div_pow2(lax.broadcasted_iota(jnp.int32, tiled.shape, 1), cols_per_block)
    return jnp.where(r == c, tiled, jnp.zeros((), tiled.dtype))


def _rows(x, i, n):
    return x[i * n:(i + 1) * n]


def _headwise_products(lhs, b):
    c = b.shape[0]
    n = len(lhs)
    b_hi, b_lo = _split2(b)
    bd_hi = _block_diag_rows(b_hi, GDN_GROUP, c, c)
    bd_lo = _block_diag_rows(b_lo, GDN_GROUP, c, c)
    parts = [_split2(x) for x in lhs]
    his = [p[0] for p in parts]
    los = [p[1] for p in parts]
    top = _dot(jnp.concatenate(his + los, axis=0), bd_hi)
    bot = _dot(jnp.concatenate(his, axis=0), bd_lo)
    return [_rows(top, i, c) + _rows(top, n + i, c) + _rows(bot, i, c) for i in range(n)]


def _unit_lower_inverses(l_cats):
    c, n = l_cats[0].shape
    r = lax.broadcasted_iota(jnp.int32, (c, n), 0)
    j = _mod_pow2(lax.broadcasted_iota(jnp.int32, (c, n), 1), c)
    eye = jnp.where(r == j, 1.0, 0.0)
    ss = [eye - l for l in l_cats]
    ps = [_headwise_products([l], l)[0] for l in l_cats]
    k = 2
    while 2 * k < c:
        stage = [_headwise_products([p, s], p) for p, s in zip(ps, ss)]
        ps = [st[0] for st in stage]
        ss = [s + st[1] for s, st in zip(ss, stage)]
        k *= 2
    return [s + _headwise_products([s], p)[0] for p, s in zip(ps, ss)]


def _gdn_prep_kernel(q_ref, k_ref, v_ref, hq_ref, hk_ref, hv_ref, small_ref, cw_ref, alog_ref,
                     dtb_ref, eg128_ref, eb128_ref, eg64_ref,
                     u_ref, w_ref, qg_ref, kd_ref, qk_ref, gl_ref, win_ref, *, blocks_per_seq):
    c = GDN_CHUNK
    rows = q_ref.shape[0]
    n_conv = cw_ref.shape[1]
    seq_start = lax.rem(pl.program_id(0), blocks_per_seq) == 0

    conv = []
    for s, (ref, halo_ref) in enumerate(((q_ref, hq_ref), (k_ref, hk_ref), (v_ref, hv_ref))):
        win_ref[s, 0:CONV_HALO, :] = jnp.where(seq_start, 0.0, halo_ref[...])
        win_ref[s, CONV_HALO:CONV_HALO + rows, :] = ref[...]
        acc = None
        for j in range(n_conv):
            shift = n_conv - 1 - j
            term = win_ref[s, pl.ds(CONV_HALO - shift, rows), :] * cw_ref[s, j:j + 1, :]
            acc = term if acc is None else acc + term
        conv.append(_silu(acc))
    cq, ck, cv = conv

    def l2n(x):
        parts = []
        for h in range(GDN_HEADS):
            xh = x[:, h * GDN_HEAD_DIM:(h + 1) * GDN_HEAD_DIM]
            ss = jnp.sum(xh * xh, axis=-1, keepdims=True)
            parts.append(xh * lax.rsqrt(ss + EPS))
        return jnp.concatenate(parts, axis=1)

    qn = l2n(cq)
    kn = l2n(ck)

    small = small_ref[...]
    g_tok = -jnp.exp(alog_ref[...]) * _softplus(small + dtb_ref[...])
    beta_tok = _sigmoid(small)
    row = lax.broadcasted_iota(jnp.int32, (rows, rows), 0)
    col = lax.broadcasted_iota(jnp.int32, (rows, rows), 1)
    same_chunk = _div_pow2(row, c) == _div_pow2(col, c)
    tril = jnp.where(row >= col, jnp.where(same_chunk, 1.0, 0.0), 0.0).astype(BF16)
    gc_tok = _dot_exact_lhs(tril, g_tok)
    gc128 = _dot_exact_rhs(gc_tok, eg128_ref[...])
    gc64 = _dot_exact_rhs(gc_tok, eg64_ref[...])
    beta128 = _dot_exact_rhs(beta_tok, eb128_ref[...])

    exp_gc = jnp.exp(gc128)
    kb = kn * beta128
    vb = cv * beta128
    kbg = kb * exp_gc
    qs = qn * (GDN_HEAD_DIM ** -0.5)
    qg_ref[...] = (qs * exp_gc).astype(BF16)
    kb16 = kb.astype(BF16)
    qs16 = qs.astype(BF16)
    kn16 = kn.astype(BF16)

    r_cat = lax.broadcasted_iota(jnp.int32, (c, GDN_CAT), 0)
    j_cat = _mod_pow2(lax.broadcasted_iota(jnp.int32, (c, GDN_CAT), 1), c)
    tri_cat = r_cat >= j_cat
    strict_cat = r_cat > j_cat
    lane_p = lax.broadcasted_iota(jnp.int32, (1, GDN_PAIR), 1)
    first = lane_p < GDN_HEAD_DIM
    group_w = GDN_GROUP * GDN_HEAD_DIM
    group_c = GDN_GROUP * c
    contract_last = (((1,), (1,)), ((), ()))

    n_chunks = rows // c
    n_groups = GDN_HEADS // GDN_GROUP
    l_cats = []
    for ch in range(n_chunks):
        rs = slice(ch * c, (ch + 1) * c)
        gc64_c = gc64[rs]
        gc_row = jnp.sum(jnp.where(r_cat == j_cat, gc64_c, 0.0), axis=0, keepdims=True)
        decay = jnp.where(tri_cat, jnp.exp(jnp.where(tri_cat, gc64_c - gc_row, 0.0)), 0.0)
        g_last = gc128[ch * c + c - 1:ch * c + c, :]
        gl_ref[ch] = jnp.exp(g_last)
        kd_ref[rs, :] = (kn[rs] * jnp.exp(g_last - gc128[rs])).astype(BF16)

        for gidx in range(n_groups):
            ksl = slice(gidx * group_w, (gidx + 1) * group_w)
            csl = slice(gidx * group_c, (gidx + 1) * group_c)
            bd_k = _block_diag_rows(kn16[rs, ksl], GDN_GROUP, c, GDN_HEAD_DIM)
            both = lax.dot_general(jnp.concatenate([kb16[rs, ksl], qs16[rs, ksl]], axis=0), bd_k,
                                   contract_last, preferred_element_type=F32)
            dec = decay[:, csl]
            l_cats.append(jnp.where(strict_cat[:, csl], both[:c] * dec, 0.0))
            qk_ref[rs, csl] = (both[c:] * dec).astype(BF16)

    invs = _unit_lower_inverses(l_cats)

    for ch in range(n_chunks):
        rs = slice(ch * c, (ch + 1) * c)
        inv_cat = jnp.concatenate(invs[ch * n_groups:(ch + 1) * n_groups], axis=1)
        for p in range(GDN_PAIRS):
            wsl = slice(p * GDN_PAIR, (p + 1) * GDN_PAIR)
            inv_p = inv_cat[:, p * 2 * c:(p + 1) * 2 * c]
            vb_p, kbg_p = vb[rs, wsl], kbg[rs, wsl]
            rhs = jnp.concatenate(
                [jnp.concatenate([jnp.where(first, vb_p, 0.0), jnp.where(first, kbg_p, 0.0)], axis=1),
                 jnp.concatenate([jnp.where(first, 0.0, vb_p), jnp.where(first, 0.0, kbg_p)], axis=1)],
                axis=0)
            i_hi, i_lo = _split2(inv_p)
            r_hi, r_lo = _split2(rhs)
            top = _dot(jnp.concatenate([i_hi, i_lo], axis=0), r_hi)
            sol = top[:c] + top[c:] + _dot(i_hi, r_lo)
            u_ref[rs, wsl] = sol[:, :GDN_PAIR]
            w_ref[rs, wsl] = sol[:, GDN_PAIR:].astype(BF16)


def _gdn_scan_kernel(u_ref, w_ref, qg_ref, kd_ref, qk_ref, gl_ref, z_ref, nw_ref, o_ref, state_ref):
    c = GDN_CHUNK

    @pl.when(pl.program_id(0) == 0)
    def _():
        state_ref[...] = jnp.zeros_like(state_ref)

    lane_p = lax.broadcasted_iota(jnp.int32, (1, GDN_PAIR), 1)
    first = lane_p < GDN_HEAD_DIM
    bm_r = _div_pow2(lax.broadcasted_iota(jnp.int32, (GDN_PAIR, GDN_PAIR), 0), GDN_HEAD_DIM)
    bm_c = _div_pow2(lax.broadcasted_iota(jnp.int32, (GDN_PAIR, GDN_PAIR), 1), GDN_HEAD_DIM)
    same_head = bm_r == bm_c
    nw = nw_ref[...]
    zero = jnp.zeros((), BF16)
    cells = [(b, p) for b in range(u_ref.shape[0]) for p in range(GDN_PAIRS)]
    wsl = lambda p: slice(p * GDN_PAIR, (p + 1) * GDN_PAIR)
    states = [state_ref[b, p] for b, p in cells]
    boths = [_dot(jnp.concatenate([w_ref[b, :, wsl(p)], qg_ref[b, :, wsl(p)]], axis=0),
                  st.astype(BF16)) for (b, p), st in zip(cells, states)]
    v16s = [(u_ref[b, :, wsl(p)] - both[:c]).astype(BF16) for (b, p), both in zip(cells, boths)]
    outs, upds = [], []
    for (b, p), both, v16 in zip(cells, boths, v16s):
        v_bd = jnp.concatenate([jnp.where(first, v16, zero), jnp.where(first, zero, v16)], axis=0)
        outs.append(both[c:] + _dot(qk_ref[b, :, p * 2 * c:(p + 1) * 2 * c], v_bd))
        upds.append(lax.dot_general(kd_ref[b, :, wsl(p)], v16, (((0,), (0,)), ((), ())),
                                    preferred_element_type=F32))
    for (b, p), st, o, upd in zip(cells, states, outs, upds):
        state_ref[b, p] = st * gl_ref[b, 0][:, wsl(p)] + jnp.where(same_head, upd, 0.0)
        halves = []
        for h in range(2):
            oh = o[:, h * GDN_HEAD_DIM:(h + 1) * GDN_HEAD_DIM]
            ms = jnp.mean(oh * oh, axis=-1, keepdims=True)
            halves.append(oh * lax.rsqrt(ms + EPS) * nw)
        o_ref[b, :, wsl(p)] = (jnp.concatenate(halves, axis=1)
                               * _silu(z_ref[b, :, wsl(p)])).astype(o_ref.dtype)


def _gdn(gdn, small, conv_w3, alog_row, dtb_row, nw_row, batch, seq):
    t = gdn.shape[0]
    c = GDN_CHUNK
    w = GDN_WIDTH
    rows = GDN_PREP_CHUNKS * c
    e_g128, e_b128, e_g64 = _gdn_expanders()
    n_conv = conv_w3.shape[1]
    tok = lambda col: (lambda i: (i, col))
    halo = lambda col: (lambda i: (jnp.maximum(i * (rows // CONV_HALO) - 1, 0), col))
    const2 = lambda i: (0, 0)
    u, wk, qg, kd, qk, gl = pl.pallas_call(
        functools.partial(_gdn_prep_kernel, blocks_per_seq=seq // rows),
        grid=(t // rows,),
        in_specs=[pl.BlockSpec((rows, w), tok(0)),
                  pl.BlockSpec((rows, w), tok(1)),
                  pl.BlockSpec((rows, w), tok(2)),
                  pl.BlockSpec((CONV_HALO, w), halo(0)),
                  pl.BlockSpec((CONV_HALO, w), halo(1)),
                  pl.BlockSpec((CONV_HALO, w), halo(2)),
                  pl.BlockSpec((rows, LANES), tok(0)),
                  pl.BlockSpec((3, n_conv, w), lambda i: (0, 0, 0)),
                  pl.BlockSpec((1, LANES), const2),
                  pl.BlockSpec((1, LANES), const2),
                  pl.BlockSpec((LANES, w), const2),
                  pl.BlockSpec((LANES, w), const2),
                  pl.BlockSpec((LANES, GDN_CAT), const2)],
        out_specs=[pl.BlockSpec((rows, w), tok(0)),
                   pl.BlockSpec((rows, w), tok(0)),
                   pl.BlockSpec((rows, w), tok(0)),
                   pl.BlockSpec((rows, w), tok(0)),
                   pl.BlockSpec((rows, GDN_CAT), tok(0)),
                   pl.BlockSpec((GDN_PREP_CHUNKS, 1, w), lambda i: (i, 0, 0))],
        out_shape=[jax.ShapeDtypeStruct((t, w), F32),
                   jax.ShapeDtypeStruct((t, w), BF16),
                   jax.ShapeDtypeStruct((t, w), BF16),
                   jax.ShapeDtypeStruct((t, w), BF16),
                   jax.ShapeDtypeStruct((t, GDN_CAT), BF16),
                   jax.ShapeDtypeStruct((t // c, 1, w), F32)],
        scratch_shapes=[pltpu.VMEM((3, rows + CONV_HALO, w), F32)],
        compiler_params=_cparams("parallel"),
        name="gdn_prep",
    )(gdn, gdn, gdn, gdn, gdn, gdn, small, conv_w3, alog_row, dtb_row,
      jnp.asarray(e_g128, BF16), jnp.asarray(e_b128, BF16), jnp.asarray(e_g64, BF16))

    seq3 = lambda a: a.reshape(batch, seq, a.shape[-1])
    blk = lambda width, col=0: pl.BlockSpec((batch, c, width), lambda n: (0, n, col))
    o = pl.pallas_call(
        _gdn_scan_kernel,
        grid=(seq // c,),
        in_specs=[blk(w), blk(w), blk(w), blk(w), blk(GDN_CAT),
                  pl.BlockSpec((batch, 1, 1, w), lambda n: (0, n, 0, 0)),
                  blk(w, 3),
                  pl.BlockSpec((1, GDN_HEAD_DIM), lambda n: (0, 0))],
        out_specs=blk(w),
        out_shape=jax.ShapeDtypeStruct((batch, seq, w), BF16),
        scratch_shapes=[pltpu.VMEM((batch, GDN_PAIRS, GDN_PAIR, GDN_PAIR), F32)],
        compiler_params=_cparams("arbitrary"),
        name="gdn_scan",
    )(seq3(u), seq3(wk), seq3(qg), seq3(kd), seq3(qk), gl.reshape(batch, seq // c, 1, w),
      seq3(gdn), nw_row)
    return o.reshape(t, w)


def _merge_out_kernel(x_ref, oa_ref, ob_ref, ga_ref, gb_ref, wa_ref, wb_ref, wo_ref, o_ref):
    ya = _dot(oa_ref[...], wa_ref[...])
    yb = _dot(ob_ref[...], wb_ref[...])
    y = (_sigmoid(ga_ref[...].astype(F32)) * ya + _sigmoid(gb_ref[...].astype(F32)) * yb)
    o_ref[...] = x_ref[...] + _dot(y.astype(BF16), wo_ref[...])


def _merge_out(x, o_a, o_b, mg, w_a, w_b, w_o, tm):
    t, d = x.shape
    tokd = pl.BlockSpec((tm, d), lambda i: (i, 0))
    wspec = pl.BlockSpec((d, d), lambda i: (0, 0))
    return pl.pallas_call(
        _merge_out_kernel,
        grid=(t // tm,),
        in_specs=[tokd, tokd, tokd,
                  pl.BlockSpec((tm, d), lambda i: (i, 0)),
                  pl.BlockSpec((tm, d), lambda i: (i, 1)),
                  wspec, wspec, wspec],
        out_specs=tokd,
        out_shape=jax.ShapeDtypeStruct((t, d), F32),
        compiler_params=_cparams("parallel"),
        name="merge_out",
    )(x, o_a, o_b, mg, mg, w_a, w_b, w_o)


def _ffn_kernel(x_ref, nw_ref, wg_ref, wu_ref, wd_ref, fw_ref, o_ref, *, final_norm):
    x = x_ref[...]
    h = _rms_norm(x, nw_ref[...]).astype(BF16)
    a = (_silu(_dot(h, wg_ref[...])) * _dot(h, wu_ref[...])).astype(BF16)
    y = x + _dot(a, wd_ref[...])
    if final_norm:
        y = _rms_norm(y, fw_ref[...])
    o_ref[...] = y


def _ffn(x, nw, w_g, w_u, w_d, fw, final_norm, tm):
    t, d = x.shape
    f = w_g.shape[1]
    tokd = pl.BlockSpec((tm, d), lambda i: (i, 0))
    rowd = pl.BlockSpec((1, d), lambda i: (0, 0))
    return pl.pallas_call(
        functools.partial(_ffn_kernel, final_norm=final_norm),
        grid=(t // tm,),
        in_specs=[tokd, rowd,
                  pl.BlockSpec((d, f), lambda i: (0, 0)),
                  pl.BlockSpec((d, f), lambda i: (0, 0)),
                  pl.BlockSpec((f, d), lambda i: (0, 0)),
                  rowd],
        out_specs=tokd,
        out_shape=jax.ShapeDtypeStruct((t, d), F32),
        compiler_params=_cparams("parallel"),
        name="ffn",
    )(x, nw, w_g, w_u, w_d, fw)


def _pad_row(v, offset):
    return jnp.zeros((1, LANES), F32).at[0, offset:offset + v.shape[0]].set(v.astype(F32))


def _layer(x, batch, seq, norm_mix_w, w_in, conv_w, a_log, dt_bias, gdn_norm_w, fox_f_bias,
           w_branch_a, w_branch_b, w_out, norm_ffn_w, w_gate, w_up, w_down, final_w, final_norm):
    d = x.shape[1]
    gw, fw = GDN_WIDTH, FOX_HEADS * FOX_HEAD_DIM
    sizes = (gw, gw, gw, gw, GDN_HEADS, GDN_HEADS, fw, fw, fw, FOX_HEADS, fw, d, d)
    offs = np.concatenate([[0], np.cumsum(sizes)])
    col = lambda i: w_in[:, offs[i]:offs[i + 1]]
    w_gdn = jnp.concatenate([col(0), col(1), col(2), col(3)], axis=1).astype(BF16)
    w_fox = jnp.concatenate([col(6), col(7), col(8), col(10)], axis=1).astype(BF16)
    w_mg = jnp.concatenate([col(11), col(12)], axis=1).astype(BF16)
    n_small = 2 * GDN_HEADS + FOX_HEADS
    w_small = jnp.concatenate([col(4), col(5), col(9), jnp.zeros((d, LANES - n_small), F32)],
                              axis=1).astype(BF16)
    nw = norm_mix_w.reshape(1, d)

    gdn = _norm_matmul(x, nw, w_gdn, F32, tm=512)
    fox = _norm_matmul(x, nw, w_fox, BF16, tm=512)
    mg = _norm_matmul(x, nw, w_mg, BF16, tm=512)
    small = _norm_matmul(x, nw, w_small, F32, tm=512)

    eq, ek = _fox_bias(small, _pad_row(fox_f_bias, SMALL_FF), batch, seq, tm=min(seq, 512))
    o_b = _fox_attention(fox, eq, ek, batch, seq, tq=min(seq, 512))

    conv_w3 = conv_w.reshape(conv_w.shape[0], 3, gw).transpose(1, 0, 2)
    o_a = _gdn(gdn, small, conv_w3, _pad_row(a_log, SMALL_GA), _pad_row(dt_bias, SMALL_GA),
               gdn_norm_w.reshape(1, GDN_HEAD_DIM), batch, seq)

    x1 = _merge_out(x, o_a, o_b, mg, w_branch_a.astype(BF16), w_branch_b.astype(BF16),
                    w_out.astype(BF16), tm=512)
    return _ffn(x1, norm_ffn_w.reshape(1, d), w_gate.astype(BF16), w_up.astype(BF16),
                w_down.astype(BF16), final_w.reshape(1, d), final_norm, tm=256)


def kernel(x, norm_mix_w, w_in, conv_w, a_log, dt_bias, gdn_norm_w, fox_f_bias, w_branch_a,
           w_branch_b, w_out, norm_ffn_w, w_gate, w_up, w_down, norm_final_w):
    batch, seq, d = x.shape
    depth = w_in.shape[0]
    h = x.reshape(batch * seq, d)
    for l in range(depth):
        h = _layer(h, batch, seq, norm_mix_w[l], w_in[l], conv_w[l], a_log[l], dt_bias[l],
                   gdn_norm_w[l], fox_f_bias[l], w_branch_a[l], w_branch_b[l], w_out[l],
                   norm_ffn_w[l], w_gate[l], w_up[l], w_down[l], norm_final_w,
                   final_norm=(l == depth - 1))
    return h.reshape(batch, seq, d)
```

```python
import functools

import jax
import jax.numpy as jnp
import numpy as np
from jax import lax
from jax.experimental import pallas as pl
from jax.experimental.pallas import tpu as pltpu

F32 = jnp.float32
BF16 = jnp.bfloat16

EPS = 1e-6
GDN_HEADS = 8
GDN_HEAD_DIM = 128
GDN_CHUNK = 64
FOX_HEADS = 16
FOX_HEAD_DIM = 64
LANES = 128
VMEM_LIMIT_BYTES = 56 * 1024 * 1024
NEG_BIG = -1e30


def _cparams(*semantics):
    return pltpu.CompilerParams(dimension_semantics=semantics,
                                vmem_limit_bytes=VMEM_LIMIT_BYTES)


def _split2(x):
    hi = x.astype(BF16)
    lo = (x - hi.astype(F32)).astype(BF16)
    return hi, lo


def _split3(x):
    hi = x.astype(BF16)
    r = x - hi.astype(F32)
    mid = r.astype(BF16)
    lo = (r - mid.astype(F32)).astype(BF16)
    return hi, mid, lo


def _dot(a, b):
    return jnp.dot(a, b, preferred_element_type=F32)


def _dot_exact_rhs(x, m_bf16):
    n = x.shape[0]
    y = _dot(jnp.concatenate(_split3(x), axis=0), m_bf16)
    return y[:n] + y[n:2 * n] + y[2 * n:]


def _dot_exact_lhs(m_bf16, x):
    n = x.shape[1]
    y = _dot(m_bf16, jnp.concatenate(_split3(x), axis=1))
    return y[:, :n] + y[:, n:2 * n] + y[:, 2 * n:]


def _div_pow2(x, n):
    assert n & (n - 1) == 0
    return jnp.right_shift(x, n.bit_length() - 1)


def _mod_pow2(x, n):
    assert n & (n - 1) == 0
    return jnp.bitwise_and(x, n - 1)


def _softplus(y):
    return jnp.maximum(y, 0.0) + jnp.log(1.0 + jnp.exp(-jnp.abs(y)))


def _sigmoid(y):
    return 1.0 / (1.0 + jnp.exp(-y))


def _silu(y):
    return y * _sigmoid(y)


def _rms_norm(x, w):
    ms = jnp.mean(x * x, axis=-1, keepdims=True)
    return x * lax.rsqrt(ms + EPS) * w


PROJ_CHUNK = 1024


def _norm_proj_kernel(x_ref, nw_ref, *refs, n_plain):
    n_w = len(refs) // 2
    h = _rms_norm(x_ref[...], nw_ref[...]).astype(BF16)
    for idx in range(n_w):
        w_ref, o_ref = refs[idx], refs[n_w + idx]
        if idx < n_plain:
            n = o_ref.shape[1]
            for c in range(0, n, PROJ_CHUNK):
                sl = slice(c, min(c + PROJ_CHUNK, n))
                o_ref[:, sl] = _dot(h, w_ref[:, sl]).astype(o_ref.dtype)
        else:
            n = o_ref.shape[0]
            for c in range(0, n, PROJ_CHUNK):
                sl = slice(c, min(c + PROJ_CHUNK, n))
                o_ref[sl, :] = lax.dot_general(w_ref[sl, :], h, (((1,), (1,)), ((), ())),
                                               preferred_element_type=F32).astype(o_ref.dtype)


def _norm_proj(x, nw, plain, transposed, tm):
    t, d = x.shape
    in_specs = [pl.BlockSpec((tm, d), lambda i: (i, 0)), pl.BlockSpec((1, d), lambda i: (0, 0))]
    out_specs, out_shape = [], []
    for w, dtype in plain:
        n = w.shape[1]
        in_specs.append(pl.BlockSpec((d, n), lambda i: (0, 0)))
        out_specs.append(pl.BlockSpec((tm, n), lambda i: (i, 0)))
        out_shape.append(jax.ShapeDtypeStruct((t, n), dtype))
    for wt, dtype in transposed:
        n = wt.shape[0]
        in_specs.append(pl.BlockSpec((n, d), lambda i: (0, 0)))
        out_specs.append(pl.BlockSpec((n, tm), lambda i: (0, i)))
        out_shape.append(jax.ShapeDtypeStruct((n, t), dtype))
    return pl.pallas_call(
        functools.partial(_norm_proj_kernel, n_plain=len(plain)),
        grid=(t // tm,),
        in_specs=in_specs,
        out_specs=out_specs,
        out_shape=out_shape,
        compiler_params=_cparams("parallel"),
        name="norm_proj",
    )(x, nw, *[w for w, _ in plain], *[w for w, _ in transposed])


SMALL_GA = 0
SMALL_GB = GDN_HEADS
SMALL_FF = 2 * GDN_HEADS
BIAS_TERMS = 3
BIAS_HEAD_STRIDE = 8


def _bias_placements():
    n_out = (FOX_HEADS // 2) * LANES
    pq = np.zeros((BIAS_TERMS, LANES, n_out), np.float32)
    pk = np.zeros((BIAS_TERMS, LANES, n_out), np.float32)
    ones_q = np.zeros((1, n_out), np.float32)
    ones_k = np.zeros((1, n_out), np.float32)
    for h in range(FOX_HEADS):
        base = (h // 2) * LANES + (h % 2) * BIAS_HEAD_STRIDE
        for t in range(BIAS_TERMS):
            pq[t, SMALL_FF + h, base + t] = 1.0
            pk[t, SMALL_FF + h, base + BIAS_TERMS + t] = -1.0
            ones_q[0, base + BIAS_TERMS + t] = 1.0
            ones_k[0, base + t] = 1.0
    return pq, pk, ones_q, ones_k


def _fox_bias_kernel(small_ref, fb_ref, pq_ref, pk_ref, oq_ref, ok_ref, eq_ref, ek_ref,
                     carry_ref):
    tm = small_ref.shape[0]

    @pl.when(pl.program_id(1) == 0)
    def _():
        carry_ref[...] = jnp.zeros_like(carry_ref)

    z = small_ref[...] + fb_ref[...]
    log_f = -_softplus(-z)
    row = lax.broadcasted_iota(jnp.int32, (tm, tm), 0)
    col = lax.broadcasted_iota(jnp.int32, (tm, tm), 1)
    tril = jnp.where(row >= col, 1.0, 0.0).astype(BF16)
    cum = _dot_exact_lhs(tril, log_f) + carry_ref[0:1, :]
    carry_ref[...] = jnp.broadcast_to(cum[tm - 1:tm, :], carry_ref.shape)
    terms = _split3(cum)
    eq = oq_ref[...]
    ek = ok_ref[...]
    for t in range(BIAS_TERMS):
        eq = eq + _dot(terms[t], pq_ref[t])
        ek = ek + _dot(terms[t], pk_ref[t])
    eq_ref[...] = eq.astype(BF16)
    ek_ref[...] = ek.astype(BF16)


def _fox_bias(small, fb_row, batch, seq, tm):
    t = small.shape[0]
    pq, pk, ones_q, ones_k = _bias_placements()
    n_out = pq.shape[2]
    nt = seq // tm
    const2 = lambda b, i: (0, 0)
    const3 = lambda b, i: (0, 0, 0)
    return pl.pallas_call(
        _fox_bias_kernel,
        grid=(batch, nt),
        in_specs=[pl.BlockSpec((tm, LANES), lambda b, i: (b * nt + i, 0)),
                  pl.BlockSpec((1, LANES), const2),
                  pl.BlockSpec((BIAS_TERMS, LANES, n_out), const3),
                  pl.BlockSpec((BIAS_TERMS, LANES, n_out), const3),
                  pl.BlockSpec((1, n_out), const2),
                  pl.BlockSpec((1, n_out), const2)],
        out_specs=[pl.BlockSpec((tm, n_out), lambda b, i: (b * nt + i, 0)),
                   pl.BlockSpec((tm, n_out), lambda b, i: (b * nt + i, 0))],
        out_shape=[jax.ShapeDtypeStruct((t, n_out), BF16),
                   jax.ShapeDtypeStruct((t, n_out), BF16)],
        scratch_shapes=[pltpu.VMEM((8, LANES), F32)],
        compiler_params=_cparams("parallel", "arbitrary"),
        name="fox_bias",
    )(small, fb_row, jnp.asarray(pq, BF16), jnp.asarray(pk, BF16),
      jnp.asarray(ones_q), jnp.asarray(ones_k))


FOX_PAIRS_PER_STEP = 4


def _fox_attention_kernel(q_ref, eq_ref, k_ref, ek_ref, vt_ref, fo_ref, o_ref, *, tq):
    i = pl.program_id(2)
    n_pairs = q_ref.shape[1] // LANES
    n_heads = 2 * n_pairs
    lane = lax.broadcasted_iota(jnp.int32, (1, LANES), 1)
    head_a = lane < FOX_HEAD_DIM
    bias_a = lane < BIAS_HEAD_STRIDE
    row_a = lax.broadcasted_iota(jnp.int32, (LANES, 1), 0) < FOX_HEAD_DIM
    zero = jnp.zeros((), BF16)
    one = jnp.ones((), BF16)
    qm = []
    for p in range(n_pairs):
        psl = slice(p * LANES, (p + 1) * LANES)
        q = q_ref[:, psl] * jnp.asarray(FOX_HEAD_DIM ** -0.5, BF16)
        eq = eq_ref[:, psl]
        qm.append(jnp.concatenate([jnp.where(head_a, q, zero), jnp.where(bias_a, eq, zero)], axis=1))
        qm.append(jnp.concatenate([jnp.where(head_a, zero, q), jnp.where(bias_a, zero, eq)], axis=1))

    def tile(j, carry, masked):
        start = pl.multiple_of(j * tq, tq)

        def score(idx):
            psl = slice((idx // 2) * LANES, (idx // 2 + 1) * LANES)
            kk = jnp.concatenate([k_ref[pl.ds(start, tq), psl], ek_ref[pl.ds(start, tq), psl]], axis=1)
            return lax.dot_general(kk, qm[idx], (((1,), (1,)), ((), ())),
                                   preferred_element_type=F32)

        def value(idx):
            p, h = divmod(idx, 2)
            vt = vt_ref[p * LANES:(p + 1) * LANES, pl.ds(start, tq)]
            return jnp.where(row_a, vt, one) if h == 0 else jnp.where(row_a, one, vt)

        out = []
        s_next = score(0)
        for idx in range(n_heads):
            s = s_next
            if idx + 1 < n_heads:
                s_next = score(idx + 1)
            m, acc = carry[idx]
            if masked:
                key = lax.broadcasted_iota(jnp.int32, (tq, tq), 0)
                qry = lax.broadcasted_iota(jnp.int32, (tq, tq), 1)
                s = jnp.where(key <= qry, s, NEG_BIG)
            m_new = jnp.maximum(m, jnp.max(s, axis=0, keepdims=True))
            alpha = jnp.exp(m - m_new)
            prob = jnp.exp(s - m_new).astype(BF16)
            acc = alpha * acc + _dot(value(idx), prob)
            out.append((m_new, acc))
        return tuple(out)

    init = tuple((jnp.full((1, tq), NEG_BIG, F32), jnp.zeros((LANES, tq), F32))
                 for _ in range(n_heads))
    carry = lax.fori_loop(0, i, lambda j, c: tile(j, c, False), init)
    final = tile(i, carry, True)
    for p in range(n_pairs):
        psl = slice(p * LANES, (p + 1) * LANES)
        acc_a, acc_b = final[2 * p][1], final[2 * p + 1][1]
        num = jnp.where(row_a, acc_a, acc_b)
        den = jnp.where(row_a, acc_a[FOX_HEAD_DIM:FOX_HEAD_DIM + 1, :], acc_b[0:1, :])
        gate = _sigmoid(fo_ref[:, psl].astype(F32))
        o_ref[:, psl] = ((num / den).T * gate).astype(o_ref.dtype)


def _fox_attention(fox, vt, eq, ek, batch, seq, tq):
    t = fox.shape[0]
    pairs = FOX_HEADS // 2
    steps = pairs // FOX_PAIRS_PER_STEP
    width = FOX_PAIRS_PER_STEP * LANES
    nq = seq // tq
    return pl.pallas_call(
        functools.partial(_fox_attention_kernel, tq=tq),
        grid=(batch, steps, nq),
        in_specs=[pl.BlockSpec((tq, width), lambda b, p, i: (b * nq + i, p)),
                  pl.BlockSpec((tq, width), lambda b, p, i: (b * nq + i, p)),
                  pl.BlockSpec((seq, width), lambda b, p, i: (b, steps + p)),
                  pl.BlockSpec((seq, width), lambda b, p, i: (b, p)),
                  pl.BlockSpec((width, seq), lambda b, p, i: (p, b)),
                  pl.BlockSpec((tq, width), lambda b, p, i: (b * nq + i, 2 * steps + p))],
        out_specs=pl.BlockSpec((tq, width), lambda b, p, i: (b * nq + i, p)),
        out_shape=jax.ShapeDtypeStruct((t, pairs * LANES), BF16),
        compiler_params=_cparams("parallel", "parallel", "arbitrary"),
        name="fox_attention",
    )(fox, eq, fox, ek, vt, fox)


GDN_WIDTH = GDN_HEADS * GDN_HEAD_DIM
GDN_GROUP = 4
GDN_CAT = GDN_HEADS * GDN_CHUNK
GDN_PAIRS = GDN_HEADS // 2
GDN_PAIR = 2 * GDN_HEAD_DIM
GDN_PREP_CHUNKS = 4
CONV_HALO = 8


def _gdn_expanders():
    e_g128 = np.zeros((LANES, GDN_WIDTH), np.float32)
    e_b128 = np.zeros((LANES, GDN_WIDTH), np.float32)
    e_g64 = np.zeros((LANES, GDN_CAT), np.float32)
    for h in range(GDN_HEADS):
        e_g128[SMALL_GA + h, h * GDN_HEAD_DIM:(h + 1) * GDN_HEAD_DIM] = 1.0
        e_b128[SMALL_GB + h, h * GDN_HEAD_DIM:(h + 1) * GDN_HEAD_DIM] = 1.0
        e_g64[SMALL_GA + h, h * GDN_CHUNK:(h + 1) * GDN_CHUNK] = 1.0
    return e_g128, e_b128, e_g64


def _block_diag_mask(n_blocks, rows_per_block, cols_per_block):
    shape = (n_blocks * rows_per_block, n_blocks * cols_per_block)
    r = _div_pow2(lax.broadcasted_iota(jnp.int32, shape, 0), rows_per_block)
    c = _div_pow2(lax.broadcasted_iota(jnp.int32, shape, 1), cols_per_block)
    return jnp.where(r == c, 1.0, 0.0).astype(BF16)


def _block_diag_rows(x, mask):
    return jnp.concatenate([x] * (mask.shape[0] // x.shape[0]), axis=0) * mask


def _rows(x, i, n):
    return x[i * n:(i + 1) * n]


def _headwise_products(lhs, b, mask):
    c = b.shape[0]
    n = len(lhs)
    b_hi, b_lo = _split2(b)
    bd_hi = _block_diag_rows(b_hi, mask)
    bd_lo = _block_diag_rows(b_lo, mask)
    parts = [_split2(x) for x in lhs]
    his = [p[0] for p in parts]
    los = [p[1] for p in parts]
    top = _dot(jnp.concatenate(his + los, axis=0), bd_hi)
    bot = _dot(jnp.concatenate(his, axis=0), bd_lo)
    return [_rows(top, i, c) + _rows(top, n + i, c) + _rows(bot, i, c) for i in range(n)]


def _unit_lower_inverses(l_cats):
    c, n = l_cats[0].shape
    r = lax.broadcasted_iota(jnp.int32, (c, n), 0)
    j = _mod_pow2(lax.broadcasted_iota(jnp.int32, (c, n), 1), c)
    eye = jnp.where(r == j, 1.0, 0.0)
    mask = _block_diag_mask(n // c, c, c)
    ss = [eye - l for l in l_cats]
    ps = [_headwise_products([l], l, mask)[0] for l in l_cats]
    k = 2
    while 2 * k < c:
        stage = [_headwise_products([p, s], p, mask) for p, s in zip(ps, ss)]
        ps = [st[0] for st in stage]
        ss = [s + st[1] for s, st in zip(ss, stage)]
        k *= 2
    return [s + _headwise_products([s], p, mask)[0] for p, s in zip(ps, ss)]


def _gdn_prep_kernel(q_ref, k_ref, v_ref, hq_ref, hk_ref, hv_ref, small_ref, cw_ref, alog_ref,
                     dtb_ref, eg128_ref, eb128_ref, eg64_ref,
                     u_ref, w_ref, qg_ref, kd_ref, qk_ref, gl_ref, *, blocks_per_seq):
    c = GDN_CHUNK
    n_conv = cw_ref.shape[1]
    seq_start = lax.rem(pl.program_id(0), blocks_per_seq) == 0

    halo_row = lax.broadcasted_iota(jnp.int32, (CONV_HALO, 1), 0)
    conv = []
    for s, (ref, halo_ref) in enumerate(((q_ref, hq_ref), (k_ref, hk_ref), (v_ref, hv_ref))):
        x = ref[...]
        halo = jnp.where(seq_start, 0.0, halo_ref[...])
        acc = x * cw_ref[s, n_conv - 1:n_conv, :]
        for shift in range(1, n_conv):
            moved = pltpu.roll(x, shift, axis=0)
            top = jnp.where(halo_row < shift, pltpu.roll(halo, shift, axis=0), moved[:CONV_HALO])
            moved = jnp.concatenate([top, moved[CONV_HALO:]], axis=0)
            acc = acc + moved * cw_ref[s, n_conv - 1 - shift:n_conv - shift, :]
        conv.append(_silu(acc))
    cq, ck, cv = conv
    rows = cq.shape[0]

    def l2n(x):
        parts = []
        for h in range(GDN_HEADS):
            xh = x[:, h * GDN_HEAD_DIM:(h + 1) * GDN_HEAD_DIM]
            ss = jnp.sum(xh * xh, axis=-1, keepdims=True)
            parts.append(xh * lax.rsqrt(ss + EPS))
        return jnp.concatenate(parts, axis=1)

    qn = l2n(cq)
    kn = l2n(ck)

    small = small_ref[...]
    g_tok = -jnp.exp(alog_ref[...]) * _softplus(small + dtb_ref[...])
    beta_tok = _sigmoid(small)
    row = lax.broadcasted_iota(jnp.int32, (rows, rows), 0)
    col = lax.broadcasted_iota(jnp.int32, (rows, rows), 1)
    same_chunk = _div_pow2(row, c) == _div_pow2(col, c)
    tril = jnp.where(row >= col, jnp.where(same_chunk, 1.0, 0.0), 0.0).astype(BF16)
    gc_tok = _dot_exact_lhs(tril, g_tok)
    gc128 = _dot_exact_rhs(gc_tok, eg128_ref[...])
    gc64 = _dot_exact_rhs(gc_tok, eg64_ref[...])
    beta128 = _dot_exact_rhs(beta_tok, eb128_ref[...])

    exp_gc = jnp.exp(gc128)
    kb = kn * beta128
    vb = cv * beta128
    kbg = kb * exp_gc
    qs = qn * (GDN_HEAD_DIM ** -0.5)
    qg_ref[...] = (qs * exp_gc).astype(BF16)
    kb16 = kb.astype(BF16)
    qs16 = qs.astype(BF16)
    kn16 = kn.astype(BF16)

    r_cat = lax.broadcasted_iota(jnp.int32, (c, GDN_CAT), 0)
    j_cat = _mod_pow2(lax.broadcasted_iota(jnp.int32, (c, GDN_CAT), 1), c)
    tri_cat = r_cat >= j_cat
    strict_cat = r_cat > j_cat
    lane_p = lax.broadcasted_iota(jnp.int32, (1, GDN_PAIR), 1)
    first = lane_p < GDN_HEAD_DIM
    group_w = GDN_GROUP * GDN_HEAD_DIM
    group_c = GDN_GROUP * c
    contract_last = (((1,), (1,)), ((), ()))
    k_mask = _block_diag_mask(GDN_GROUP, c, GDN_HEAD_DIM)

    n_chunks = rows // c
    n_groups = GDN_HEADS // GDN_GROUP
    l_cats = []
    for ch in range(n_chunks):
        rs = slice(ch * c, (ch + 1) * c)
        gc64_c = gc64[rs]
        gc_row = jnp.sum(jnp.where(r_cat == j_cat, gc64_c, 0.0), axis=0, keepdims=True)
        decay = jnp.where(tri_cat, jnp.exp(jnp.where(tri_cat, gc64_c - gc_row, 0.0)), 0.0)
        g_last = gc128[ch * c + c - 1:ch * c + c, :]
        gl_ref[ch] = jnp.exp(g_last)
        kd_ref[rs, :] = (kn[rs] * jnp.exp(g_last - gc128[rs])).astype(BF16)

        for gidx in range(n_groups):
            ksl = slice(gidx * group_w, (gidx + 1) * group_w)
            csl = slice(gidx * group_c, (gidx + 1) * group_c)
            bd_k = _block_diag_rows(kn16[rs, ksl], k_mask)
            both = lax.dot_general(jnp.concatenate([kb16[rs, ksl], qs16[rs, ksl]], axis=0), bd_k,
                                   contract_last, preferred_element_type=F32)
            dec = decay[:, csl]
            l_cats.append(jnp.where(strict_cat[:, csl], both[:c] * dec, 0.0))
            qk_ref[rs, csl] = (both[c:] * dec).astype(BF16)

    invs = _unit_lower_inverses(l_cats)

    for ch in range(n_chunks):
        rs = slice(ch * c, (ch + 1) * c)
        inv_cat = jnp.concatenate(invs[ch * n_groups:(ch + 1) * n_groups], axis=1)
        for p in range(GDN_PAIRS):
            wsl = slice(p * GDN_PAIR, (p + 1) * GDN_PAIR)
            inv_p = inv_cat[:, p * 2 * c:(p + 1) * 2 * c]
            vb_p, kbg_p = vb[rs, wsl], kbg[rs, wsl]
            rhs = jnp.concatenate(
                [jnp.concatenate([jnp.where(first, vb_p, 0.0), jnp.where(first, kbg_p, 0.0)], axis=1),
                 jnp.concatenate([jnp.where(first, 0.0, vb_p), jnp.where(first, 0.0, kbg_p)], axis=1)],
                axis=0)
            i_hi, i_lo = _split2(inv_p)
            r_hi, r_lo = _split2(rhs)
            top = _dot(jnp.concatenate([i_hi, i_lo], axis=0), r_hi)
            sol = top[:c] + top[c:] + _dot(i_hi, r_lo)
            u_ref[rs, wsl] = sol[:, :GDN_PAIR]
            w_ref[rs, wsl] = sol[:, GDN_PAIR:].astype(BF16)


def _gdn_scan_kernel(u_ref, w_ref, qg_ref, kd_ref, qk_ref, gl_ref, z_ref, nw_ref, o_ref, state_ref):
    c = GDN_CHUNK

    @pl.when(pl.program_id(0) == 0)
    def _():
        state_ref[...] = jnp.zeros_like(state_ref)

    lane_p = lax.broadcasted_iota(jnp.int32, (1, GDN_PAIR), 1)
    first = lane_p < GDN_HEAD_DIM
    bm_r = _div_pow2(lax.broadcasted_iota(jnp.int32, (GDN_PAIR, GDN_PAIR), 0), GDN_HEAD_DIM)
    bm_c = _div_pow2(lax.broadcasted_iota(jnp.int32, (GDN_PAIR, GDN_PAIR), 1), GDN_HEAD_DIM)
    same_head = bm_r == bm_c
    nw = nw_ref[...]
    zero = jnp.zeros((), BF16)
    cells = [(b, p) for b in range(u_ref.shape[0]) for p in range(GDN_PAIRS)]
    wsl = lambda p: slice(p * GDN_PAIR, (p + 1) * GDN_PAIR)
    states = [state_ref[b, p] for b, p in cells]
    boths = [_dot(jnp.concatenate([w_ref[b, :, wsl(p)], qg_ref[b, :, wsl(p)]], axis=0),
                  st.astype(BF16)) for (b, p), st in zip(cells, states)]
    v16s = [(u_ref[b, :, wsl(p)] - both[:c]).astype(BF16) for (b, p), both in zip(cells, boths)]
    outs, upds = [], []
    for (b, p), both, v16 in zip(cells, boths, v16s):
        v_bd = jnp.concatenate([jnp.where(first, v16, zero), jnp.where(first, zero, v16)], axis=0)
        outs.append(both[c:] + _dot(qk_ref[b, :, p * 2 * c:(p + 1) * 2 * c], v_bd))
        upds.append(lax.dot_general(kd_ref[b, :, wsl(p)], v16, (((0,), (0,)), ((), ())),
                                    preferred_element_type=F32))
    for (b, p), st, o, upd in zip(cells, states, outs, upds):
        state_ref[b, p] = st * gl_ref[b, 0][:, wsl(p)] + jnp.where(same_head, upd, 0.0)
        halves = []
        for h in range(2):
            oh = o[:, h * GDN_HEAD_DIM:(h + 1) * GDN_HEAD_DIM]
            ms = jnp.mean(oh * oh, axis=-1, keepdims=True)
            halves.append(oh * lax.rsqrt(ms + EPS) * nw)
        o_ref[b, :, wsl(p)] = (jnp.concatenate(halves, axis=1)
                               * _silu(z_ref[b, :, wsl(p)])).astype(o_ref.dtype)


def _gdn(gdn, small, conv_w3, alog_row, dtb_row, nw_row, batch, seq):
    t = gdn.shape[0]
    c = GDN_CHUNK
    w = GDN_WIDTH
    rows = GDN_PREP_CHUNKS * c
    e_g128, e_b128, e_g64 = _gdn_expanders()
    n_conv = conv_w3.shape[1]
    tok = lambda col: (lambda i: (i, col))
    halo = lambda col: (lambda i: (jnp.maximum(i * (rows // CONV_HALO) - 1, 0), col))
    const2 = lambda i: (0, 0)
    u, wk, qg, kd, qk, gl = pl.pallas_call(
        functools.partial(_gdn_prep_kernel, blocks_per_seq=seq // rows),
        grid=(t // rows,),
        in_specs=[pl.BlockSpec((rows, w), tok(0)),
                  pl.BlockSpec((rows, w), tok(1)),
                  pl.BlockSpec((rows, w), tok(2)),
                  pl.BlockSpec((CONV_HALO, w), halo(0)),
                  pl.BlockSpec((CONV_HALO, w), halo(1)),
                  pl.BlockSpec((CONV_HALO, w), halo(2)),
                  pl.BlockSpec((rows, LANES), tok(0)),
                  pl.BlockSpec((3, n_conv, w), lambda i: (0, 0, 0)),
                  pl.BlockSpec((1, LANES), const2),
                  pl.BlockSpec((1, LANES), const2),
                  pl.BlockSpec((LANES, w), const2),
                  pl.BlockSpec((LANES, w), const2),
                  pl.BlockSpec((LANES, GDN_CAT), const2)],
        out_specs=[pl.BlockSpec((rows, w), tok(0)),
                   pl.BlockSpec((rows, w), tok(0)),
                   pl.BlockSpec((rows, w), tok(0)),
                   pl.BlockSpec((rows, w), tok(0)),
                   pl.BlockSpec((rows, GDN_CAT), tok(0)),
                   pl.BlockSpec((GDN_PREP_CHUNKS, 1, w), lambda i: (i, 0, 0))],
        out_shape=[jax.ShapeDtypeStruct((t, w), F32),
                   jax.ShapeDtypeStruct((t, w), BF16),
                   jax.ShapeDtypeStruct((t, w), BF16),
                   jax.ShapeDtypeStruct((t, w), BF16),
                   jax.ShapeDtypeStruct((t, GDN_CAT), BF16),
                   jax.ShapeDtypeStruct((t // c, 1, w), F32)],
        compiler_params=_cparams("parallel"),
        name="gdn_prep",
    )(gdn, gdn, gdn, gdn, gdn, gdn, small, conv_w3, alog_row, dtb_row,
      jnp.asarray(e_g128, BF16), jnp.asarray(e_b128, BF16), jnp.asarray(e_g64, BF16))

    seq3 = lambda a: a.reshape(batch, seq, a.shape[-1])
    blk = lambda width, col=0: pl.BlockSpec((batch, c, width), lambda n: (0, n, col))
    o = pl.pallas_call(
        _gdn_scan_kernel,
        grid=(seq // c,),
        in_specs=[blk(w), blk(w), blk(w), blk(w), blk(GDN_CAT),
                  pl.BlockSpec((batch, 1, 1, w), lambda n: (0, n, 0, 0)),
                  blk(w, 3),
                  pl.BlockSpec((1, GDN_HEAD_DIM), lambda n: (0, 0))],
        out_specs=blk(w),
        out_shape=jax.ShapeDtypeStruct((batch, seq, w), BF16),
        scratch_shapes=[pltpu.VMEM((batch, GDN_PAIRS, GDN_PAIR, GDN_PAIR), F32)],
        compiler_params=_cparams("arbitrary"),
        name="gdn_scan",
    )(seq3(u), seq3(wk), seq3(qg), seq3(kd), seq3(qk), gl.reshape(batch, seq // c, 1, w),
      seq3(gdn), nw_row)
    return o.reshape(t, w)


def _merge_out_kernel(x_ref, oa_ref, ob_ref, ga_ref, gb_ref, wa_ref, wb_ref, wo_ref, o_ref):
    ya = _dot(oa_ref[...], wa_ref[...])
    yb = _dot(ob_ref[...], wb_ref[...])
    y = (_sigmoid(ga_ref[...].astype(F32)) * ya + _sigmoid(gb_ref[...].astype(F32)) * yb)
    o_ref[...] = x_ref[...] + _dot(y.astype(BF16), wo_ref[...])


def _merge_out(x, o_a, o_b, mg, w_a, w_b, w_o, tm):
    t, d = x.shape
    tokd = pl.BlockSpec((tm, d), lambda i: (i, 0))
    wspec = pl.BlockSpec((d, d), lambda i: (0, 0))
    return pl.pallas_call(
        _merge_out_kernel,
        grid=(t // tm,),
        in_specs=[tokd, tokd, tokd,
                  pl.BlockSpec((tm, d), lambda i: (i, 0)),
                  pl.BlockSpec((tm, d), lambda i: (i, 1)),
                  wspec, wspec, wspec],
        out_specs=tokd,
        out_shape=jax.ShapeDtypeStruct((t, d), F32),
        compiler_params=_cparams("parallel"),
        name="merge_out",
    )(x, o_a, o_b, mg, mg, w_a, w_b, w_o)


def _ffn_kernel(x_ref, nw_ref, wg_ref, wu_ref, wd_ref, fw_ref, o_ref, *, final_norm):
    x = x_ref[...]
    h = _rms_norm(x, nw_ref[...]).astype(BF16)
    a = (_silu(_dot(h, wg_ref[...])) * _dot(h, wu_ref[...])).astype(BF16)
    y = x + _dot(a, wd_ref[...])
    if final_norm:
        y = _rms_norm(y, fw_ref[...])
    o_ref[...] = y


def _ffn(x, nw, w_g, w_u, w_d, fw, final_norm, tm):
    t, d = x.shape
    f = w_g.shape[1]
    tokd = pl.BlockSpec((tm, d), lambda i: (i, 0))
    rowd = pl.BlockSpec((1, d), lambda i: (0, 0))
    return pl.pallas_call(
        functools.partial(_ffn_kernel, final_norm=final_norm),
        grid=(t // tm,),
        in_specs=[tokd, rowd,
                  pl.BlockSpec((d, f), lambda i: (0, 0)),
                  pl.BlockSpec((d, f), lambda i: (0, 0)),
                  pl.BlockSpec((f, d), lambda i: (0, 0)),
                  rowd],
        out_specs=tokd,
        out_shape=jax.ShapeDtypeStruct((t, d), F32),
        compiler_params=_cparams("parallel"),
        name="ffn",
    )(x, nw, w_g, w_u, w_d, fw)


def _pad_row(v, offset):
    return jnp.zeros((1, LANES), F32).at[0, offset:offset + v.shape[0]].set(v.astype(F32))


def _layer(x, batch, seq, norm_mix_w, w_in, conv_w, a_log, dt_bias, gdn_norm_w, fox_f_bias,
           w_branch_a, w_branch_b, w_out, norm_ffn_w, w_gate, w_up, w_down, final_w, final_norm):
    d = x.shape[1]
    gw, fw = GDN_WIDTH, FOX_HEADS * FOX_HEAD_DIM
    sizes = (gw, gw, gw, gw, GDN_HEADS, GDN_HEADS, fw, fw, fw, FOX_HEADS, fw, d, d)
    offs = np.concatenate([[0], np.cumsum(sizes)])
    col = lambda i: w_in[:, offs[i]:offs[i + 1]]
    w_gdn = jnp.concatenate([col(0), col(1), col(2), col(3)], axis=1).astype(BF16)
    w_fox = jnp.concatenate([col(6), col(7), col(10)], axis=1).astype(BF16)
    w_fv_t = col(8).T.astype(BF16)
    w_mg = jnp.concatenate([col(11), col(12)], axis=1).astype(BF16)
    n_small = 2 * GDN_HEADS + FOX_HEADS
    w_small = jnp.concatenate([col(4), col(5), col(9), jnp.zeros((d, LANES - n_small), F32)],
                              axis=1).astype(BF16)
    nw = norm_mix_w.reshape(1, d)

    (gdn,) = _norm_proj(x, nw, [(w_gdn, F32)], [], tm=512)
    fox, vt = _norm_proj(x, nw, [(w_fox, BF16)], [(w_fv_t, BF16)], tm=512)
    mg, small = _norm_proj(x, nw, [(w_mg, BF16), (w_small, F32)], [], tm=512)

    eq, ek = _fox_bias(small, _pad_row(fox_f_bias, SMALL_FF), batch, seq, tm=min(seq, 512))
    o_b = _fox_attention(fox, vt, eq, ek, batch, seq, tq=min(seq, 512))

    conv_w3 = conv_w.reshape(conv_w.shape[0], 3, gw).transpose(1, 0, 2)
    o_a = _gdn(gdn, small, conv_w3, _pad_row(a_log, SMALL_GA), _pad_row(dt_bias, SMALL_GA),
               gdn_norm_w.reshape(1, GDN_HEAD_DIM), batch, seq)

    x1 = _merge_out(x, o_a, o_b, mg, w_branch_a.astype(BF16), w_branch_b.astype(BF16),
                    w_out.astype(BF16), tm=512)
    return _ffn(x1, norm_ffn_w.reshape(1, d), w_gate.astype(BF16), w_up.astype(BF16),
                w_down.astype(BF16), final_w.reshape(1, d), final_norm, tm=256)


def kernel(x, norm_mix_w, w_in, conv_w, a_log, dt_bias, gdn_norm_w, fox_f_bias, w_branch_a,
           w_branch_b, w_out, norm_ffn_w, w_gate, w_up, w_down, norm_final_w):
    batch, seq, d = x.shape
    depth = w_in.shape[0]
    h = x.reshape(batch * seq, d)
    for l in range(depth):
        h = _layer(h, batch, seq, norm_mix_w[l], w_in[l], conv_w[l], a_log[l], dt_bias[l],
                   gdn_norm_w[l], fox_f_bias[l], w_branch_a[l], w_branch_b[l], w_out[l],
                   norm_ffn_w[l], w_gate[l], w_up[l], w_down[l], norm_final_w,
                   final_norm=(l == depth - 1))
    return h.reshape(batch, seq, d)
```

```python
import functools

import jax
import jax.numpy as jnp
import numpy as np
from jax import lax
from jax.experimental import pallas as pl
from jax.experimental.pallas import tpu as pltpu

F32 = jnp.float32
BF16 = jnp.bfloat16

EPS = 1e-6
GDN_HEADS = 8
GDN_HEAD_DIM = 128
GDN_CHUNK = 64
FOX_HEADS = 16
FOX_HEAD_DIM = 64
LANES = 128
VMEM_LIMIT_BYTES = 56 * 1024 * 1024
NEG_BIG = -1e30


def _cparams(*semantics):
    return pltpu.CompilerParams(dimension_semantics=semantics,
                                vmem_limit_bytes=VMEM_LIMIT_BYTES)


def _split2(x):
    hi = x.astype(BF16)
    lo = (x - hi.astype(F32)).astype(BF16)
    return hi, lo


def _split3(x):
    hi = x.astype(BF16)
    r = x - hi.astype(F32)
    mid = r.astype(BF16)
    lo = (r - mid.astype(F32)).astype(BF16)
    return hi, mid, lo


def _dot(a, b):
    return jnp.dot(a, b, preferred_element_type=F32)


def _dot_exact_rhs(x, m_bf16):
    n = x.shape[0]
    y = _dot(jnp.concatenate(_split3(x), axis=0), m_bf16)
    return y[:n] + y[n:2 * n] + y[2 * n:]


def _dot_exact_lhs(m_bf16, x):
    n = x.shape[1]
    y = _dot(m_bf16, jnp.concatenate(_split3(x), axis=1))
    return y[:, :n] + y[:, n:2 * n] + y[:, 2 * n:]


def _div_pow2(x, n):
    assert n & (n - 1) == 0
    return jnp.right_shift(x, n.bit_length() - 1)


def _mod_pow2(x, n):
    assert n & (n - 1) == 0
    return jnp.bitwise_and(x, n - 1)


def _softplus(y):
    return jnp.maximum(y, 0.0) + jnp.log(1.0 + jnp.exp(-jnp.abs(y)))


def _sigmoid(y):
    return 1.0 / (1.0 + jnp.exp(-y))


def _silu(y):
    return y * _sigmoid(y)


def _rms_norm(x, w):
    ms = jnp.mean(x * x, axis=-1, keepdims=True)
    return x * lax.rsqrt(ms + EPS) * w


PROJ_CHUNK = 1024


def _norm_proj_kernel(x_ref, nw_ref, *refs, n_plain):
    n_w = len(refs) // 2
    h = _rms_norm(x_ref[...], nw_ref[...]).astype(BF16)
    for idx in range(n_w):
        w_ref, o_ref = refs[idx], refs[n_w + idx]
        if idx < n_plain:
            n = o_ref.shape[1]
            for c in range(0, n, PROJ_CHUNK):
                sl = slice(c, min(c + PROJ_CHUNK, n))
                o_ref[:, sl] = _dot(h, w_ref[:, sl]).astype(o_ref.dtype)
        else:
            n = o_ref.shape[0]
            for c in range(0, n, PROJ_CHUNK):
                sl = slice(c, min(c + PROJ_CHUNK, n))
                o_ref[sl, :] = lax.dot_general(w_ref[sl, :], h, (((1,), (1,)), ((), ())),
                                               preferred_element_type=F32).astype(o_ref.dtype)


def _norm_proj(x, nw, plain, transposed, tm):
    t, d = x.shape
    in_specs = [pl.BlockSpec((tm, d), lambda i: (i, 0)), pl.BlockSpec((1, d), lambda i: (0, 0))]
    out_specs, out_shape = [], []
    for w, dtype in plain:
        n = w.shape[1]
        in_specs.append(pl.BlockSpec((d, n), lambda i: (0, 0)))
        out_specs.append(pl.BlockSpec((tm, n), lambda i: (i, 0)))
        out_shape.append(jax.ShapeDtypeStruct((t, n), dtype))
    for wt, dtype in transposed:
        n = wt.shape[0]
        in_specs.append(pl.BlockSpec((n, d), lambda i: (0, 0)))
        out_specs.append(pl.BlockSpec((n, tm), lambda i: (0, i)))
        out_shape.append(jax.ShapeDtypeStruct((n, t), dtype))
    return pl.pallas_call(
        functools.partial(_norm_proj_kernel, n_plain=len(plain)),
        grid=(t // tm,),
        in_specs=in_specs,
        out_specs=out_specs,
        out_shape=out_shape,
        compiler_params=_cparams("parallel"),
        name="norm_proj",
    )(x, nw, *[w for w, _ in plain], *[w for w, _ in transposed])


SMALL_GA = 0
SMALL_GB = GDN_HEADS
SMALL_FF = 2 * GDN_HEADS
BIAS_TERMS = 3
BIAS_HEAD_STRIDE = 8


def _bias_placements():
    n_out = (FOX_HEADS // 2) * LANES
    pq = np.zeros((BIAS_TERMS * LANES, n_out), np.float32)
    pk = np.zeros((BIAS_TERMS * LANES, n_out), np.float32)
    ones_q = np.zeros((1, n_out), np.float32)
    ones_k = np.zeros((1, n_out), np.float32)
    for h in range(FOX_HEADS):
        base = (h // 2) * LANES + (h % 2) * BIAS_HEAD_STRIDE
        for t in range(BIAS_TERMS):
            pq[t * LANES + SMALL_FF + h, base + t] = 1.0
            pk[t * LANES + SMALL_FF + h, base + BIAS_TERMS + t] = -1.0
            ones_q[0, base + BIAS_TERMS + t] = 1.0
            ones_k[0, base + t] = 1.0
    return pq, pk, ones_q, ones_k


def _fox_bias_kernel(small_ref, fb_ref, pq_ref, pk_ref, oq_ref, ok_ref, eq_ref, ek_ref,
                     carry_ref):
    tm = small_ref.shape[0]

    @pl.when(pl.program_id(1) == 0)
    def _():
        carry_ref[...] = jnp.zeros_like(carry_ref)

    z = small_ref[...] + fb_ref[...]
    log_f = -_softplus(-z)
    row = lax.broadcasted_iota(jnp.int32, (tm, tm), 0)
    col = lax.broadcasted_iota(jnp.int32, (tm, tm), 1)
    tril = jnp.where(row >= col, 1.0, 0.0).astype(BF16)
    cum = _dot_exact_lhs(tril, log_f) + carry_ref[0:1, :]
    carry_ref[...] = jnp.broadcast_to(cum[tm - 1:tm, :], carry_ref.shape)
    terms = jnp.concatenate(_split3(cum), axis=1)
    eq_ref[...] = (oq_ref[...] + _dot(terms, pq_ref[...])).astype(BF16)
    ek_ref[...] = (ok_ref[...] + _dot(terms, pk_ref[...])).astype(BF16)


def _fox_bias(small, fb_row, batch, seq, tm):
    t = small.shape[0]
    pq, pk, ones_q, ones_k = _bias_placements()
    n_out = pq.shape[1]
    nt = seq // tm
    const2 = lambda b, i: (0, 0)
    return pl.pallas_call(
        _fox_bias_kernel,
        grid=(batch, nt),
        in_specs=[pl.BlockSpec((tm, LANES), lambda b, i: (b * nt + i, 0)),
                  pl.BlockSpec((1, LANES), const2),
                  pl.BlockSpec((BIAS_TERMS * LANES, n_out), const2),
                  pl.BlockSpec((BIAS_TERMS * LANES, n_out), const2),
                  pl.BlockSpec((1, n_out), const2),
                  pl.BlockSpec((1, n_out), const2)],
        out_specs=[pl.BlockSpec((tm, n_out), lambda b, i: (b * nt + i, 0)),
                   pl.BlockSpec((tm, n_out), lambda b, i: (b * nt + i, 0))],
        out_shape=[jax.ShapeDtypeStruct((t, n_out), BF16),
                   jax.ShapeDtypeStruct((t, n_out), BF16)],
        scratch_shapes=[pltpu.VMEM((8, LANES), F32)],
        compiler_params=_cparams("parallel", "arbitrary"),
        name="fox_bias",
    )(small, fb_row, jnp.asarray(pq, BF16), jnp.asarray(pk, BF16),
      jnp.asarray(ones_q), jnp.asarray(ones_k))


FOX_PAIRS_PER_STEP = 4
FOX_SCORES_AHEAD = 3


def _fox_attention_kernel(q_ref, eq_ref, k_ref, ek_ref, vt_ref, fo_ref, o_ref, *, tq):
    i = pl.program_id(2)
    n_pairs = q_ref.shape[1] // LANES
    n_heads = 2 * n_pairs
    lane = lax.broadcasted_iota(jnp.int32, (1, LANES), 1)
    head_a = lane < FOX_HEAD_DIM
    bias_a = lane < BIAS_HEAD_STRIDE
    row_a = lax.broadcasted_iota(jnp.int32, (LANES, 1), 0) < FOX_HEAD_DIM
    zero = jnp.zeros((), BF16)
    one = jnp.ones((), BF16)
    qm = []
    for p in range(n_pairs):
        psl = slice(p * LANES, (p + 1) * LANES)
        q = q_ref[:, psl] * jnp.asarray(FOX_HEAD_DIM ** -0.5, BF16)
        eq = eq_ref[:, psl]
        qm.append(jnp.concatenate([jnp.where(head_a, q, zero), jnp.where(bias_a, eq, zero)], axis=1))
        qm.append(jnp.concatenate([jnp.where(head_a, zero, q), jnp.where(bias_a, zero, eq)], axis=1))

    def tile(j, carry, masked):
        start = pl.multiple_of(j * tq, tq)

        def score(idx):
            psl = slice((idx // 2) * LANES, (idx // 2 + 1) * LANES)
            kk = jnp.concatenate([k_ref[pl.ds(start, tq), psl], ek_ref[pl.ds(start, tq), psl]], axis=1)
            return lax.dot_general(kk, qm[idx], (((1,), (1,)), ((), ())),
                                   preferred_element_type=F32)

        def value(idx):
            p, h = divmod(idx, 2)
            vt = vt_ref[p * LANES:(p + 1) * LANES, pl.ds(start, tq)]
            return jnp.where(row_a, vt, one) if h == 0 else jnp.where(row_a, one, vt)

        out = []
        ahead = [score(idx) for idx in range(min(FOX_SCORES_AHEAD, n_heads))]
        for idx in range(n_heads):
            s = ahead.pop(0)
            if idx + FOX_SCORES_AHEAD < n_heads:
                ahead.append(score(idx + FOX_SCORES_AHEAD))
            m, acc = carry[idx]
            if masked:
                key = lax.broadcasted_iota(jnp.int32, (tq, tq), 0)
                qry = lax.broadcasted_iota(jnp.int32, (tq, tq), 1)
                s = jnp.where(key <= qry, s, NEG_BIG)
            m_new = jnp.maximum(m, jnp.max(s, axis=0, keepdims=True))
            alpha = jnp.exp(m - m_new)
            prob = jnp.exp(s - m_new).astype(BF16)
            acc = alpha * acc + _dot(value(idx), prob)
            out.append((m_new, acc))
        return tuple(out)

    init = tuple((jnp.full((1, tq), NEG_BIG, F32), jnp.zeros((LANES, tq), F32))
                 for _ in range(n_heads))
    carry = lax.fori_loop(0, i, lambda j, c: tile(j, c, False), init)
    final = tile(i, carry, True)
    for p in range(n_pairs):
        psl = slice(p * LANES, (p + 1) * LANES)
        acc_a, acc_b = final[2 * p][1], final[2 * p + 1][1]
        num = jnp.where(row_a, acc_a, acc_b)
        den = jnp.where(row_a, acc_a[FOX_HEAD_DIM:FOX_HEAD_DIM + 1, :], acc_b[0:1, :])
        gate = _sigmoid(fo_ref[:, psl].astype(F32))
        o_ref[:, psl] = ((num / den).T * gate).astype(o_ref.dtype)


def _fox_attention(fox, vt, eq, ek, batch, seq, tq):
    t = fox.shape[0]
    pairs = FOX_HEADS // 2
    steps = pairs // FOX_PAIRS_PER_STEP
    width = FOX_PAIRS_PER_STEP * LANES
    nq = seq // tq
    return pl.pallas_call(
        functools.partial(_fox_attention_kernel, tq=tq),
        grid=(batch, steps, nq),
        in_specs=[pl.BlockSpec((tq, width), lambda b, p, i: (b * nq + i, p)),
                  pl.BlockSpec((tq, width), lambda b, p, i: (b * nq + i, p)),
                  pl.BlockSpec((seq, width), lambda b, p, i: (b, steps + p)),
                  pl.BlockSpec((seq, width), lambda b, p, i: (b, p)),
                  pl.BlockSpec((width, seq), lambda b, p, i: (p, b)),
                  pl.BlockSpec((tq, width), lambda b, p, i: (b * nq + i, 2 * steps + p))],
        out_specs=pl.BlockSpec((tq, width), lambda b, p, i: (b * nq + i, p)),
        out_shape=jax.ShapeDtypeStruct((t, pairs * LANES), BF16),
        compiler_params=_cparams("parallel", "parallel", "arbitrary"),
        name="fox_attention",
    )(fox, eq, fox, ek, vt, fox)


GDN_WIDTH = GDN_HEADS * GDN_HEAD_DIM
GDN_GROUP = 4
GDN_CAT = GDN_HEADS * GDN_CHUNK
GDN_PAIRS = GDN_HEADS // 2
GDN_PAIR = 2 * GDN_HEAD_DIM
GDN_PREP_CHUNKS = 4
CONV_HALO = 8


def _gdn_expanders():
    e_g128 = np.zeros((LANES, GDN_WIDTH), np.float32)
    e_b128 = np.zeros((LANES, GDN_WIDTH), np.float32)
    e_g64 = np.zeros((LANES, GDN_CAT), np.float32)
    for h in range(GDN_HEADS):
        e_g128[SMALL_GA + h, h * GDN_HEAD_DIM:(h + 1) * GDN_HEAD_DIM] = 1.0
        e_b128[SMALL_GB + h, h * GDN_HEAD_DIM:(h + 1) * GDN_HEAD_DIM] = 1.0
        e_g64[SMALL_GA + h, h * GDN_CHUNK:(h + 1) * GDN_CHUNK] = 1.0
    return e_g128, e_b128, e_g64


def _block_diag_mask(n_blocks, rows_per_block, cols_per_block):
    shape = (n_blocks * rows_per_block, n_blocks * cols_per_block)
    r = _div_pow2(lax.broadcasted_iota(jnp.int32, shape, 0), rows_per_block)
    c = _div_pow2(lax.broadcasted_iota(jnp.int32, shape, 1), cols_per_block)
    return jnp.where(r == c, 1.0, 0.0).astype(BF16)


def _block_diag_rows(x, mask):
    return jnp.concatenate([x] * (mask.shape[0] // x.shape[0]), axis=0) * mask


def _rows(x, i, n):
    return x[i * n:(i + 1) * n]


def _headwise_products(lhs, b, mask):
    c = b.shape[0]
    n = len(lhs)
    b_hi, b_lo = _split2(b)
    bd_hi = _block_diag_rows(b_hi, mask)
    bd_lo = _block_diag_rows(b_lo, mask)
    parts = [_split2(x) for x in lhs]
    his = [p[0] for p in parts]
    los = [p[1] for p in parts]
    top = _dot(jnp.concatenate(his + los, axis=0), bd_hi)
    bot = _dot(jnp.concatenate(his, axis=0), bd_lo)
    return [_rows(top, i, c) + _rows(top, n + i, c) + _rows(bot, i, c) for i in range(n)]


def _unit_lower_inverses(l_cats):
    c, n = l_cats[0].shape
    r = lax.broadcasted_iota(jnp.int32, (c, n), 0)
    j = _mod_pow2(lax.broadcasted_iota(jnp.int32, (c, n), 1), c)
    eye = jnp.where(r == j, 1.0, 0.0)
    mask = _block_diag_mask(n // c, c, c)
    ss = [eye - l for l in l_cats]
    ps = [_headwise_products([l], l, mask)[0] for l in l_cats]
    k = 2
    while 2 * k < c:
        stage = [_headwise_products([p, s], p, mask) for p, s in zip(ps, ss)]
        ps = [st[0] for st in stage]
        ss = [s + st[1] for s, st in zip(ss, stage)]
        k *= 2
    return [s + _headwise_products([s], p, mask)[0] for p, s in zip(ps, ss)]


def _gdn_prep_kernel(q_ref, k_ref, v_ref, hq_ref, hk_ref, hv_ref, small_ref, cw_ref, alog_ref,
                     dtb_ref, eg128_ref, eb128_ref, eg64_ref,
                     u_ref, w_ref, qg_ref, kd_ref, qk_ref, gl_ref, *, blocks_per_seq):
    c = GDN_CHUNK
    n_conv = cw_ref.shape[1]
    seq_start = lax.rem(pl.program_id(0), blocks_per_seq) == 0

    halo_row = lax.broadcasted_iota(jnp.int32, (CONV_HALO, 1), 0)
    conv = []
    for s, (ref, halo_ref) in enumerate(((q_ref, hq_ref), (k_ref, hk_ref), (v_ref, hv_ref))):
        x = ref[...]
        halo = jnp.where(seq_start, 0.0, halo_ref[...])
        acc = x * cw_ref[s, n_conv - 1:n_conv, :]
        for shift in range(1, n_conv):
            moved = pltpu.roll(x, shift, axis=0)
            top = jnp.where(halo_row < shift, pltpu.roll(halo, shift, axis=0), moved[:CONV_HALO])
            moved = jnp.concatenate([top, moved[CONV_HALO:]], axis=0)
            acc = acc + moved * cw_ref[s, n_conv - 1 - shift:n_conv - shift, :]
        conv.append(_silu(acc))
    cq, ck, cv = conv
    rows = cq.shape[0]

    def l2n(x):
        parts = []
        for h in range(GDN_HEADS):
            xh = x[:, h * GDN_HEAD_DIM:(h + 1) * GDN_HEAD_DIM]
            ss = jnp.sum(xh * xh, axis=-1, keepdims=True)
            parts.append(xh * lax.rsqrt(ss + EPS))
        return jnp.concatenate(parts, axis=1)

    qn = l2n(cq)
    kn = l2n(ck)

    small = small_ref[...]
    g_tok = -jnp.exp(alog_ref[...]) * _softplus(small + dtb_ref[...])
    beta_tok = _sigmoid(small)
    row = lax.broadcasted_iota(jnp.int32, (rows, rows), 0)
    col = lax.broadcasted_iota(jnp.int32, (rows, rows), 1)
    same_chunk = _div_pow2(row, c) == _div_pow2(col, c)
    tril = jnp.where(row >= col, jnp.where(same_chunk, 1.0, 0.0), 0.0).astype(BF16)
    gc_tok = _dot_exact_lhs(tril, g_tok)
    gc128 = _dot_exact_rhs(gc_tok, eg128_ref[...])
    gc64 = _dot_exact_rhs(gc_tok, eg64_ref[...])
    beta128 = _dot_exact_rhs(beta_tok, eb128_ref[...])

    exp_gc = jnp.exp(gc128)
    kb = kn * beta128
    vb = cv * beta128
    kbg = kb * exp_gc
    qs = qn * (GDN_HEAD_DIM ** -0.5)
    qg_ref[...] = (qs * exp_gc).astype(BF16)
    kb16 = kb.astype(BF16)
    qs16 = qs.astype(BF16)
    kn16 = kn.astype(BF16)

    r_cat = lax.broadcasted_iota(jnp.int32, (c, GDN_CAT), 0)
    j_cat = _mod_pow2(lax.broadcasted_iota(jnp.int32, (c, GDN_CAT), 1), c)
    tri_cat = r_cat >= j_cat
    strict_cat = r_cat > j_cat
    lane_p = lax.broadcasted_iota(jnp.int32, (1, GDN_PAIR), 1)
    first = lane_p < GDN_HEAD_DIM
    group_w = GDN_GROUP * GDN_HEAD_DIM
    group_c = GDN_GROUP * c
    contract_last = (((1,), (1,)), ((), ()))
    k_mask = _block_diag_mask(GDN_GROUP, c, GDN_HEAD_DIM)

    n_chunks = rows // c
    n_groups = GDN_HEADS // GDN_GROUP
    l_cats = []
    for ch in range(n_chunks):
        rs = slice(ch * c, (ch + 1) * c)
        gc64_c = gc64[rs]
        gc_row = jnp.sum(jnp.where(r_cat == j_cat, gc64_c, 0.0), axis=0, keepdims=True)
        decay = jnp.where(tri_cat, jnp.exp(jnp.where(tri_cat, gc64_c - gc_row, 0.0)), 0.0)
        g_last = gc128[ch * c + c - 1:ch * c + c, :]
        gl_ref[ch] = jnp.exp(g_last)
        kd_ref[rs, :] = (kn[rs] * jnp.exp(g_last - gc128[rs])).astype(BF16)

        for gidx in range(n_groups):
            ksl = slice(gidx * group_w, (gidx + 1) * group_w)
            csl = slice(gidx * group_c, (gidx + 1) * group_c)
            bd_k = _block_diag_rows(kn16[rs, ksl], k_mask)
            both = lax.dot_general(jnp.concatenate([kb16[rs, ksl], qs16[rs, ksl]], axis=0), bd_k,
                                   contract_last, preferred_element_type=F32)
            dec = decay[:, csl]
            l_cats.append(jnp.where(strict_cat[:, csl], both[:c] * dec, 0.0))
            qk_ref[rs, csl] = (both[c:] * dec).astype(BF16)

    invs = _unit_lower_inverses(l_cats)

    for ch in range(n_chunks):
        rs = slice(ch * c, (ch + 1) * c)
        inv_cat = jnp.concatenate(invs[ch * n_groups:(ch + 1) * n_groups], axis=1)
        for p in range(GDN_PAIRS):
            wsl = slice(p * GDN_PAIR, (p + 1) * GDN_PAIR)
            inv_p = inv_cat[:, p * 2 * c:(p + 1) * 2 * c]
            vb_p, kbg_p = vb[rs, wsl], kbg[rs, wsl]
            rhs = jnp.concatenate(
                [jnp.concatenate([jnp.where(first, vb_p, 0.0), jnp.where(first, kbg_p, 0.0)], axis=1),
                 jnp.concatenate([jnp.where(first, 0.0, vb_p), jnp.where(first, 0.0, kbg_p)], axis=1)],
                axis=0)
            i_hi, i_lo = _split2(inv_p)
            r_hi, r_lo = _split2(rhs)
            top = _dot(jnp.concatenate([i_hi, i_lo], axis=0), r_hi)
            sol = top[:c] + top[c:] + _dot(i_hi, r_lo)
            u_ref[rs, wsl] = sol[:, :GDN_PAIR]
            w_ref[rs, wsl] = sol[:, GDN_PAIR:].astype(BF16)


def _gdn_scan_kernel(u_ref, w_ref, qg_ref, kd_ref, qk_ref, gl_ref, z_ref, nw_ref, o_ref, state_ref):
    c = GDN_CHUNK

    @pl.when(pl.program_id(0) == 0)
    def _():
        state_ref[...] = jnp.zeros_like(state_ref)

    lane_p = lax.broadcasted_iota(jnp.int32, (1, GDN_PAIR), 1)
    first = lane_p < GDN_HEAD_DIM
    bm_r = _div_pow2(lax.broadcasted_iota(jnp.int32, (GDN_PAIR, GDN_PAIR), 0), GDN_HEAD_DIM)
    bm_c = _div_pow2(lax.broadcasted_iota(jnp.int32, (GDN_PAIR, GDN_PAIR), 1), GDN_HEAD_DIM)
    same_head = bm_r == bm_c
    nw = nw_ref[...]
    zero = jnp.zeros((), BF16)
    cells = [(b, p) for b in range(u_ref.shape[0]) for p in range(GDN_PAIRS)]
    wsl = lambda p: slice(p * GDN_PAIR, (p + 1) * GDN_PAIR)
    states = [state_ref[b, p] for b, p in cells]
    boths = [_dot(jnp.concatenate([w_ref[b, :, wsl(p)], qg_ref[b, :, wsl(p)]], axis=0),
                  st.astype(BF16)) for (b, p), st in zip(cells, states)]
    v16s = [(u_ref[b, :, wsl(p)] - both[:c]).astype(BF16) for (b, p), both in zip(cells, boths)]
    outs, upds = [], []
    for (b, p), both, v16 in zip(cells, boths, v16s):
        v_bd = jnp.concatenate([jnp.where(first, v16, zero), jnp.where(first, zero, v16)], axis=0)
        outs.append(both[c:] + _dot(qk_ref[b, :, p * 2 * c:(p + 1) * 2 * c], v_bd))
        upds.append(lax.dot_general(kd_ref[b, :, wsl(p)], v16, (((0,), (0,)), ((), ())),
                                    preferred_element_type=F32))
    for (b, p), st, o, upd in zip(cells, states, outs, upds):
        state_ref[b, p] = st * gl_ref[b, 0][:, wsl(p)] + jnp.where(same_head, upd, 0.0)
        halves = []
        for h in range(2):
            oh = o[:, h * GDN_HEAD_DIM:(h + 1) * GDN_HEAD_DIM]
            ms = jnp.mean(oh * oh, axis=-1, keepdims=True)
            halves.append(oh * lax.rsqrt(ms + EPS) * nw)
        o_ref[b, :, wsl(p)] = (jnp.concatenate(halves, axis=1)
                               * _silu(z_ref[b, :, wsl(p)])).astype(o_ref.dtype)


def _gdn(gdn, small, conv_w3, alog_row, dtb_row, nw_row, batch, seq):
    t = gdn.shape[0]
    c = GDN_CHUNK
    w = GDN_WIDTH
    rows = GDN_PREP_CHUNKS * c
    e_g128, e_b128, e_g64 = _gdn_expanders()
    n_conv = conv_w3.shape[1]
    tok = lambda col: (lambda i: (i, col))
    halo = lambda col: (lambda i: (jnp.maximum(i * (rows // CONV_HALO) - 1, 0), col))
    const2 = lambda i: (0, 0)
    u, wk, qg, kd, qk, gl = pl.pallas_call(
        functools.partial(_gdn_prep_kernel, blocks_per_seq=seq // rows),
        grid=(t // rows,),
        in_specs=[pl.BlockSpec((rows, w), tok(0)),
                  pl.BlockSpec((rows, w), tok(1)),
                  pl.BlockSpec((rows, w), tok(2)),
                  pl.BlockSpec((CONV_HALO, w), halo(0)),
                  pl.BlockSpec((CONV_HALO, w), halo(1)),
                  pl.BlockSpec((CONV_HALO, w), halo(2)),
                  pl.BlockSpec((rows, LANES), tok(0)),
                  pl.BlockSpec((3, n_conv, w), lambda i: (0, 0, 0)),
                  pl.BlockSpec((1, LANES), const2),
                  pl.BlockSpec((1, LANES), const2),
                  pl.BlockSpec((LANES, w), const2),
                  pl.BlockSpec((LANES, w), const2),
                  pl.BlockSpec((LANES, GDN_CAT), const2)],
        out_specs=[pl.BlockSpec((rows, w), tok(0)),
                   pl.BlockSpec((rows, w), tok(0)),
                   pl.BlockSpec((rows, w), tok(0)),
                   pl.BlockSpec((rows, w), tok(0)),
                   pl.BlockSpec((rows, GDN_CAT), tok(0)),
                   pl.BlockSpec((GDN_PREP_CHUNKS, 1, w), lambda i: (i, 0, 0))],
        out_shape=[jax.ShapeDtypeStruct((t, w), F32),
                   jax.ShapeDtypeStruct((t, w), BF16),
                   jax.ShapeDtypeStruct((t, w), BF16),
                   jax.ShapeDtypeStruct((t, w), BF16),
                   jax.ShapeDtypeStruct((t, GDN_CAT), BF16),
                   jax.ShapeDtypeStruct((t // c, 1, w), F32)],
        compiler_params=_cparams("parallel"),
        name="gdn_prep",
    )(gdn, gdn, gdn, gdn, gdn, gdn, small, conv_w3, alog_row, dtb_row,
      jnp.asarray(e_g128, BF16), jnp.asarray(e_b128, BF16), jnp.asarray(e_g64, BF16))

    seq3 = lambda a: a.reshape(batch, seq, a.shape[-1])
    blk = lambda width, col=0: pl.BlockSpec((batch, c, width), lambda n: (0, n, col))
    o = pl.pallas_call(
        _gdn_scan_kernel,
        grid=(seq // c,),
        in_specs=[blk(w), blk(w), blk(w), blk(w), blk(GDN_CAT),
                  pl.BlockSpec((batch, 1, 1, w), lambda n: (0, n, 0, 0)),
                  blk(w, 3),
                  pl.BlockSpec((1, GDN_HEAD_DIM), lambda n: (0, 0))],
        out_specs=blk(w),
        out_shape=jax.ShapeDtypeStruct((batch, seq, w), BF16),
        scratch_shapes=[pltpu.VMEM((batch, GDN_PAIRS, GDN_PAIR, GDN_PAIR), F32)],
        compiler_params=_cparams("arbitrary"),
        name="gdn_scan",
    )(seq3(u), seq3(wk), seq3(qg), seq3(kd), seq3(qk), gl.reshape(batch, seq // c, 1, w),
      seq3(gdn), nw_row)
    return o.reshape(t, w)


def _merge_out_kernel(x_ref, oa_ref, ob_ref, ga_ref, gb_ref, wa_ref, wb_ref, wo_ref, o_ref):
    ya = _dot(oa_ref[...], wa_ref[...])
    yb = _dot(ob_ref[...], wb_ref[...])
    y = (_sigmoid(ga_ref[...].astype(F32)) * ya + _sigmoid(gb_ref[...].astype(F32)) * yb)
    o_ref[...] = x_ref[...] + _dot(y.astype(BF16), wo_ref[...])


def _merge_out(x, o_a, o_b, mg, w_a, w_b, w_o, tm):
    t, d = x.shape
    tokd = pl.BlockSpec((tm, d), lambda i: (i, 0))
    wspec = pl.BlockSpec((d, d), lambda i: (0, 0))
    return pl.pallas_call(
        _merge_out_kernel,
        grid=(t // tm,),
        in_specs=[tokd, tokd, tokd,
                  pl.BlockSpec((tm, d), lambda i: (i, 0)),
                  pl.BlockSpec((tm, d), lambda i: (i, 1)),
                  wspec, wspec, wspec],
        out_specs=tokd,
        out_shape=jax.ShapeDtypeStruct((t, d), F32),
        compiler_params=_cparams("parallel"),
        name="merge_out",
    )(x, o_a, o_b, mg, mg, w_a, w_b, w_o)


def _ffn_kernel(x_ref, nw_ref, wg_ref, wu_ref, wd_ref, fw_ref, o_ref, *, final_norm):
    x = x_ref[...]
    h = _rms_norm(x, nw_ref[...]).astype(BF16)
    a = (_silu(_dot(h, wg_ref[...])) * _dot(h, wu_ref[...])).astype(BF16)
    y = x + _dot(a, wd_ref[...])
    if final_norm:
        y = _rms_norm(y, fw_ref[...])
    o_ref[...] = y


def _ffn(x, nw, w_g, w_u, w_d, fw, final_norm, tm):
    t, d = x.shape
    f = w_g.shape[1]
    tokd = pl.BlockSpec((tm, d), lambda i: (i, 0))
    rowd = pl.BlockSpec((1, d), lambda i: (0, 0))
    return pl.pallas_call(
        functools.partial(_ffn_kernel, final_norm=final_norm),
        grid=(t // tm,),
        in_specs=[tokd, rowd,
                  pl.BlockSpec((d, f), lambda i: (0, 0), pipeline_mode=pl.Buffered(1)),
                  pl.BlockSpec((d, f), lambda i: (0, 0), pipeline_mode=pl.Buffered(1)),
                  pl.BlockSpec((f, d), lambda i: (0, 0), pipeline_mode=pl.Buffered(1)),
                  rowd],
        out_specs=tokd,
        out_shape=jax.ShapeDtypeStruct((t, d), F32),
        compiler_params=_cparams("parallel"),
        name="ffn",
    )(x, nw, w_g, w_u, w_d, fw)


def _pad_row(v, offset):
    return jnp.zeros((1, LANES), F32).at[0, offset:offset + v.shape[0]].set(v.astype(F32))


def _layer(x, batch, seq, norm_mix_w, w_in, conv_w, a_log, dt_bias, gdn_norm_w, fox_f_bias,
           w_branch_a, w_branch_b, w_out, norm_ffn_w, w_gate, w_up, w_down, final_w, final_norm):
    d = x.shape[1]
    gw, fw = GDN_WIDTH, FOX_HEADS * FOX_HEAD_DIM
    sizes = (gw, gw, gw, gw, GDN_HEADS, GDN_HEADS, fw, fw, fw, FOX_HEADS, fw, d, d)
    offs = np.concatenate([[0], np.cumsum(sizes)])
    col = lambda i: w_in[:, offs[i]:offs[i + 1]]
    w_gdn = jnp.concatenate([col(0), col(1), col(2), col(3)], axis=1).astype(BF16)
    w_fox = jnp.concatenate([col(6), col(7), col(10)], axis=1).astype(BF16)
    w_fv_t = col(8).T.astype(BF16)
    w_mg = jnp.concatenate([col(11), col(12)], axis=1).astype(BF16)
    n_small = 2 * GDN_HEADS + FOX_HEADS
    w_small = jnp.concatenate([col(4), col(5), col(9), jnp.zeros((d, LANES - n_small), F32)],
                              axis=1).astype(BF16)
    nw = norm_mix_w.reshape(1, d)

    (gdn,) = _norm_proj(x, nw, [(w_gdn, F32)], [], tm=512)
    fox, vt = _norm_proj(x, nw, [(w_fox, BF16)], [(w_fv_t, BF16)], tm=512)
    mg, small = _norm_proj(x, nw, [(w_mg, BF16), (w_small, F32)], [], tm=512)

    eq, ek = _fox_bias(small, _pad_row(fox_f_bias, SMALL_FF), batch, seq, tm=min(seq, 512))
    o_b = _fox_attention(fox, vt, eq, ek, batch, seq, tq=min(seq, 512))

    conv_w3 = conv_w.reshape(conv_w.shape[0], 3, gw).transpose(1, 0, 2)
    o_a = _gdn(gdn, small, conv_w3, _pad_row(a_log, SMALL_GA), _pad_row(dt_bias, SMALL_GA),
               gdn_norm_w.reshape(1, GDN_HEAD_DIM), batch, seq)

    x1 = _merge_out(x, o_a, o_b, mg, w_branch_a.astype(BF16), w_branch_b.astype(BF16),
                    w_out.astype(BF16), tm=512)
    return _ffn(x1, norm_ffn_w.reshape(1, d), w_gate.astype(BF16), w_up.astype(BF16),
                w_down.astype(BF16), final_w.reshape(1, d), final_norm, tm=512)


def kernel(x, norm_mix_w, w_in, conv_w, a_log, dt_bias, gdn_norm_w, fox_f_bias, w_branch_a,
           w_branch_b, w_out, norm_ffn_w, w_gate, w_up, w_down, norm_final_w):
    batch, seq, d = x.shape
    depth = w_in.shape[0]
    h = x.reshape(batch * seq, d)
    for l in range(depth):
        h = _layer(h, batch, seq, norm_mix_w[l], w_in[l], conv_w[l], a_log[l], dt_bias[l],
                   gdn_norm_w[l], fox_f_bias[l], w_branch_a[l], w_branch_b[l], w_out[l],
                   norm_ffn_w[l], w_gate[l], w_up[l], w_down[l], norm_final_w,
                   final_norm=(l == depth - 1))
    return h.reshape(batch, seq, d)
```

```python
import functools

import jax
import jax.numpy as jnp
import numpy as np
from jax import lax
from jax.experimental import pallas as pl
from jax.experimental.pallas import tpu as pltpu

F32 = jnp.float32
BF16 = jnp.bfloat16

EPS = 1e-6
GDN_HEADS = 8
GDN_HEAD_DIM = 128
GDN_CHUNK = 64
FOX_HEADS = 16
FOX_HEAD_DIM = 64
LANES = 128
VMEM_LIMIT_BYTES = 56 * 1024 * 1024
TOKEN_TILE = 512
NEG_BIG = -1e30


def _cparams(*semantics):
    return pltpu.CompilerParams(dimension_semantics=semantics,
                                vmem_limit_bytes=VMEM_LIMIT_BYTES)


def _split2(x):
    hi = x.astype(BF16)
    lo = (x - hi.astype(F32)).astype(BF16)
    return hi, lo


def _split3(x):
    hi = x.astype(BF16)
    r = x - hi.astype(F32)
    mid = r.astype(BF16)
    lo = (r - mid.astype(F32)).astype(BF16)
    return hi, mid, lo


def _dot(a, b):
    return jnp.dot(a, b, preferred_element_type=F32)


def _dot_exact_rhs(x, m_bf16):
    n = x.shape[0]
    y = _dot(jnp.concatenate(_split3(x), axis=0), m_bf16)
    return y[:n] + y[n:2 * n] + y[2 * n:]


def _dot_exact_lhs(m_bf16, x):
    n = x.shape[1]
    y = _dot(m_bf16, jnp.concatenate(_split3(x), axis=1))
    return y[:, :n] + y[:, n:2 * n] + y[:, 2 * n:]


def _div_pow2(x, n):
    assert n & (n - 1) == 0
    return jnp.right_shift(x, n.bit_length() - 1)


def _mod_pow2(x, n):
    assert n & (n - 1) == 0
    return jnp.bitwise_and(x, n - 1)


def _softplus(y):
    return jnp.maximum(y, 0.0) + jnp.log(1.0 + jnp.exp(-jnp.abs(y)))


def _sigmoid(y):
    return 1.0 / (1.0 + jnp.exp(-y))


def _silu(y):
    return y * _sigmoid(y)


def _rms_norm(x, w):
    ms = jnp.mean(x * x, axis=-1, keepdims=True)
    return x * lax.rsqrt(ms + EPS) * w


PROJ_CHUNK = 1024


def _norm_proj_kernel(x_ref, nw_ref, *refs, n_plain):
    n_w = len(refs) // 2
    h = _rms_norm(x_ref[...], nw_ref[...]).astype(BF16)
    for idx in range(n_w):
        w_ref, o_ref = refs[idx], refs[n_w + idx]
        if idx < n_plain:
            n = o_ref.shape[1]
            for c in range(0, n, PROJ_CHUNK):
                sl = slice(c, min(c + PROJ_CHUNK, n))
                o_ref[:, sl] = _dot(h, w_ref[:, sl]).astype(o_ref.dtype)
        else:
            n = o_ref.shape[0]
            for c in range(0, n, PROJ_CHUNK):
                sl = slice(c, min(c + PROJ_CHUNK, n))
                o_ref[sl, :] = lax.dot_general(w_ref[sl, :], h, (((1,), (1,)), ((), ())),
                                               preferred_element_type=F32).astype(o_ref.dtype)


def _norm_proj(x, nw, plain, transposed, tm):
    t, d = x.shape
    in_specs = [pl.BlockSpec((tm, d), lambda i: (i, 0)), pl.BlockSpec((1, d), lambda i: (0, 0))]
    out_specs, out_shape = [], []
    for w, dtype in plain:
        n = w.shape[1]
        in_specs.append(pl.BlockSpec((d, n), lambda i: (0, 0)))
        out_specs.append(pl.BlockSpec((tm, n), lambda i: (i, 0)))
        out_shape.append(jax.ShapeDtypeStruct((t, n), dtype))
    for wt, dtype in transposed:
        n = wt.shape[0]
        in_specs.append(pl.BlockSpec((n, d), lambda i: (0, 0)))
        out_specs.append(pl.BlockSpec((n, tm), lambda i: (0, i)))
        out_shape.append(jax.ShapeDtypeStruct((n, t), dtype))
    return pl.pallas_call(
        functools.partial(_norm_proj_kernel, n_plain=len(plain)),
        grid=(t // tm,),
        in_specs=in_specs,
        out_specs=out_specs,
        out_shape=out_shape,
        compiler_params=_cparams("parallel"),
        name="norm_proj",
    )(x, nw, *[w for w, _ in plain], *[w for w, _ in transposed])


SMALL_GA = 0
SMALL_GB = GDN_HEADS
SMALL_FF = 2 * GDN_HEADS
BIAS_TERMS = 3
BIAS_HEAD_STRIDE = 8


def _bias_placements():
    n_out = (FOX_HEADS // 2) * LANES
    pq = np.zeros((LANES, n_out), np.float32)
    pk = np.zeros((LANES, n_out), np.float32)
    ones_q = np.zeros((1, n_out), np.float32)
    ones_k = np.zeros((1, n_out), np.float32)
    for h in range(FOX_HEADS):
        base = (h // 2) * LANES + (h % 2) * BIAS_HEAD_STRIDE
        for t in range(BIAS_TERMS):
            pq[SMALL_FF + t * FOX_HEADS + h, base + t] = 1.0
            pk[SMALL_FF + t * FOX_HEADS + h, base + BIAS_TERMS + t] = -1.0
            ones_q[0, base + BIAS_TERMS + t] = 1.0
            ones_k[0, base + t] = 1.0
    return pq, pk, ones_q, ones_k


def _fox_bias_kernel(small_ref, fb_ref, pq_ref, pk_ref, oq_ref, ok_ref, eq_ref, ek_ref,
                     carry_ref):
    tm = small_ref.shape[0]

    @pl.when(pl.program_id(1) == 0)
    def _():
        carry_ref[...] = jnp.zeros_like(carry_ref)

    z = small_ref[...] + fb_ref[...]
    log_f = -_softplus(-z)
    row = lax.broadcasted_iota(jnp.int32, (tm, tm), 0)
    col = lax.broadcasted_iota(jnp.int32, (tm, tm), 1)
    tril = jnp.where(row >= col, 1.0, 0.0).astype(BF16)
    cum = _dot_exact_lhs(tril, log_f) + carry_ref[0:1, :]
    carry_ref[...] = jnp.broadcast_to(cum[tm - 1:tm, :], carry_ref.shape)
    lane = lax.broadcasted_iota(jnp.int32, (1, LANES), 1)
    is_ff = jnp.logical_and(lane >= SMALL_FF, lane < SMALL_FF + FOX_HEADS)
    packed = jnp.zeros_like(cum)
    for t, term in enumerate(_split3(cum)):
        part = jnp.where(is_ff, term.astype(F32), 0.0)
        packed = packed + (pltpu.roll(part, t * FOX_HEADS, axis=1) if t else part)
    terms = packed.astype(BF16)
    eq_ref[...] = (oq_ref[...] + _dot(terms, pq_ref[...])).astype(BF16)
    ek_ref[...] = (ok_ref[...] + _dot(terms, pk_ref[...])).astype(BF16)


def _fox_bias(small, fb_row, batch, seq, tm):
    t = small.shape[0]
    pq, pk, ones_q, ones_k = _bias_placements()
    n_out = pq.shape[1]
    nt = seq // tm
    const2 = lambda b, i: (0, 0)
    return pl.pallas_call(
        _fox_bias_kernel,
        grid=(batch, nt),
        in_specs=[pl.BlockSpec((tm, LANES), lambda b, i: (b * nt + i, 0)),
                  pl.BlockSpec((1, LANES), const2),
                  pl.BlockSpec((LANES, n_out), const2),
                  pl.BlockSpec((LANES, n_out), const2),
                  pl.BlockSpec((1, n_out), const2),
                  pl.BlockSpec((1, n_out), const2)],
        out_specs=[pl.BlockSpec((tm, n_out), lambda b, i: (b * nt + i, 0)),
                   pl.BlockSpec((tm, n_out), lambda b, i: (b * nt + i, 0))],
        out_shape=[jax.ShapeDtypeStruct((t, n_out), BF16),
                   jax.ShapeDtypeStruct((t, n_out), BF16)],
        scratch_shapes=[pltpu.VMEM((8, LANES), F32)],
        compiler_params=_cparams("parallel", "arbitrary"),
        name="fox_bias",
    )(small, fb_row, jnp.asarray(pq, BF16), jnp.asarray(pk, BF16),
      jnp.asarray(ones_q), jnp.asarray(ones_k))


FOX_PAIRS_PER_STEP = 4
FOX_SCORES_AHEAD = 3


def _fox_attention_kernel(q_ref, eq_ref, k_ref, ek_ref, vt_ref, fo_ref, o_ref, *, tq):
    i = pl.program_id(2)
    n_pairs = q_ref.shape[1] // LANES
    n_heads = 2 * n_pairs
    lane = lax.broadcasted_iota(jnp.int32, (1, LANES), 1)
    head_a = lane < FOX_HEAD_DIM
    bias_a = lane < BIAS_HEAD_STRIDE
    row_a = lax.broadcasted_iota(jnp.int32, (LANES, 1), 0) < FOX_HEAD_DIM
    zero = jnp.zeros((), BF16)
    one = jnp.ones((), BF16)
    qm = []
    for p in range(n_pairs):
        psl = slice(p * LANES, (p + 1) * LANES)
        q = q_ref[:, psl] * jnp.asarray(FOX_HEAD_DIM ** -0.5, BF16)
        eq = eq_ref[:, psl]
        qm.append(jnp.concatenate([jnp.where(head_a, q, zero), jnp.where(bias_a, eq, zero)], axis=1))
        qm.append(jnp.concatenate([jnp.where(head_a, zero, q), jnp.where(bias_a, zero, eq)], axis=1))

    def tile(j, carry, masked):
        start = pl.multiple_of(j * tq, tq)

        def score(idx):
            psl = slice((idx // 2) * LANES, (idx // 2 + 1) * LANES)
            kk = jnp.concatenate([k_ref[pl.ds(start, tq), psl], ek_ref[pl.ds(start, tq), psl]], axis=1)
            return lax.dot_general(kk, qm[idx], (((1,), (1,)), ((), ())),
                                   preferred_element_type=F32)

        def value(idx):
            p, h = divmod(idx, 2)
            vt = vt_ref[p * LANES:(p + 1) * LANES, pl.ds(start, tq)]
            return jnp.where(row_a, vt, one) if h == 0 else jnp.where(row_a, one, vt)

        out = []
        ahead = [score(idx) for idx in range(min(FOX_SCORES_AHEAD, n_heads))]
        for idx in range(n_heads):
            s = ahead.pop(0)
            if idx + FOX_SCORES_AHEAD < n_heads:
                ahead.append(score(idx + FOX_SCORES_AHEAD))
            m, acc = carry[idx]
            if masked:
                key = lax.broadcasted_iota(jnp.int32, (tq, tq), 0)
                qry = lax.broadcasted_iota(jnp.int32, (tq, tq), 1)
                s = jnp.where(key <= qry, s, NEG_BIG)
            m_new = jnp.maximum(m, jnp.max(s, axis=0, keepdims=True))
            alpha = jnp.exp(m - m_new)
            prob = jnp.exp(s - m_new).astype(BF16)
            acc = alpha * acc + _dot(value(idx), prob)
            out.append((m_new, acc))
        return tuple(out)

    init = tuple((jnp.full((1, tq), NEG_BIG, F32), jnp.zeros((LANES, tq), F32))
                 for _ in range(n_heads))
    carry = lax.fori_loop(0, i, lambda j, c: tile(j, c, False), init)
    final = tile(i, carry, True)
    for p in range(n_pairs):
        psl = slice(p * LANES, (p + 1) * LANES)
        acc_a, acc_b = final[2 * p][1], final[2 * p + 1][1]
        num = jnp.where(row_a, acc_a, acc_b)
        den = jnp.where(row_a, acc_a[FOX_HEAD_DIM:FOX_HEAD_DIM + 1, :], acc_b[0:1, :])
        gate = _sigmoid(fo_ref[:, psl].astype(F32))
        o_ref[:, psl] = ((num / den).T * gate).astype(o_ref.dtype)


def _fox_attention(fox, vt, eq, ek, batch, seq, tq):
    t = fox.shape[0]
    pairs = FOX_HEADS // 2
    steps = pairs // FOX_PAIRS_PER_STEP
    width = FOX_PAIRS_PER_STEP * LANES
    nq = seq // tq
    return pl.pallas_call(
        functools.partial(_fox_attention_kernel, tq=tq),
        grid=(batch, steps, nq),
        in_specs=[pl.BlockSpec((tq, width), lambda b, p, i: (b * nq + i, p)),
                  pl.BlockSpec((tq, width), lambda b, p, i: (b * nq + i, p)),
                  pl.BlockSpec((seq, width), lambda b, p, i: (b, steps + p)),
                  pl.BlockSpec((seq, width), lambda b, p, i: (b, p)),
                  pl.BlockSpec((width, seq), lambda b, p, i: (p, b)),
                  pl.BlockSpec((tq, width), lambda b, p, i: (b * nq + i, 2 * steps + p))],
        out_specs=pl.BlockSpec((tq, width), lambda b, p, i: (b * nq + i, p)),
        out_shape=jax.ShapeDtypeStruct((t, pairs * LANES), BF16),
        compiler_params=_cparams("parallel", "parallel", "arbitrary"),
        name="fox_attention",
    )(fox, eq, fox, ek, vt, fox)


GDN_WIDTH = GDN_HEADS * GDN_HEAD_DIM
GDN_GROUP = 4
GDN_CAT = GDN_HEADS * GDN_CHUNK
GDN_PAIRS = GDN_HEADS // 2
GDN_PAIR = 2 * GDN_HEAD_DIM
GDN_PREP_CHUNKS = 4
CONV_HALO = 8


def _gdn_expanders():
    e_g128 = np.zeros((LANES, GDN_WIDTH), np.float32)
    e_b128 = np.zeros((LANES, GDN_WIDTH), np.float32)
    e_g64 = np.zeros((LANES, GDN_CAT), np.float32)
    for h in range(GDN_HEADS):
        e_g128[SMALL_GA + h, h * GDN_HEAD_DIM:(h + 1) * GDN_HEAD_DIM] = 1.0
        e_b128[SMALL_GB + h, h * GDN_HEAD_DIM:(h + 1) * GDN_HEAD_DIM] = 1.0
        e_g64[SMALL_GA + h, h * GDN_CHUNK:(h + 1) * GDN_CHUNK] = 1.0
    return e_g128, e_b128, e_g64


def _block_diag(x, n_blocks):
    r, total = x.shape
    w = total // n_blocks
    tile_w = max(w, LANES)
    per_tile = tile_w // w
    zeros = jnp.zeros((r, tile_w), x.dtype)
    lane_block = _div_pow2(lax.broadcasted_iota(jnp.int32, (1, tile_w), 1), w)
    rows = []
    for h in range(n_blocks):
        t = h // per_tile
        tile = x[:, t * tile_w:(t + 1) * tile_w]
        if per_tile > 1:
            tile = tile * jnp.where(lane_block == h % per_tile, 1.0, 0.0).astype(x.dtype)
        rows.append(jnp.concatenate([tile if i == t else zeros for i in range(total // tile_w)], axis=1))
    return jnp.concatenate(rows, axis=0)


def _rows(x, i, n):
    return x[i * n:(i + 1) * n]


def _headwise_products(lhs, b):
    c = b.shape[0]
    n = len(lhs)
    b_hi, b_lo = _split2(b)
    bd_hi = _block_diag(b_hi, GDN_GROUP)
    bd_lo = _block_diag(b_lo, GDN_GROUP)
    parts = [_split2(x) for x in lhs]
    his = [p[0] for p in parts]
    los = [p[1] for p in parts]
    top = _dot(jnp.concatenate(his + los, axis=0), bd_hi)
    bot = _dot(jnp.concatenate(his, axis=0), bd_lo)
    return [_rows(top, i, c) + _rows(top, n + i, c) + _rows(bot, i, c) for i in range(n)]


def _unit_lower_inverses(l_cats):
    c, n = l_cats[0].shape
    r = lax.broadcasted_iota(jnp.int32, (c, n), 0)
    j = _mod_pow2(lax.broadcasted_iota(jnp.int32, (c, n), 1), c)
    eye = jnp.where(r == j, 1.0, 0.0)
    ss = [eye - l for l in l_cats]
    ps = [_headwise_products([l], l)[0] for l in l_cats]
    k = 2
    while 2 * k < c:
        stage = [_headwise_products([p, s], p) for p, s in zip(ps, ss)]
        ps = [st[0] for st in stage]
        ss = [s + st[1] for s, st in zip(ss, stage)]
        k *= 2
    return [s + _headwise_products([s], p)[0] for p, s in zip(ps, ss)]


def _gdn_prep_kernel(q_ref, k_ref, v_ref, hq_ref, hk_ref, hv_ref, small_ref, cw_ref, alog_ref,
                     dtb_ref, eg128_ref, eb128_ref, eg64_ref,
                     u_ref, w_ref, qg_ref, kd_ref, qk_ref, gl_ref, *, blocks_per_seq):
    c = GDN_CHUNK
    n_conv = cw_ref.shape[1]
    seq_start = lax.rem(pl.program_id(0), blocks_per_seq) == 0

    halo_row = lax.broadcasted_iota(jnp.int32, (CONV_HALO, 1), 0)
    conv = []
    for s, (ref, halo_ref) in enumerate(((q_ref, hq_ref), (k_ref, hk_ref), (v_ref, hv_ref))):
        x = ref[...]
        halo = jnp.where(seq_start, 0.0, halo_ref[...])
        acc = x * cw_ref[s, n_conv - 1:n_conv, :]
        for shift in range(1, n_conv):
            moved = pltpu.roll(x, shift, axis=0)
            top = jnp.where(halo_row < shift, pltpu.roll(halo, shift, axis=0), moved[:CONV_HALO])
            moved = jnp.concatenate([top, moved[CONV_HALO:]], axis=0)
            acc = acc + moved * cw_ref[s, n_conv - 1 - shift:n_conv - shift, :]
        conv.append(_silu(acc))
    cq, ck, cv = conv
    rows = cq.shape[0]

    def l2n(x):
        parts = []
        for h in range(GDN_HEADS):
            xh = x[:, h * GDN_HEAD_DIM:(h + 1) * GDN_HEAD_DIM]
            ss = jnp.sum(xh * xh, axis=-1, keepdims=True)
            parts.append(xh * lax.rsqrt(ss + EPS))
        return jnp.concatenate(parts, axis=1)

    qn = l2n(cq)
    kn = l2n(ck)

    small = small_ref[...]
    g_tok = -jnp.exp(alog_ref[...]) * _softplus(small + dtb_ref[...])
    beta_tok = _sigmoid(small)
    row = lax.broadcasted_iota(jnp.int32, (rows, rows), 0)
    col = lax.broadcasted_iota(jnp.int32, (rows, rows), 1)
    same_chunk = _div_pow2(row, c) == _div_pow2(col, c)
    tril = jnp.where(row >= col, jnp.where(same_chunk, 1.0, 0.0), 0.0).astype(BF16)
    gc_tok = _dot_exact_lhs(tril, g_tok)
    gc128 = _dot_exact_rhs(gc_tok, eg128_ref[...])
    gc64 = _dot_exact_rhs(gc_tok, eg64_ref[...])
    beta128 = _dot_exact_rhs(beta_tok, eb128_ref[...])

    exp_gc = jnp.exp(gc128)
    kb = kn * beta128
    vb = cv * beta128
    kbg = kb * exp_gc
    qs = qn * (GDN_HEAD_DIM ** -0.5)
    qg_ref[...] = (qs * exp_gc).astype(BF16)
    kb16 = kb.astype(BF16)
    qs16 = qs.astype(BF16)
    kn16 = kn.astype(BF16)

    r_cat = lax.broadcasted_iota(jnp.int32, (c, GDN_CAT), 0)
    j_cat = _mod_pow2(lax.broadcasted_iota(jnp.int32, (c, GDN_CAT), 1), c)
    tri_cat = r_cat >= j_cat
    strict_cat = r_cat > j_cat
    group_w = GDN_GROUP * GDN_HEAD_DIM
    group_c = GDN_GROUP * c
    contract_last = (((1,), (1,)), ((), ()))

    n_chunks = rows // c
    n_groups = GDN_HEADS // GDN_GROUP
    l_cats = []
    for ch in range(n_chunks):
        rs = slice(ch * c, (ch + 1) * c)
        gc64_c = gc64[rs]
        gc_row = jnp.sum(jnp.where(r_cat == j_cat, gc64_c, 0.0), axis=0, keepdims=True)
        decay = jnp.where(tri_cat, jnp.exp(jnp.where(tri_cat, gc64_c - gc_row, 0.0)), 0.0)
        g_last = gc128[ch * c + c - 1:ch * c + c, :]
        gl_ref[ch] = jnp.exp(g_last)
        kd_ref[rs, :] = (kn[rs] * jnp.exp(g_last - gc128[rs])).astype(BF16)

        for gidx in range(n_groups):
            ksl = slice(gidx * group_w, (gidx + 1) * group_w)
            csl = slice(gidx * group_c, (gidx + 1) * group_c)
            bd_k = _block_diag(kn16[rs, ksl], GDN_GROUP)
            both = lax.dot_general(jnp.concatenate([kb16[rs, ksl], qs16[rs, ksl]], axis=0), bd_k,
                                   contract_last, preferred_element_type=F32)
            dec = decay[:, csl]
            l_cats.append(jnp.where(strict_cat[:, csl], both[:c] * dec, 0.0))
            qk_ref[rs, csl] = (both[c:] * dec).astype(BF16)

    invs = _unit_lower_inverses(l_cats)

    for ch in range(n_chunks):
        rs = slice(ch * c, (ch + 1) * c)
        inv_cat = jnp.concatenate(invs[ch * n_groups:(ch + 1) * n_groups], axis=1)
        for p in range(GDN_PAIRS):
            wsl = slice(p * GDN_PAIR, (p + 1) * GDN_PAIR)
            inv_p = inv_cat[:, p * 2 * c:(p + 1) * 2 * c]
            vb_p, kbg_p = vb[rs, wsl], kbg[rs, wsl]
            rhs = jnp.concatenate([_block_diag(vb_p, 2), _block_diag(kbg_p, 2)], axis=1)
            i_hi, i_lo = _split2(inv_p)
            r_hi, r_lo = _split2(rhs)
            top = _dot(jnp.concatenate([i_hi, i_lo], axis=0), r_hi)
            sol = top[:c] + top[c:] + _dot(i_hi, r_lo)
            u_ref[rs, wsl] = sol[:, :GDN_PAIR]
            w_ref[rs, wsl] = sol[:, GDN_PAIR:].astype(BF16)


def _gdn_scan_kernel(u_ref, w_ref, qg_ref, kd_ref, qk_ref, gl_ref, z_ref, nw_ref, o_ref, state_ref):
    c = GDN_CHUNK

    @pl.when(pl.program_id(0) == 0)
    def _():
        state_ref[...] = jnp.zeros_like(state_ref)

    nw = nw_ref[...]
    zblock = jnp.zeros((GDN_HEAD_DIM, GDN_HEAD_DIM), BF16)
    hsl = lambda h: slice(h * GDN_HEAD_DIM, (h + 1) * GDN_HEAD_DIM)
    wsl = lambda p: slice(p * GDN_PAIR, (p + 1) * GDN_PAIR)
    cells = [(b, p) for b in range(u_ref.shape[0]) for p in range(GDN_PAIRS)]
    states = [(state_ref[b, 2 * p], state_ref[b, 2 * p + 1]) for b, p in cells]
    boths = []
    for (b, p), (sa, sb) in zip(cells, states):
        s_bd = jnp.concatenate([jnp.concatenate([sa.astype(BF16), zblock], axis=1),
                                jnp.concatenate([zblock, sb.astype(BF16)], axis=1)], axis=0)
        boths.append(_dot(jnp.concatenate([w_ref[b, :, wsl(p)], qg_ref[b, :, wsl(p)]], axis=0), s_bd))
    v16s = [(u_ref[b, :, wsl(p)] - both[:c]).astype(BF16) for (b, p), both in zip(cells, boths)]
    outs, upds = [], []
    for (b, p), both, v16 in zip(cells, boths, v16s):
        v_bd = _block_diag(v16, 2)
        outs.append(both[c:] + _dot(qk_ref[b, :, p * 2 * c:(p + 1) * 2 * c], v_bd))
        upds.append([lax.dot_general(kd_ref[b, :, hsl(2 * p + h)], v16[:, hsl(h)],
                                     (((0,), (0,)), ((), ())), preferred_element_type=F32)
                     for h in range(2)])
    for (b, p), st, o, upd in zip(cells, states, outs, upds):
        halves = []
        for h in range(2):
            head = 2 * p + h
            state_ref[b, head] = st[h] * gl_ref[b, 0][:, hsl(head)] + upd[h]
            oh = o[:, hsl(h)]
            ms = jnp.mean(oh * oh, axis=-1, keepdims=True)
            halves.append(oh * lax.rsqrt(ms + EPS) * nw)
        o_ref[b, :, wsl(p)] = (jnp.concatenate(halves, axis=1)
                               * _silu(z_ref[b, :, wsl(p)])).astype(o_ref.dtype)


def _gdn(gdn, small, conv_w3, alog_row, dtb_row, nw_row, batch, seq):
    t = gdn.shape[0]
    c = GDN_CHUNK
    w = GDN_WIDTH
    rows = GDN_PREP_CHUNKS * c
    e_g128, e_b128, e_g64 = _gdn_expanders()
    n_conv = conv_w3.shape[1]
    tok = lambda col: (lambda i: (i, col))
    halo = lambda col: (lambda i: (jnp.maximum(i * (rows // CONV_HALO) - 1, 0), col))
    const2 = lambda i: (0, 0)
    u, wk, qg, kd, qk, gl = pl.pallas_call(
        functools.partial(_gdn_prep_kernel, blocks_per_seq=seq // rows),
        grid=(t // rows,),
        in_specs=[pl.BlockSpec((rows, w), tok(0)),
                  pl.BlockSpec((rows, w), tok(1)),
                  pl.BlockSpec((rows, w), tok(2)),
                  pl.BlockSpec((CONV_HALO, w), halo(0)),
                  pl.BlockSpec((CONV_HALO, w), halo(1)),
                  pl.BlockSpec((CONV_HALO, w), halo(2)),
                  pl.BlockSpec((rows, LANES), tok(0)),
                  pl.BlockSpec((3, n_conv, w), lambda i: (0, 0, 0)),
                  pl.BlockSpec((1, LANES), const2),
                  pl.BlockSpec((1, LANES), const2),
                  pl.BlockSpec((LANES, w), const2),
                  pl.BlockSpec((LANES, w), const2),
                  pl.BlockSpec((LANES, GDN_CAT), const2)],
        out_specs=[pl.BlockSpec((rows, w), tok(0)),
                   pl.BlockSpec((rows, w), tok(0)),
                   pl.BlockSpec((rows, w), tok(0)),
                   pl.BlockSpec((rows, w), tok(0)),
                   pl.BlockSpec((rows, GDN_CAT), tok(0)),
                   pl.BlockSpec((GDN_PREP_CHUNKS, 1, w), lambda i: (i, 0, 0))],
        out_shape=[jax.ShapeDtypeStruct((t, w), F32),
                   jax.ShapeDtypeStruct((t, w), BF16),
                   jax.ShapeDtypeStruct((t, w), BF16),
                   jax.ShapeDtypeStruct((t, w), BF16),
                   jax.ShapeDtypeStruct((t, GDN_CAT), BF16),
                   jax.ShapeDtypeStruct((t // c, 1, w), F32)],
        compiler_params=_cparams("parallel"),
        name="gdn_prep",
    )(gdn, gdn, gdn, gdn, gdn, gdn, small, conv_w3, alog_row, dtb_row,
      jnp.asarray(e_g128, BF16), jnp.asarray(e_b128, BF16), jnp.asarray(e_g64, BF16))

    seq3 = lambda a: a.reshape(batch, seq, a.shape[-1])
    blk = lambda width, col=0: pl.BlockSpec((batch, c, width), lambda n: (0, n, col))
    o = pl.pallas_call(
        _gdn_scan_kernel,
        grid=(seq // c,),
        in_specs=[blk(w), blk(w), blk(w), blk(w), blk(GDN_CAT),
                  pl.BlockSpec((batch, 1, 1, w), lambda n: (0, n, 0, 0)),
                  blk(w, 3),
                  pl.BlockSpec((1, GDN_HEAD_DIM), lambda n: (0, 0))],
        out_specs=blk(w),
        out_shape=jax.ShapeDtypeStruct((batch, seq, w), BF16),
        scratch_shapes=[pltpu.VMEM((batch, GDN_HEADS, GDN_HEAD_DIM, GDN_HEAD_DIM), F32)],
        compiler_params=_cparams("arbitrary"),
        name="gdn_scan",
    )(seq3(u), seq3(wk), seq3(qg), seq3(kd), seq3(qk), gl.reshape(batch, seq // c, 1, w),
      seq3(gdn), nw_row)
    return o.reshape(t, w)


def _merge_out_kernel(x_ref, oa_ref, ob_ref, ga_ref, gb_ref, wa_ref, wb_ref, wo_ref, o_ref):
    ya = _dot(oa_ref[...], wa_ref[...])
    yb = _dot(ob_ref[...], wb_ref[...])
    y = (_sigmoid(ga_ref[...].astype(F32)) * ya + _sigmoid(gb_ref[...].astype(F32)) * yb)
    o_ref[...] = x_ref[...] + _dot(y.astype(BF16), wo_ref[...])


def _merge_out(x, o_a, o_b, mg, w_a, w_b, w_o, tm):
    t, d = x.shape
    tokd = pl.BlockSpec((tm, d), lambda i: (i, 0))
    wspec = pl.BlockSpec((d, d), lambda i: (0, 0))
    return pl.pallas_call(
        _merge_out_kernel,
        grid=(t // tm,),
        in_specs=[tokd, tokd, tokd,
                  pl.BlockSpec((tm, d), lambda i: (i, 0)),
                  pl.BlockSpec((tm, d), lambda i: (i, 1)),
                  wspec, wspec, wspec],
        out_specs=tokd,
        out_shape=jax.ShapeDtypeStruct((t, d), F32),
        compiler_params=_cparams("parallel"),
        name="merge_out",
    )(x, o_a, o_b, mg, mg, w_a, w_b, w_o)


def _ffn_kernel(x_ref, nw_ref, wg_ref, wu_ref, wd_ref, fw_ref, o_ref, *, final_norm):
    x = x_ref[...]
    h = _rms_norm(x, nw_ref[...]).astype(BF16)
    a = (_silu(_dot(h, wg_ref[...])) * _dot(h, wu_ref[...])).astype(BF16)
    y = x + _dot(a, wd_ref[...])
    if final_norm:
        y = _rms_norm(y, fw_ref[...])
    o_ref[...] = y


def _ffn(x, nw, w_g, w_u, w_d, fw, final_norm, tm):
    t, d = x.shape
    f = w_g.shape[1]
    tokd = pl.BlockSpec((tm, d), lambda i: (i, 0))
    rowd = pl.BlockSpec((1, d), lambda i: (0, 0))
    return pl.pallas_call(
        functools.partial(_ffn_kernel, final_norm=final_norm),
        grid=(t // tm,),
        in_specs=[tokd, rowd,
                  pl.BlockSpec((d, f), lambda i: (0, 0), pipeline_mode=pl.Buffered(1)),
                  pl.BlockSpec((d, f), lambda i: (0, 0), pipeline_mode=pl.Buffered(1)),
                  pl.BlockSpec((f, d), lambda i: (0, 0), pipeline_mode=pl.Buffered(1)),
                  rowd],
        out_specs=tokd,
        out_shape=jax.ShapeDtypeStruct((t, d), F32),
        compiler_params=_cparams("parallel"),
        name="ffn",
    )(x, nw, w_g, w_u, w_d, fw)


def _pad_row(v, offset):
    return jnp.zeros((1, LANES), F32).at[0, offset:offset + v.shape[0]].set(v.astype(F32))


def _layer(x, batch, seq, norm_mix_w, w_in, conv_w, a_log, dt_bias, gdn_norm_w, fox_f_bias,
           w_branch_a, w_branch_b, w_out, norm_ffn_w, w_gate, w_up, w_down, final_w, final_norm):
    d = x.shape[1]
    gw, fw = GDN_WIDTH, FOX_HEADS * FOX_HEAD_DIM
    sizes = (gw, gw, gw, gw, GDN_HEADS, GDN_HEADS, fw, fw, fw, FOX_HEADS, fw, d, d)
    offs = np.concatenate([[0], np.cumsum(sizes)])
    col = lambda i: w_in[:, offs[i]:offs[i + 1]]
    w_gdn = jnp.concatenate([col(0), col(1), col(2), col(3)], axis=1).astype(BF16)
    w_fox = jnp.concatenate([col(6), col(7), col(10)], axis=1).astype(BF16)
    w_fv_t = col(8).T.astype(BF16)
    w_mg = jnp.concatenate([col(11), col(12)], axis=1).astype(BF16)
    n_small = 2 * GDN_HEADS + FOX_HEADS
    w_small = jnp.concatenate([col(4), col(5), col(9), jnp.zeros((d, LANES - n_small), F32)],
                              axis=1).astype(BF16)
    nw = norm_mix_w.reshape(1, d)

    tm = min(seq, TOKEN_TILE)
    (gdn,) = _norm_proj(x, nw, [(w_gdn, F32)], [], tm)
    fox, vt = _norm_proj(x, nw, [(w_fox, BF16)], [(w_fv_t, BF16)], tm)
    mg, small = _norm_proj(x, nw, [(w_mg, BF16), (w_small, F32)], [], tm)

    eq, ek = _fox_bias(small, _pad_row(fox_f_bias, SMALL_FF), batch, seq, tm)
    o_b = _fox_attention(fox, vt, eq, ek, batch, seq, tq=tm)

    conv_w3 = conv_w.reshape(conv_w.shape[0], 3, gw).transpose(1, 0, 2)
    o_a = _gdn(gdn, small, conv_w3, _pad_row(a_log, SMALL_GA), _pad_row(dt_bias, SMALL_GA),
               gdn_norm_w.reshape(1, GDN_HEAD_DIM), batch, seq)

    x1 = _merge_out(x, o_a, o_b, mg, w_branch_a.astype(BF16), w_branch_b.astype(BF16),
                    w_out.astype(BF16), tm)
    return _ffn(x1, norm_ffn_w.reshape(1, d), w_gate.astype(BF16), w_up.astype(BF16),
                w_down.astype(BF16), final_w.reshape(1, d), final_norm, tm)


def kernel(x, norm_mix_w, w_in, conv_w, a_log, dt_bias, gdn_norm_w, fox_f_bias, w_branch_a,
           w_branch_b, w_out, norm_ffn_w, w_gate, w_up, w_down, norm_final_w):
    batch, seq, d = x.shape
    depth = w_in.shape[0]
    h = x.reshape(batch * seq, d)
    for l in range(depth):
        h = _layer(h, batch, seq, norm_mix_w[l], w_in[l], conv_w[l], a_log[l], dt_bias[l],
                   gdn_norm_w[l], fox_f_bias[l], w_branch_a[l], w_branch_b[l], w_out[l],
                   norm_ffn_w[l], w_gate[l], w_up[l], w_down[l], norm_final_w,
                   final_norm=(l == depth - 1))
    return h.reshape(batch, seq, d)
```

```python
import functools

import jax
import jax.numpy as jnp
import numpy as np
from jax import lax
from jax.experimental import pallas as pl
from jax.experimental.pallas import tpu as pltpu

F32 = jnp.float32
BF16 = jnp.bfloat16

EPS = 1e-6
GDN_HEADS = 8
GDN_HEAD_DIM = 128
GDN_CHUNK = 64
FOX_HEADS = 16
FOX_HEAD_DIM = 64
LANES = 128
VMEM_LIMIT_BYTES = 56 * 1024 * 1024
TOKEN_TILE = 512
NEG_BIG = -1e30


def _cparams(*semantics):
    return pltpu.CompilerParams(dimension_semantics=semantics,
                                vmem_limit_bytes=VMEM_LIMIT_BYTES)


def _split2(x):
    hi = x.astype(BF16)
    lo = (x - hi.astype(F32)).astype(BF16)
    return hi, lo


def _split3(x):
    hi = x.astype(BF16)
    r = x - hi.astype(F32)
    mid = r.astype(BF16)
    lo = (r - mid.astype(F32)).astype(BF16)
    return hi, mid, lo


def _dot(a, b):
    return jnp.dot(a, b, preferred_element_type=F32)


def _dot_exact_rhs(x, m_bf16):
    n = x.shape[0]
    y = _dot(jnp.concatenate(_split3(x), axis=0), m_bf16)
    return y[:n] + y[n:2 * n] + y[2 * n:]


def _dot_exact_lhs(m_bf16, x):
    n = x.shape[1]
    y = _dot(m_bf16, jnp.concatenate(_split3(x), axis=1))
    return y[:, :n] + y[:, n:2 * n] + y[:, 2 * n:]


def _div_pow2(x, n):
    assert n & (n - 1) == 0
    return jnp.right_shift(x, n.bit_length() - 1)


def _mod_pow2(x, n):
    assert n & (n - 1) == 0
    return jnp.bitwise_and(x, n - 1)


def _softplus(y):
    return jnp.maximum(y, 0.0) + jnp.log(1.0 + jnp.exp(-jnp.abs(y)))


def _sigmoid(y):
    return 0.5 * jnp.tanh(0.5 * y) + 0.5


def _silu(y):
    half = 0.5 * y
    return half + half * jnp.tanh(half)


def _rms_norm(x, w):
    ms = jnp.mean(x * x, axis=-1, keepdims=True)
    return x * lax.rsqrt(ms + EPS) * w


PROJ_CHUNK = 1024


def _norm_proj_kernel(x_ref, nw_ref, *refs, n_plain):
    n_w = len(refs) // 2
    h = _rms_norm(x_ref[...], nw_ref[...]).astype(BF16)
    for idx in range(n_w):
        w_ref, o_ref = refs[idx], refs[n_w + idx]
        if idx < n_plain:
            n = o_ref.shape[1]
            for c in range(0, n, PROJ_CHUNK):
                sl = slice(c, min(c + PROJ_CHUNK, n))
                o_ref[:, sl] = _dot(h, w_ref[:, sl]).astype(o_ref.dtype)
        else:
            n = o_ref.shape[0]
            for c in range(0, n, PROJ_CHUNK):
                sl = slice(c, min(c + PROJ_CHUNK, n))
                o_ref[sl, :] = lax.dot_general(w_ref[sl, :], h, (((1,), (1,)), ((), ())),
                                               preferred_element_type=F32).astype(o_ref.dtype)


def _norm_proj(x, nw, plain, transposed, tm):
    t, d = x.shape
    in_specs = [pl.BlockSpec((tm, d), lambda i: (i, 0)), pl.BlockSpec((1, d), lambda i: (0, 0))]
    out_specs, out_shape = [], []
    for w, dtype in plain:
        n = w.shape[1]
        in_specs.append(pl.BlockSpec((d, n), lambda i: (0, 0)))
        out_specs.append(pl.BlockSpec((tm, n), lambda i: (i, 0)))
        out_shape.append(jax.ShapeDtypeStruct((t, n), dtype))
    for wt, dtype in transposed:
        n = wt.shape[0]
        in_specs.append(pl.BlockSpec((n, d), lambda i: (0, 0)))
        out_specs.append(pl.BlockSpec((n, tm), lambda i: (0, i)))
        out_shape.append(jax.ShapeDtypeStruct((n, t), dtype))
    return pl.pallas_call(
        functools.partial(_norm_proj_kernel, n_plain=len(plain)),
        grid=(t // tm,),
        in_specs=in_specs,
        out_specs=out_specs,
        out_shape=out_shape,
        compiler_params=_cparams("parallel"),
        name="norm_proj",
    )(x, nw, *[w for w, _ in plain], *[w for w, _ in transposed])


SMALL_GA = 0
SMALL_GB = GDN_HEADS
SMALL_FF = 2 * GDN_HEADS
BIAS_TERMS = 3
BIAS_HEAD_STRIDE = 8


def _bias_placements():
    n_out = (FOX_HEADS // 2) * LANES
    pq = np.zeros((LANES, n_out), np.float32)
    pk = np.zeros((LANES, n_out), np.float32)
    ones_q = np.zeros((1, n_out), np.float32)
    ones_k = np.zeros((1, n_out), np.float32)
    for h in range(FOX_HEADS):
        base = (h // 2) * LANES + (h % 2) * BIAS_HEAD_STRIDE
        for t in range(BIAS_TERMS):
            pq[SMALL_FF + t * FOX_HEADS + h, base + t] = 1.0
            pk[SMALL_FF + t * FOX_HEADS + h, base + BIAS_TERMS + t] = -1.0
            ones_q[0, base + BIAS_TERMS + t] = 1.0
            ones_k[0, base + t] = 1.0
    return pq, pk, ones_q, ones_k


def _fox_bias_kernel(small_ref, fb_ref, pq_ref, pk_ref, oq_ref, ok_ref, eq_ref, ek_ref,
                     carry_ref):
    tm = small_ref.shape[0]

    @pl.when(pl.program_id(1) == 0)
    def _():
        carry_ref[...] = jnp.zeros_like(carry_ref)

    z = small_ref[...] + fb_ref[...]
    log_f = -_softplus(-z)
    row = lax.broadcasted_iota(jnp.int32, (tm, tm), 0)
    col = lax.broadcasted_iota(jnp.int32, (tm, tm), 1)
    tril = jnp.where(row >= col, 1.0, 0.0).astype(BF16)
    cum = _dot_exact_lhs(tril, log_f) + carry_ref[0:1, :]
    carry_ref[...] = jnp.broadcast_to(cum[tm - 1:tm, :], carry_ref.shape)
    lane = lax.broadcasted_iota(jnp.int32, (1, LANES), 1)
    is_ff = jnp.logical_and(lane >= SMALL_FF, lane < SMALL_FF + FOX_HEADS)
    packed = jnp.zeros_like(cum)
    for t, term in enumerate(_split3(cum)):
        part = jnp.where(is_ff, term.astype(F32), 0.0)
        packed = packed + (pltpu.roll(part, t * FOX_HEADS, axis=1) if t else part)
    terms = packed.astype(BF16)
    eq_ref[...] = (oq_ref[...] + _dot(terms, pq_ref[...])).astype(BF16)
    ek_ref[...] = (ok_ref[...] + _dot(terms, pk_ref[...])).astype(BF16)


def _fox_bias(small, fb_row, batch, seq, tm):
    t = small.shape[0]
    pq, pk, ones_q, ones_k = _bias_placements()
    n_out = pq.shape[1]
    nt = seq // tm
    const2 = lambda b, i: (0, 0)
    return pl.pallas_call(
        _fox_bias_kernel,
        grid=(batch, nt),
        in_specs=[pl.BlockSpec((tm, LANES), lambda b, i: (b * nt + i, 0)),
                  pl.BlockSpec((1, LANES), const2),
                  pl.BlockSpec((LANES, n_out), const2),
                  pl.BlockSpec((LANES, n_out), const2),
                  pl.BlockSpec((1, n_out), const2),
                  pl.BlockSpec((1, n_out), const2)],
        out_specs=[pl.BlockSpec((tm, n_out), lambda b, i: (b * nt + i, 0)),
                   pl.BlockSpec((tm, n_out), lambda b, i: (b * nt + i, 0))],
        out_shape=[jax.ShapeDtypeStruct((t, n_out), BF16),
                   jax.ShapeDtypeStruct((t, n_out), BF16)],
        scratch_shapes=[pltpu.VMEM((8, LANES), F32)],
        compiler_params=_cparams("parallel", "arbitrary"),
        name="fox_bias",
    )(small, fb_row, jnp.asarray(pq, BF16), jnp.asarray(pk, BF16),
      jnp.asarray(ones_q), jnp.asarray(ones_k))


FOX_PAIRS_PER_STEP = 4
FOX_SCORES_AHEAD = 3


def _fox_attention_kernel(q_ref, eq_ref, k_ref, ek_ref, vt_ref, fo_ref, o_ref, *, tq):
    i = pl.program_id(2)
    n_pairs = q_ref.shape[1] // LANES
    n_heads = 2 * n_pairs
    lane = lax.broadcasted_iota(jnp.int32, (1, LANES), 1)
    head_a = lane < FOX_HEAD_DIM
    bias_a = lane < BIAS_HEAD_STRIDE
    row_a = lax.broadcasted_iota(jnp.int32, (LANES, 1), 0) < FOX_HEAD_DIM
    zero = jnp.zeros((), BF16)
    one = jnp.ones((), BF16)
    qm = []
    for p in range(n_pairs):
        psl = slice(p * LANES, (p + 1) * LANES)
        q = q_ref[:, psl] * jnp.asarray(FOX_HEAD_DIM ** -0.5, BF16)
        eq = eq_ref[:, psl]
        qm.append(jnp.concatenate([jnp.where(head_a, q, zero), jnp.where(bias_a, eq, zero)], axis=1))
        qm.append(jnp.concatenate([jnp.where(head_a, zero, q), jnp.where(bias_a, zero, eq)], axis=1))

    def tile(j, carry, masked):
        start = pl.multiple_of(j * tq, tq)

        def score(idx):
            psl = slice((idx // 2) * LANES, (idx // 2 + 1) * LANES)
            kk = jnp.concatenate([k_ref[pl.ds(start, tq), psl], ek_ref[pl.ds(start, tq), psl]], axis=1)
            return lax.dot_general(kk, qm[idx], (((1,), (1,)), ((), ())),
                                   preferred_element_type=F32)

        def value(idx):
            p, h = divmod(idx, 2)
            vt = vt_ref[p * LANES:(p + 1) * LANES, pl.ds(start, tq)]
            return jnp.where(row_a, vt, one) if h == 0 else jnp.where(row_a, one, vt)

        out = []
        ahead = [score(idx) for idx in range(min(FOX_SCORES_AHEAD, n_heads))]
        for idx in range(n_heads):
            s = ahead.pop(0)
            if idx + FOX_SCORES_AHEAD < n_heads:
                ahead.append(score(idx + FOX_SCORES_AHEAD))
            m, acc = carry[idx]
            if masked:
                key = lax.broadcasted_iota(jnp.int32, (tq, tq), 0)
                qry = lax.broadcasted_iota(jnp.int32, (tq, tq), 1)
                s = jnp.where(key <= qry, s, NEG_BIG)
            m_new = jnp.maximum(m, jnp.max(s, axis=0, keepdims=True))
            alpha = jnp.exp(m - m_new)
            prob = jnp.exp(s - m_new).astype(BF16)
            acc = alpha * acc + _dot(value(idx), prob)
            out.append((m_new, acc))
        return tuple(out)

    init = tuple((jnp.full((1, tq), NEG_BIG, F32), jnp.zeros((LANES, tq), F32))
                 for _ in range(n_heads))
    carry = lax.fori_loop(0, i, lambda j, c: tile(j, c, False), init)
    final = tile(i, carry, True)
    for p in range(n_pairs):
        psl = slice(p * LANES, (p + 1) * LANES)
        acc_a, acc_b = final[2 * p][1], final[2 * p + 1][1]
        num = jnp.where(row_a, acc_a, acc_b)
        den = jnp.where(row_a, acc_a[FOX_HEAD_DIM:FOX_HEAD_DIM + 1, :], acc_b[0:1, :])
        gate = _sigmoid(fo_ref[:, psl].astype(F32))
        o_ref[:, psl] = ((num / den).T * gate).astype(o_ref.dtype)


def _fox_attention(fox, vt, eq, ek, batch, seq, tq):
    t = fox.shape[0]
    pairs = FOX_HEADS // 2
    steps = pairs // FOX_PAIRS_PER_STEP
    width = FOX_PAIRS_PER_STEP * LANES
    nq = seq // tq
    return pl.pallas_call(
        functools.partial(_fox_attention_kernel, tq=tq),
        grid=(batch, steps, nq),
        in_specs=[pl.BlockSpec((tq, width), lambda b, p, i: (b * nq + i, p)),
                  pl.BlockSpec((tq, width), lambda b, p, i: (b * nq + i, p)),
                  pl.BlockSpec((seq, width), lambda b, p, i: (b, steps + p)),
                  pl.BlockSpec((seq, width), lambda b, p, i: (b, p)),
                  pl.BlockSpec((width, seq), lambda b, p, i: (p, b)),
                  pl.BlockSpec((tq, width), lambda b, p, i: (b * nq + i, 2 * steps + p))],
        out_specs=pl.BlockSpec((tq, width), lambda b, p, i: (b * nq + i, p)),
        out_shape=jax.ShapeDtypeStruct((t, pairs * LANES), BF16),
        compiler_params=_cparams("parallel", "parallel", "arbitrary"),
        name="fox_attention",
    )(fox, eq, fox, ek, vt, fox)


GDN_WIDTH = GDN_HEADS * GDN_HEAD_DIM
GDN_GROUP = 4
GDN_CAT = GDN_HEADS * GDN_CHUNK
GDN_PAIRS = GDN_HEADS // 2
GDN_PAIR = 2 * GDN_HEAD_DIM
GDN_PREP_CHUNKS = 4
CONV_HALO = 8


def _gdn_expanders():
    e_g128 = np.zeros((LANES, GDN_WIDTH), np.float32)
    e_b128 = np.zeros((LANES, GDN_WIDTH), np.float32)
    e_g64 = np.zeros((LANES, GDN_CAT), np.float32)
    for h in range(GDN_HEADS):
        e_g128[SMALL_GA + h, h * GDN_HEAD_DIM:(h + 1) * GDN_HEAD_DIM] = 1.0
        e_b128[SMALL_GB + h, h * GDN_HEAD_DIM:(h + 1) * GDN_HEAD_DIM] = 1.0
        e_g64[SMALL_GA + h, h * GDN_CHUNK:(h + 1) * GDN_CHUNK] = 1.0
    return e_g128, e_b128, e_g64


def _block_diag(x, n_blocks):
    r, total = x.shape
    w = total // n_blocks
    tile_w = max(w, LANES)
    per_tile = tile_w // w
    zeros = jnp.zeros((r, tile_w), x.dtype)
    lane_block = _div_pow2(lax.broadcasted_iota(jnp.int32, (1, tile_w), 1), w)
    rows = []
    for h in range(n_blocks):
        t = h // per_tile
        tile = x[:, t * tile_w:(t + 1) * tile_w]
        if per_tile > 1:
            tile = tile * jnp.where(lane_block == h % per_tile, 1.0, 0.0).astype(x.dtype)
        rows.append(jnp.concatenate([tile if i == t else zeros for i in range(total // tile_w)], axis=1))
    return jnp.concatenate(rows, axis=0)


def _rows(x, i, n):
    return x[i * n:(i + 1) * n]


def _headwise_products(lhs, b):
    c = b.shape[0]
    n = len(lhs)
    b_hi, b_lo = _split2(b)
    bd_hi = _block_diag(b_hi, GDN_GROUP)
    bd_lo = _block_diag(b_lo, GDN_GROUP)
    parts = [_split2(x) for x in lhs]
    his = [p[0] for p in parts]
    los = [p[1] for p in parts]
    top = _dot(jnp.concatenate(his + los, axis=0), bd_hi)
    bot = _dot(jnp.concatenate(his, axis=0), bd_lo)
    return [_rows(top, i, c) + _rows(top, n + i, c) + _rows(bot, i, c) for i in range(n)]


def _unit_lower_inverses(l_cats):
    c, n = l_cats[0].shape
    r = lax.broadcasted_iota(jnp.int32, (c, n), 0)
    j = _mod_pow2(lax.broadcasted_iota(jnp.int32, (c, n), 1), c)
    eye = jnp.where(r == j, 1.0, 0.0)
    ss = [eye - l for l in l_cats]
    ps = [_headwise_products([l], l)[0] for l in l_cats]
    k = 2
    while 2 * k < c:
        stage = [_headwise_products([p, s], p) for p, s in zip(ps, ss)]
        ps = [st[0] for st in stage]
        ss = [s + st[1] for s, st in zip(ss, stage)]
        k *= 2
    return [s + _headwise_products([s], p)[0] for p, s in zip(ps, ss)]


def _gdn_prep_kernel(q_ref, k_ref, v_ref, hq_ref, hk_ref, hv_ref, small_ref, cw_ref, alog_ref,
                     dtb_ref, eg128_ref, eb128_ref, eg64_ref,
                     u_ref, w_ref, qg_ref, kd_ref, qk_ref, gl_ref, *, blocks_per_seq):
    c = GDN_CHUNK
    n_conv = cw_ref.shape[1]
    seq_start = lax.rem(pl.program_id(0), blocks_per_seq) == 0

    sublane = lax.broadcasted_iota(jnp.int32, (1, CONV_HALO, 1), 1)
    conv = []
    for s, (ref, halo_ref) in enumerate(((q_ref, hq_ref), (k_ref, hk_ref), (v_ref, hv_ref))):
        x = ref[...]
        halo = jnp.where(seq_start, 0.0, halo_ref[...])
        groups = x.reshape(-1, CONV_HALO, x.shape[1])
        prev = jnp.concatenate([halo[None], groups[:-1]], axis=0)
        acc = x * cw_ref[s, n_conv - 1:n_conv, :]
        for shift in range(1, n_conv):
            mixed = jnp.where(sublane >= CONV_HALO - shift, prev, groups)
            moved = pltpu.roll(mixed, shift, axis=1).reshape(x.shape)
            acc = acc + moved * cw_ref[s, n_conv - 1 - shift:n_conv - shift, :]
        conv.append(_silu(acc))
    cq, ck, cv = conv
    rows = cq.shape[0]

    def l2n(x):
        parts = []
        for h in range(GDN_HEADS):
            xh = x[:, h * GDN_HEAD_DIM:(h + 1) * GDN_HEAD_DIM]
            ss = jnp.sum(xh * xh, axis=-1, keepdims=True)
            parts.append(xh * lax.rsqrt(ss + EPS))
        return jnp.concatenate(parts, axis=1)

    qn = l2n(cq)
    kn = l2n(ck)

    small = small_ref[...]
    g_tok = -jnp.exp(alog_ref[...]) * _softplus(small + dtb_ref[...])
    beta_tok = _sigmoid(small)
    row = lax.broadcasted_iota(jnp.int32, (rows, rows), 0)
    col = lax.broadcasted_iota(jnp.int32, (rows, rows), 1)
    same_chunk = _div_pow2(row, c) == _div_pow2(col, c)
    tril = jnp.where(row >= col, jnp.where(same_chunk, 1.0, 0.0), 0.0).astype(BF16)
    gc_tok = _dot_exact_lhs(tril, g_tok)
    gc128 = _dot_exact_rhs(gc_tok, eg128_ref[...])
    gc64 = _dot_exact_rhs(gc_tok, eg64_ref[...])
    beta128 = _dot_exact_rhs(beta_tok, eb128_ref[...])

    exp_gc = jnp.exp(gc128)
    kb = kn * beta128
    vb = cv * beta128
    kbg = kb * exp_gc
    qs = qn * (GDN_HEAD_DIM ** -0.5)
    qg_ref[...] = (qs * exp_gc).astype(BF16)
    kb16 = kb.astype(BF16)
    qs16 = qs.astype(BF16)
    kn16 = kn.astype(BF16)

    r_cat = lax.broadcasted_iota(jnp.int32, (c, GDN_CAT), 0)
    j_cat = _mod_pow2(lax.broadcasted_iota(jnp.int32, (c, GDN_CAT), 1), c)
    tri_cat = r_cat >= j_cat
    strict_cat = r_cat > j_cat
    group_w = GDN_GROUP * GDN_HEAD_DIM
    group_c = GDN_GROUP * c
    contract_last = (((1,), (1,)), ((), ()))

    n_chunks = rows // c
    n_groups = GDN_HEADS // GDN_GROUP
    l_cats = []
    for ch in range(n_chunks):
        rs = slice(ch * c, (ch + 1) * c)
        gc64_c = gc64[rs]
        gc_row = jnp.sum(jnp.where(r_cat == j_cat, gc64_c, 0.0), axis=0, keepdims=True)
        decay = jnp.where(tri_cat, jnp.exp(jnp.where(tri_cat, gc64_c - gc_row, 0.0)), 0.0)
        g_last = gc128[ch * c + c - 1:ch * c + c, :]
        gl_ref[ch] = jnp.exp(g_last)
        kd_ref[rs, :] = (kn[rs] * jnp.exp(g_last - gc128[rs])).astype(BF16)

        for gidx in range(n_groups):
            ksl = slice(gidx * group_w, (gidx + 1) * group_w)
            csl = slice(gidx * group_c, (gidx + 1) * group_c)
            bd_k = _block_diag(kn16[rs, ksl], GDN_GROUP)
            both = lax.dot_general(jnp.concatenate([kb16[rs, ksl], qs16[rs, ksl]], axis=0), bd_k,
                                   contract_last, preferred_element_type=F32)
            dec = decay[:, csl]
            l_cats.append(jnp.where(strict_cat[:, csl], both[:c] * dec, 0.0))
            qk_ref[rs, csl] = (both[c:] * dec).astype(BF16)

    invs = _unit_lower_inverses(l_cats)

    for ch in range(n_chunks):
        rs = slice(ch * c, (ch + 1) * c)
        inv_cat = jnp.concatenate(invs[ch * n_groups:(ch + 1) * n_groups], axis=1)
        for p in range(GDN_PAIRS):
            wsl = slice(p * GDN_PAIR, (p + 1) * GDN_PAIR)
            inv_p = inv_cat[:, p * 2 * c:(p + 1) * 2 * c]
            vb_p, kbg_p = vb[rs, wsl], kbg[rs, wsl]
            rhs = jnp.concatenate([_block_diag(vb_p, 2), _block_diag(kbg_p, 2)], axis=1)
            i_hi, i_lo = _split2(inv_p)
            r_hi, r_lo = _split2(rhs)
            top = _dot(jnp.concatenate([i_hi, i_lo], axis=0), r_hi)
            sol = top[:c] + top[c:] + _dot(i_hi, r_lo)
            u_ref[rs, wsl] = sol[:, :GDN_PAIR]
            w_ref[rs, wsl] = sol[:, GDN_PAIR:].astype(BF16)


def _gdn_scan_kernel(u_ref, w_ref, qg_ref, kd_ref, qk_ref, gl_ref, z_ref, nw_ref, o_ref, state_ref):
    c = GDN_CHUNK

    @pl.when(pl.program_id(0) == 0)
    def _():
        state_ref[...] = jnp.zeros_like(state_ref)

    nw = nw_ref[...]
    zblock = jnp.zeros((GDN_HEAD_DIM, GDN_HEAD_DIM), BF16)
    hsl = lambda h: slice(h * GDN_HEAD_DIM, (h + 1) * GDN_HEAD_DIM)
    wsl = lambda p: slice(p * GDN_PAIR, (p + 1) * GDN_PAIR)
    cells = [(b, p) for b in range(u_ref.shape[0]) for p in range(GDN_PAIRS)]
    states = [(state_ref[b, 2 * p], state_ref[b, 2 * p + 1]) for b, p in cells]
    boths = []
    for (b, p), (sa, sb) in zip(cells, states):
        s_bd = jnp.concatenate([jnp.concatenate([sa.astype(BF16), zblock], axis=1),
                                jnp.concatenate([zblock, sb.astype(BF16)], axis=1)], axis=0)
        boths.append(_dot(jnp.concatenate([w_ref[b, :, wsl(p)], qg_ref[b, :, wsl(p)]], axis=0), s_bd))
    v16s = [(u_ref[b, :, wsl(p)] - both[:c]).astype(BF16) for (b, p), both in zip(cells, boths)]
    outs, upds = [], []
    for (b, p), both, v16 in zip(cells, boths, v16s):
        v_bd = _block_diag(v16, 2)
        outs.append(both[c:] + _dot(qk_ref[b, :, p * 2 * c:(p + 1) * 2 * c], v_bd))
        upds.append([lax.dot_general(kd_ref[b, :, hsl(2 * p + h)], v16[:, hsl(h)],
                                     (((0,), (0,)), ((), ())), preferred_element_type=F32)
                     for h in range(2)])
    for (b, p), st, o, upd in zip(cells, states, outs, upds):
        halves = []
        for h in range(2):
            head = 2 * p + h
            state_ref[b, head] = st[h] * gl_ref[b, 0][:, hsl(head)] + upd[h]
            oh = o[:, hsl(h)]
            ms = jnp.mean(oh * oh, axis=-1, keepdims=True)
            halves.append(oh * lax.rsqrt(ms + EPS) * nw)
        o_ref[b, :, wsl(p)] = (jnp.concatenate(halves, axis=1)
                               * _silu(z_ref[b, :, wsl(p)])).astype(o_ref.dtype)


def _gdn(gdn, small, conv_w3, alog_row, dtb_row, nw_row, batch, seq):
    t = gdn.shape[0]
    c = GDN_CHUNK
    w = GDN_WIDTH
    rows = GDN_PREP_CHUNKS * c
    e_g128, e_b128, e_g64 = _gdn_expanders()
    n_conv = conv_w3.shape[1]
    tok = lambda col: (lambda i: (i, col))
    halo = lambda col: (lambda i: (jnp.maximum(i * (rows // CONV_HALO) - 1, 0), col))
    const2 = lambda i: (0, 0)
    u, wk, qg, kd, qk, gl = pl.pallas_call(
        functools.partial(_gdn_prep_kernel, blocks_per_seq=seq // rows),
        grid=(t // rows,),
        in_specs=[pl.BlockSpec((rows, w), tok(0)),
                  pl.BlockSpec((rows, w), tok(1)),
                  pl.BlockSpec((rows, w), tok(2)),
                  pl.BlockSpec((CONV_HALO, w), halo(0)),
                  pl.BlockSpec((CONV_HALO, w), halo(1)),
                  pl.BlockSpec((CONV_HALO, w), halo(2)),
                  pl.BlockSpec((rows, LANES), tok(0)),
                  pl.BlockSpec((3, n_conv, w), lambda i: (0, 0, 0)),
                  pl.BlockSpec((1, LANES), const2),
                  pl.BlockSpec((1, LANES), const2),
                  pl.BlockSpec((LANES, w), const2),
                  pl.BlockSpec((LANES, w), const2),
                  pl.BlockSpec((LANES, GDN_CAT), const2)],
        out_specs=[pl.BlockSpec((rows, w), tok(0)),
                   pl.BlockSpec((rows, w), tok(0)),
                   pl.BlockSpec((rows, w), tok(0)),
                   pl.BlockSpec((rows, w), tok(0)),
                   pl.BlockSpec((rows, GDN_CAT), tok(0)),
                   pl.BlockSpec((GDN_PREP_CHUNKS, 1, w), lambda i: (i, 0, 0))],
        out_shape=[jax.ShapeDtypeStruct((t, w), F32),
                   jax.ShapeDtypeStruct((t, w), BF16),
                   jax.ShapeDtypeStruct((t, w), BF16),
                   jax.ShapeDtypeStruct((t, w), BF16),
                   jax.ShapeDtypeStruct((t, GDN_CAT), BF16),
                   jax.ShapeDtypeStruct((t // c, 1, w), F32)],
        compiler_params=_cparams("parallel"),
        name="gdn_prep",
    )(gdn, gdn, gdn, gdn, gdn, gdn, small, conv_w3, alog_row, dtb_row,
      jnp.asarray(e_g128, BF16), jnp.asarray(e_b128, BF16), jnp.asarray(e_g64, BF16))

    seq3 = lambda a: a.reshape(batch, seq, a.shape[-1])
    blk = lambda width, col=0: pl.BlockSpec((batch, c, width), lambda n: (0, n, col))
    o = pl.pallas_call(
        _gdn_scan_kernel,
        grid=(seq // c,),
        in_specs=[blk(w), blk(w), blk(w), blk(w), blk(GDN_CAT),
                  pl.BlockSpec((batch, 1, 1, w), lambda n: (0, n, 0, 0)),
                  blk(w, 3),
                  pl.BlockSpec((1, GDN_HEAD_DIM), lambda n: (0, 0))],
        out_specs=blk(w),
        out_shape=jax.ShapeDtypeStruct((batch, seq, w), BF16),
        scratch_shapes=[pltpu.VMEM((batch, GDN_HEADS, GDN_HEAD_DIM, GDN_HEAD_DIM), F32)],
        compiler_params=_cparams("arbitrary"),
        name="gdn_scan",
    )(seq3(u), seq3(wk), seq3(qg), seq3(kd), seq3(qk), gl.reshape(batch, seq // c, 1, w),
      seq3(gdn), nw_row)
    return o.reshape(t, w)


def _merge_out_kernel(x_ref, oa_ref, ob_ref, ga_ref, gb_ref, wa_ref, wb_ref, wo_ref, o_ref):
    ya = _dot(oa_ref[...], wa_ref[...])
    yb = _dot(ob_ref[...], wb_ref[...])
    y = (_sigmoid(ga_ref[...].astype(F32)) * ya + _sigmoid(gb_ref[...].astype(F32)) * yb)
    o_ref[...] = x_ref[...] + _dot(y.astype(BF16), wo_ref[...])


def _merge_out(x, o_a, o_b, mg, w_a, w_b, w_o, tm):
    t, d = x.shape
    tokd = pl.BlockSpec((tm, d), lambda i: (i, 0))
    wspec = pl.BlockSpec((d, d), lambda i: (0, 0))
    return pl.pallas_call(
        _merge_out_kernel,
        grid=(t // tm,),
        in_specs=[tokd, tokd, tokd,
                  pl.BlockSpec((tm, d), lambda i: (i, 0)),
                  pl.BlockSpec((tm, d), lambda i: (i, 1)),
                  wspec, wspec, wspec],
        out_specs=tokd,
        out_shape=jax.ShapeDtypeStruct((t, d), F32),
        compiler_params=_cparams("parallel"),
        name="merge_out",
    )(x, o_a, o_b, mg, mg, w_a, w_b, w_o)


def _ffn_kernel(x_ref, nw_ref, wg_ref, wu_ref, wd_ref, fw_ref, o_ref, *, final_norm):
    x = x_ref[...]
    h = _rms_norm(x, nw_ref[...]).astype(BF16)
    a = (_silu(_dot(h, wg_ref[...])) * _dot(h, wu_ref[...])).astype(BF16)
    y = x + _dot(a, wd_ref[...])
    if final_norm:
        y = _rms_norm(y, fw_ref[...])
    o_ref[...] = y


def _ffn(x, nw, w_g, w_u, w_d, fw, final_norm, tm):
    t, d = x.shape
    f = w_g.shape[1]
    tokd = pl.BlockSpec((tm, d), lambda i: (i, 0))
    rowd = pl.BlockSpec((1, d), lambda i: (0, 0))
    return pl.pallas_call(
        functools.partial(_ffn_kernel, final_norm=final_norm),
        grid=(t // tm,),
        in_specs=[tokd, rowd,
                  pl.BlockSpec((d, f), lambda i: (0, 0), pipeline_mode=pl.Buffered(1)),
                  pl.BlockSpec((d, f), lambda i: (0, 0), pipeline_mode=pl.Buffered(1)),
                  pl.BlockSpec((f, d), lambda i: (0, 0), pipeline_mode=pl.Buffered(1)),
                  rowd],
        out_specs=tokd,
        out_shape=jax.ShapeDtypeStruct((t, d), F32),
        compiler_params=_cparams("parallel"),
        name="ffn",
    )(x, nw, w_g, w_u, w_d, fw)


def _pad_row(v, offset):
    return jnp.zeros((1, LANES), F32).at[0, offset:offset + v.shape[0]].set(v.astype(F32))


def _layer(x, batch, seq, norm_mix_w, w_in, conv_w, a_log, dt_bias, gdn_norm_w, fox_f_bias,
           w_branch_a, w_branch_b, w_out, norm_ffn_w, w_gate, w_up, w_down, final_w, final_norm):
    d = x.shape[1]
    gw, fw = GDN_WIDTH, FOX_HEADS * FOX_HEAD_DIM
    sizes = (gw, gw, gw, gw, GDN_HEADS, GDN_HEADS, fw, fw, fw, FOX_HEADS, fw, d, d)
    offs = np.concatenate([[0], np.cumsum(sizes)])
    col = lambda i: w_in[:, offs[i]:offs[i + 1]]
    w_gdn = jnp.concatenate([col(0), col(1), col(2), col(3)], axis=1).astype(BF16)
    w_fox = jnp.concatenate([col(6), col(7), col(10)], axis=1).astype(BF16)
    w_fv_t = col(8).T.astype(BF16)
    w_mg = jnp.concatenate([col(11), col(12)], axis=1).astype(BF16)
    n_small = 2 * GDN_HEADS + FOX_HEADS
    w_small = jnp.concatenate([col(4), col(5), col(9), jnp.zeros((d, LANES - n_small), F32)],
                              axis=1).astype(BF16)
    nw = norm_mix_w.reshape(1, d)

    tm = min(seq, TOKEN_TILE)
    (gdn,) = _norm_proj(x, nw, [(w_gdn, F32)], [], tm)
    fox, vt = _norm_proj(x, nw, [(w_fox, BF16)], [(w_fv_t, BF16)], tm)
    mg, small = _norm_proj(x, nw, [(w_mg, BF16), (w_small, F32)], [], tm)

    eq, ek = _fox_bias(small, _pad_row(fox_f_bias, SMALL_FF), batch, seq, tm)
    o_b = _fox_attention(fox, vt, eq, ek, batch, seq, tq=tm)

    conv_w3 = conv_w.reshape(conv_w.shape[0], 3, gw).transpose(1, 0, 2)
    o_a = _gdn(gdn, small, conv_w3, _pad_row(a_log, SMALL_GA), _pad_row(dt_bias, SMALL_GA),
               gdn_norm_w.reshape(1, GDN_HEAD_DIM), batch, seq)

    x1 = _merge_out(x, o_a, o_b, mg, w_branch_a.astype(BF16), w_branch_b.astype(BF16),
                    w_out.astype(BF16), tm)
    return _ffn(x1, norm_ffn_w.reshape(1, d), w_gate.astype(BF16), w_up.astype(BF16),
                w_down.astype(BF16), final_w.reshape(1, d), final_norm, tm)


def kernel(x, norm_mix_w, w_in, conv_w, a_log, dt_bias, gdn_norm_w, fox_f_bias, w_branch_a,
           w_branch_b, w_out, norm_ffn_w, w_gate, w_up, w_down, norm_final_w):
    batch, seq, d = x.shape
    depth = w_in.shape[0]
    h = x.reshape(batch * seq, d)
    for l in range(depth):
        h = _layer(h, batch, seq, norm_mix_w[l], w_in[l], conv_w[l], a_log[l], dt_bias[l],
                   gdn_norm_w[l], fox_f_bias[l], w_branch_a[l], w_branch_b[l], w_out[l],
                   norm_ffn_w[l], w_gate[l], w_up[l], w_down[l], norm_final_w,
                   final_norm=(l == depth - 1))
    return h.reshape(batch, seq, d)
```

```python
import functools

import jax
import jax.numpy as jnp
import numpy as np
from jax import lax
from jax.experimental import pallas as pl
from jax.experimental.pallas import tpu as pltpu

F32 = jnp.float32
BF16 = jnp.bfloat16

EPS = 1e-6
GDN_HEADS = 8
GDN_HEAD_DIM = 128
GDN_CHUNK = 64
FOX_HEADS = 16
FOX_HEAD_DIM = 64
LANES = 128
VMEM_LIMIT_BYTES = 56 * 1024 * 1024
TOKEN_TILE = 512
NEG_BIG = -1e30


def _cparams(*semantics):
    return pltpu.CompilerParams(dimension_semantics=semantics,
                                vmem_limit_bytes=VMEM_LIMIT_BYTES)


def _split2(x):
    hi = x.astype(BF16)
    lo = (x - hi.astype(F32)).astype(BF16)
    return hi, lo


def _split3(x):
    hi = x.astype(BF16)
    r = x - hi.astype(F32)
    mid = r.astype(BF16)
    lo = (r - mid.astype(F32)).astype(BF16)
    return hi, mid, lo


def _dot(a, b):
    return jnp.dot(a, b, preferred_element_type=F32)


def _dot_exact_rhs(x, m_bf16):
    n = x.shape[0]
    y = _dot(jnp.concatenate(_split3(x), axis=0), m_bf16)
    return y[:n] + y[n:2 * n] + y[2 * n:]


def _dot_exact_lhs(m_bf16, x):
    n = x.shape[1]
    y = _dot(m_bf16, jnp.concatenate(_split3(x), axis=1))
    return y[:, :n] + y[:, n:2 * n] + y[:, 2 * n:]


def _div_pow2(x, n):
    assert n & (n - 1) == 0
    return jnp.right_shift(x, n.bit_length() - 1)


def _mod_pow2(x, n):
    assert n & (n - 1) == 0
    return jnp.bitwise_and(x, n - 1)


def _softplus(y):
    return jnp.maximum(y, 0.0) + jnp.log(1.0 + jnp.exp(-jnp.abs(y)))


def _sigmoid(y):
    return 0.5 * jnp.tanh(0.5 * y) + 0.5


def _silu(y):
    half = 0.5 * y
    return half + half * jnp.tanh(half)


def _rms_norm(x, w):
    ms = jnp.mean(x * x, axis=-1, keepdims=True)
    return x * lax.rsqrt(ms + EPS) * w


PROJ_CHUNK = 1024


def _norm_proj_kernel(x_ref, nw_ref, *refs, n_plain):
    n_w = len(refs) // 2
    h = _rms_norm(x_ref[...], nw_ref[...]).astype(BF16)
    for idx in range(n_w):
        w_ref, o_ref = refs[idx], refs[n_w + idx]
        if idx < n_plain:
            n = o_ref.shape[1]
            for c in range(0, n, PROJ_CHUNK):
                sl = slice(c, min(c + PROJ_CHUNK, n))
                o_ref[:, sl] = _dot(h, w_ref[:, sl]).astype(o_ref.dtype)
        else:
            n = o_ref.shape[0]
            for c in range(0, n, PROJ_CHUNK):
                sl = slice(c, min(c + PROJ_CHUNK, n))
                o_ref[sl, :] = lax.dot_general(w_ref[sl, :], h, (((1,), (1,)), ((), ())),
                                               preferred_element_type=F32).astype(o_ref.dtype)


def _norm_proj(x, nw, plain, transposed, tm):
    t, d = x.shape
    in_specs = [pl.BlockSpec((tm, d), lambda i: (i, 0)), pl.BlockSpec((1, d), lambda i: (0, 0))]
    out_specs, out_shape = [], []
    for w, dtype in plain:
        n = w.shape[1]
        in_specs.append(pl.BlockSpec((d, n), lambda i: (0, 0)))
        out_specs.append(pl.BlockSpec((tm, n), lambda i: (i, 0)))
        out_shape.append(jax.ShapeDtypeStruct((t, n), dtype))
    for wt, dtype in transposed:
        n = wt.shape[0]
        in_specs.append(pl.BlockSpec((n, d), lambda i: (0, 0)))
        out_specs.append(pl.BlockSpec((n, tm), lambda i: (0, i)))
        out_shape.append(jax.ShapeDtypeStruct((n, t), dtype))
    return pl.pallas_call(
        functools.partial(_norm_proj_kernel, n_plain=len(plain)),
        grid=(t // tm,),
        in_specs=in_specs,
        out_specs=out_specs,
        out_shape=out_shape,
        compiler_params=_cparams("parallel"),
        name="norm_proj",
    )(x, nw, *[w for w, _ in plain], *[w for w, _ in transposed])


SMALL_GA = 0
SMALL_GB = GDN_HEADS
SMALL_FF = 2 * GDN_HEADS
BIAS_TERMS = 3
BIAS_HEAD_STRIDE = 8


def _bias_placements():
    n_out = (FOX_HEADS // 2) * LANES
    pq = np.zeros((LANES, n_out), np.float32)
    pk = np.zeros((LANES, n_out), np.float32)
    ones_q = np.zeros((1, n_out), np.float32)
    ones_k = np.zeros((1, n_out), np.float32)
    for h in range(FOX_HEADS):
        base = (h // 2) * LANES + (h % 2) * BIAS_HEAD_STRIDE
        for t in range(BIAS_TERMS):
            pq[SMALL_FF + t * FOX_HEADS + h, base + t] = 1.0
            pk[SMALL_FF + t * FOX_HEADS + h, base + BIAS_TERMS + t] = -1.0
            ones_q[0, base + BIAS_TERMS + t] = 1.0
            ones_k[0, base + t] = 1.0
    return pq, pk, ones_q, ones_k


def _fox_bias_kernel(small_ref, fb_ref, pq_ref, pk_ref, oq_ref, ok_ref, eq_ref, ek_ref,
                     carry_ref):
    tm = small_ref.shape[0]

    @pl.when(pl.program_id(1) == 0)
    def _():
        carry_ref[...] = jnp.zeros_like(carry_ref)

    z = small_ref[...] + fb_ref[...]
    log_f = -_softplus(-z)
    row = lax.broadcasted_iota(jnp.int32, (tm, tm), 0)
    col = lax.broadcasted_iota(jnp.int32, (tm, tm), 1)
    tril = jnp.where(row >= col, 1.0, 0.0).astype(BF16)
    cum = _dot_exact_lhs(tril, log_f) + carry_ref[0:1, :]
    carry_ref[...] = jnp.broadcast_to(cum[tm - 1:tm, :], carry_ref.shape)
    lane = lax.broadcasted_iota(jnp.int32, (1, LANES), 1)
    is_ff = jnp.logical_and(lane >= SMALL_FF, lane < SMALL_FF + FOX_HEADS)
    packed = jnp.zeros_like(cum)
    for t, term in enumerate(_split3(cum)):
        part = jnp.where(is_ff, term.astype(F32), 0.0)
        packed = packed + (pltpu.roll(part, t * FOX_HEADS, axis=1) if t else part)
    terms = packed.astype(BF16)
    eq_ref[...] = (oq_ref[...] + _dot(terms, pq_ref[...])).astype(BF16)
    ek_ref[...] = (ok_ref[...] + _dot(terms, pk_ref[...])).astype(BF16)


def _fox_bias(small, fb_row, batch, seq, tm):
    t = small.shape[0]
    pq, pk, ones_q, ones_k = _bias_placements()
    n_out = pq.shape[1]
    nt = seq // tm
    const2 = lambda b, i: (0, 0)
    return pl.pallas_call(
        _fox_bias_kernel,
        grid=(batch, nt),
        in_specs=[pl.BlockSpec((tm, LANES), lambda b, i: (b * nt + i, 0)),
                  pl.BlockSpec((1, LANES), const2),
                  pl.BlockSpec((LANES, n_out), const2),
                  pl.BlockSpec((LANES, n_out), const2),
                  pl.BlockSpec((1, n_out), const2),
                  pl.BlockSpec((1, n_out), const2)],
        out_specs=[pl.BlockSpec((tm, n_out), lambda b, i: (b * nt + i, 0)),
                   pl.BlockSpec((tm, n_out), lambda b, i: (b * nt + i, 0))],
        out_shape=[jax.ShapeDtypeStruct((t, n_out), BF16),
                   jax.ShapeDtypeStruct((t, n_out), BF16)],
        scratch_shapes=[pltpu.VMEM((8, LANES), F32)],
        compiler_params=_cparams("parallel", "arbitrary"),
        name="fox_bias",
    )(small, fb_row, jnp.asarray(pq, BF16), jnp.asarray(pk, BF16),
      jnp.asarray(ones_q), jnp.asarray(ones_k))


FOX_PAIRS_PER_STEP = 4
FOX_SUM_ROWS = 16
FOX_SCORES_AHEAD = 3


def _fox_attention_kernel(q_ref, eq_ref, k_ref, ek_ref, vt_ref, fo_ref, o_ref, *, tq):
    i = pl.program_id(2)
    n_pairs = q_ref.shape[1] // LANES
    n_heads = 2 * n_pairs
    lane = lax.broadcasted_iota(jnp.int32, (1, LANES), 1)
    head_a = lane < FOX_HEAD_DIM
    bias_a = lane < BIAS_HEAD_STRIDE
    zero = jnp.zeros((), BF16)
    qm = []
    for p in range(n_pairs):
        psl = slice(p * LANES, (p + 1) * LANES)
        q = q_ref[:, psl] * jnp.asarray(FOX_HEAD_DIM ** -0.5, BF16)
        eq = eq_ref[:, psl]
        qm.append(jnp.concatenate([jnp.where(head_a, q, zero), jnp.where(bias_a, eq, zero)], axis=1))
        qm.append(jnp.concatenate([jnp.where(head_a, zero, q), jnp.where(bias_a, zero, eq)], axis=1))

    def tile(j, carry, masked):
        start = pl.multiple_of(j * tq, tq)

        def score(idx):
            psl = slice((idx // 2) * LANES, (idx // 2 + 1) * LANES)
            kk = jnp.concatenate([k_ref[pl.ds(start, tq), psl], ek_ref[pl.ds(start, tq), psl]], axis=1)
            return lax.dot_general(kk, qm[idx], (((1,), (1,)), ((), ())),
                                   preferred_element_type=F32)

        def value(idx):
            vt = vt_ref[idx * FOX_HEAD_DIM:(idx + 1) * FOX_HEAD_DIM, pl.ds(start, tq)]
            return jnp.concatenate([vt, jnp.ones((FOX_SUM_ROWS, tq), BF16)], axis=0)

        out = []
        ahead = [score(idx) for idx in range(min(FOX_SCORES_AHEAD, n_heads))]
        for idx in range(n_heads):
            s = ahead.pop(0)
            if idx + FOX_SCORES_AHEAD < n_heads:
                ahead.append(score(idx + FOX_SCORES_AHEAD))
            m, acc = carry[idx]
            if masked:
                key = lax.broadcasted_iota(jnp.int32, (tq, tq), 0)
                qry = lax.broadcasted_iota(jnp.int32, (tq, tq), 1)
                s = jnp.where(key <= qry, s, NEG_BIG)
            m_new = jnp.maximum(m, jnp.max(s, axis=0, keepdims=True))
            alpha = jnp.exp(m - m_new)
            prob = jnp.exp(s - m_new).astype(BF16)
            acc = alpha * acc + _dot(value(idx), prob)
            out.append((m_new, acc))
        return tuple(out)

    init = tuple((jnp.full((1, tq), NEG_BIG, F32), jnp.zeros((FOX_HEAD_DIM + FOX_SUM_ROWS, tq), F32))
                 for _ in range(n_heads))
    carry = lax.fori_loop(0, i, lambda j, c: tile(j, c, False), init)
    final = tile(i, carry, True)
    for p in range(n_pairs):
        psl = slice(p * LANES, (p + 1) * LANES)
        heads = []
        for h in range(2):
            acc = final[2 * p + h][1]
            heads.append(acc[:FOX_HEAD_DIM] / acc[FOX_HEAD_DIM:FOX_HEAD_DIM + 1])
        gate = _sigmoid(fo_ref[:, psl].astype(F32))
        o_ref[:, psl] = (jnp.concatenate(heads, axis=0).T * gate).astype(o_ref.dtype)


def _fox_attention(fox, vt, eq, ek, batch, seq, tq):
    t = fox.shape[0]
    pairs = FOX_HEADS // 2
    steps = pairs // FOX_PAIRS_PER_STEP
    width = FOX_PAIRS_PER_STEP * LANES
    nq = seq // tq
    return pl.pallas_call(
        functools.partial(_fox_attention_kernel, tq=tq),
        grid=(batch, steps, nq),
        in_specs=[pl.BlockSpec((tq, width), lambda b, p, i: (b * nq + i, p)),
                  pl.BlockSpec((tq, width), lambda b, p, i: (b * nq + i, p)),
                  pl.BlockSpec((seq, width), lambda b, p, i: (b, steps + p)),
                  pl.BlockSpec((seq, width), lambda b, p, i: (b, p)),
                  pl.BlockSpec((width, seq), lambda b, p, i: (p, b)),
                  pl.BlockSpec((tq, width), lambda b, p, i: (b * nq + i, 2 * steps + p))],
        out_specs=pl.BlockSpec((tq, width), lambda b, p, i: (b * nq + i, p)),
        out_shape=jax.ShapeDtypeStruct((t, pairs * LANES), BF16),
        compiler_params=_cparams("parallel", "parallel", "arbitrary"),
        name="fox_attention",
    )(fox, eq, fox, ek, vt, fox)


GDN_WIDTH = GDN_HEADS * GDN_HEAD_DIM
GDN_GROUP = 4
GDN_CAT = GDN_HEADS * GDN_CHUNK
GDN_PAIRS = GDN_HEADS // 2
GDN_PAIR = 2 * GDN_HEAD_DIM
GDN_PREP_CHUNKS = 4
CONV_HALO = 8


def _gdn_expanders():
    e_g128 = np.zeros((LANES, GDN_WIDTH), np.float32)
    e_b128 = np.zeros((LANES, GDN_WIDTH), np.float32)
    e_g64 = np.zeros((LANES, GDN_CAT), np.float32)
    for h in range(GDN_HEADS):
        e_g128[SMALL_GA + h, h * GDN_HEAD_DIM:(h + 1) * GDN_HEAD_DIM] = 1.0
        e_b128[SMALL_GB + h, h * GDN_HEAD_DIM:(h + 1) * GDN_HEAD_DIM] = 1.0
        e_g64[SMALL_GA + h, h * GDN_CHUNK:(h + 1) * GDN_CHUNK] = 1.0
    return e_g128, e_b128, e_g64


def _block_diag(x, n_blocks):
    r, total = x.shape
    w = total // n_blocks
    tile_w = max(w, LANES)
    per_tile = tile_w // w
    zeros = jnp.zeros((r, tile_w), x.dtype)
    lane_block = _div_pow2(lax.broadcasted_iota(jnp.int32, (1, tile_w), 1), w)
    rows = []
    for h in range(n_blocks):
        t = h // per_tile
        tile = x[:, t * tile_w:(t + 1) * tile_w]
        if per_tile > 1:
            tile = tile * jnp.where(lane_block == h % per_tile, 1.0, 0.0).astype(x.dtype)
        rows.append(jnp.concatenate([tile if i == t else zeros for i in range(total // tile_w)], axis=1))
    return jnp.concatenate(rows, axis=0)


def _rows(x, i, n):
    return x[i * n:(i + 1) * n]


def _headwise_products(lhs, b):
    c = b.shape[0]
    n = len(lhs)
    b_hi, b_lo = _split2(b)
    bd_hi = _block_diag(b_hi, GDN_GROUP)
    bd_lo = _block_diag(b_lo, GDN_GROUP)
    parts = [_split2(x) for x in lhs]
    his = [p[0] for p in parts]
    los = [p[1] for p in parts]
    top = _dot(jnp.concatenate(his + los, axis=0), bd_hi)
    bot = _dot(jnp.concatenate(his, axis=0), bd_lo)
    return [_rows(top, i, c) + _rows(top, n + i, c) + _rows(bot, i, c) for i in range(n)]


def _unit_lower_inverses(l_cats):
    c, n = l_cats[0].shape
    r = lax.broadcasted_iota(jnp.int32, (c, n), 0)
    j = _mod_pow2(lax.broadcasted_iota(jnp.int32, (c, n), 1), c)
    eye = jnp.where(r == j, 1.0, 0.0)
    ss = [eye - l for l in l_cats]
    ps = [_headwise_products([l], l)[0] for l in l_cats]
    k = 2
    while 2 * k < c:
        stage = [_headwise_products([p, s], p) for p, s in zip(ps, ss)]
        ps = [st[0] for st in stage]
        ss = [s + st[1] for s, st in zip(ss, stage)]
        k *= 2
    return [s + _headwise_products([s], p)[0] for p, s in zip(ps, ss)]


def _gdn_prep_kernel(q_ref, k_ref, v_ref, hq_ref, hk_ref, hv_ref, small_ref, cw_ref, alog_ref,
                     dtb_ref, eg128_ref, eb128_ref, eg64_ref,
                     u_ref, w_ref, qg_ref, kd_ref, qk_ref, gl_ref, *, blocks_per_seq):
    c = GDN_CHUNK
    n_conv = cw_ref.shape[1]
    seq_start = lax.rem(pl.program_id(0), blocks_per_seq) == 0

    sublane = lax.broadcasted_iota(jnp.int32, (1, CONV_HALO, 1), 1)
    conv = []
    for s, (ref, halo_ref) in enumerate(((q_ref, hq_ref), (k_ref, hk_ref), (v_ref, hv_ref))):
        x = ref[...]
        halo = jnp.where(seq_start, 0.0, halo_ref[...])
        groups = x.reshape(-1, CONV_HALO, x.shape[1])
        prev = jnp.concatenate([halo[None], groups[:-1]], axis=0)
        acc = x * cw_ref[s, n_conv - 1:n_conv, :]
        for shift in range(1, n_conv):
            mixed = jnp.where(sublane >= CONV_HALO - shift, prev, groups)
            moved = pltpu.roll(mixed, shift, axis=1).reshape(x.shape)
            acc = acc + moved * cw_ref[s, n_conv - 1 - shift:n_conv - shift, :]
        conv.append(_silu(acc))
    cq, ck, cv = conv
    rows = cq.shape[0]

    def l2n(x):
        parts = []
        for h in range(GDN_HEADS):
            xh = x[:, h * GDN_HEAD_DIM:(h + 1) * GDN_HEAD_DIM]
            ss = jnp.sum(xh * xh, axis=-1, keepdims=True)
            parts.append(xh * lax.rsqrt(ss + EPS))
        return jnp.concatenate(parts, axis=1)

    qn = l2n(cq)
    kn = l2n(ck)

    small = small_ref[...]
    g_tok = -jnp.exp(alog_ref[...]) * _softplus(small + dtb_ref[...])
    beta_tok = _sigmoid(small)
    row = lax.broadcasted_iota(jnp.int32, (rows, rows), 0)
    col = lax.broadcasted_iota(jnp.int32, (rows, rows), 1)
    same_chunk = _div_pow2(row, c) == _div_pow2(col, c)
    tril = jnp.where(row >= col, jnp.where(same_chunk, 1.0, 0.0), 0.0).astype(BF16)
    gc_tok = _dot_exact_lhs(tril, g_tok)
    gc128 = _dot_exact_rhs(gc_tok, eg128_ref[...])
    gc64 = _dot_exact_rhs(gc_tok, eg64_ref[...])
    beta128 = _dot_exact_rhs(beta_tok, eb128_ref[...])

    exp_gc = jnp.exp(gc128)
    kb = kn * beta128
    vb = cv * beta128
    kbg = kb * exp_gc
    qs = qn * (GDN_HEAD_DIM ** -0.5)
    qg_ref[...] = (qs * exp_gc).astype(BF16)
    kb16 = kb.astype(BF16)
    qs16 = qs.astype(BF16)
    kn16 = kn.astype(BF16)

    r_cat = lax.broadcasted_iota(jnp.int32, (c, GDN_CAT), 0)
    j_cat = _mod_pow2(lax.broadcasted_iota(jnp.int32, (c, GDN_CAT), 1), c)
    tri_cat = r_cat >= j_cat
    strict_cat = r_cat > j_cat
    group_w = GDN_GROUP * GDN_HEAD_DIM
    group_c = GDN_GROUP * c
    contract_last = (((1,), (1,)), ((), ()))

    n_chunks = rows // c
    n_groups = GDN_HEADS // GDN_GROUP
    l_cats = []
    for ch in range(n_chunks):
        rs = slice(ch * c, (ch + 1) * c)
        gc64_c = gc64[rs]
        gc_row = jnp.sum(jnp.where(r_cat == j_cat, gc64_c, 0.0), axis=0, keepdims=True)
        decay = jnp.where(tri_cat, jnp.exp(jnp.where(tri_cat, gc64_c - gc_row, 0.0)), 0.0)
        g_last = gc128[ch * c + c - 1:ch * c + c, :]
        gl_ref[ch] = jnp.exp(g_last)
        kd_ref[rs, :] = (kn[rs] * jnp.exp(g_last - gc128[rs])).astype(BF16)

        for gidx in range(n_groups):
            ksl = slice(gidx * group_w, (gidx + 1) * group_w)
            csl = slice(gidx * group_c, (gidx + 1) * group_c)
            bd_k = _block_diag(kn16[rs, ksl], GDN_GROUP)
            both = lax.dot_general(jnp.concatenate([kb16[rs, ksl], qs16[rs, ksl]], axis=0), bd_k,
                                   contract_last, preferred_element_type=F32)
            dec = decay[:, csl]
            l_cats.append(jnp.where(strict_cat[:, csl], both[:c] * dec, 0.0))
            qk_ref[rs, csl] = (both[c:] * dec).astype(BF16)

    invs = _unit_lower_inverses(l_cats)

    for ch in range(n_chunks):
        rs = slice(ch * c, (ch + 1) * c)
        inv_cat = jnp.concatenate(invs[ch * n_groups:(ch + 1) * n_groups], axis=1)
        for p in range(GDN_PAIRS):
            wsl = slice(p * GDN_PAIR, (p + 1) * GDN_PAIR)
            inv_p = inv_cat[:, p * 2 * c:(p + 1) * 2 * c]
            vb_p, kbg_p = vb[rs, wsl], kbg[rs, wsl]
            rhs = jnp.concatenate([_block_diag(vb_p, 2), _block_diag(kbg_p, 2)], axis=1)
            i_hi, i_lo = _split2(inv_p)
            r_hi, r_lo = _split2(rhs)
            top = _dot(jnp.concatenate([i_hi, i_lo], axis=0), r_hi)
            sol = top[:c] + top[c:] + _dot(i_hi, r_lo)
            u_ref[rs, wsl] = sol[:, :GDN_PAIR]
            w_ref[rs, wsl] = sol[:, GDN_PAIR:].astype(BF16)


def _gdn_scan_kernel(u_ref, w_ref, qg_ref, kd_ref, qk_ref, gl_ref, z_ref, nw_ref, o_ref, state_ref):
    c = GDN_CHUNK

    @pl.when(pl.program_id(0) == 0)
    def _():
        state_ref[...] = jnp.zeros_like(state_ref)

    nw = nw_ref[...]
    zblock = jnp.zeros((GDN_HEAD_DIM, GDN_HEAD_DIM), BF16)
    hsl = lambda h: slice(h * GDN_HEAD_DIM, (h + 1) * GDN_HEAD_DIM)
    wsl = lambda p: slice(p * GDN_PAIR, (p + 1) * GDN_PAIR)
    cells = [(b, p) for b in range(u_ref.shape[0]) for p in range(GDN_PAIRS)]
    states = [(state_ref[b, 2 * p], state_ref[b, 2 * p + 1]) for b, p in cells]
    boths = []
    for (b, p), (sa, sb) in zip(cells, states):
        s_bd = jnp.concatenate([jnp.concatenate([sa.astype(BF16), zblock], axis=1),
                                jnp.concatenate([zblock, sb.astype(BF16)], axis=1)], axis=0)
        boths.append(_dot(jnp.concatenate([w_ref[b, :, wsl(p)], qg_ref[b, :, wsl(p)]], axis=0), s_bd))
    v16s = [(u_ref[b, :, wsl(p)] - both[:c]).astype(BF16) for (b, p), both in zip(cells, boths)]
    outs, upds = [], []
    for (b, p), both, v16 in zip(cells, boths, v16s):
        v_bd = _block_diag(v16, 2)
        outs.append(both[c:] + _dot(qk_ref[b, :, p * 2 * c:(p + 1) * 2 * c], v_bd))
        upds.append([lax.dot_general(kd_ref[b, :, hsl(2 * p + h)], v16[:, hsl(h)],
                                     (((0,), (0,)), ((), ())), preferred_element_type=F32)
                     for h in range(2)])
    for (b, p), st, o, upd in zip(cells, states, outs, upds):
        halves = []
        for h in range(2):
            head = 2 * p + h
            state_ref[b, head] = st[h] * gl_ref[b, 0][:, hsl(head)] + upd[h]
            oh = o[:, hsl(h)]
            ms = jnp.mean(oh * oh, axis=-1, keepdims=True)
            halves.append(oh * lax.rsqrt(ms + EPS) * nw)
        o_ref[b, :, wsl(p)] = (jnp.concatenate(halves, axis=1)
                               * _silu(z_ref[b, :, wsl(p)])).astype(o_ref.dtype)


def _gdn(gdn, small, conv_w3, alog_row, dtb_row, nw_row, batch, seq):
    t = gdn.shape[0]
    c = GDN_CHUNK
    w = GDN_WIDTH
    rows = GDN_PREP_CHUNKS * c
    e_g128, e_b128, e_g64 = _gdn_expanders()
    n_conv = conv_w3.shape[1]
    tok = lambda col: (lambda i: (i, col))
    halo = lambda col: (lambda i: (jnp.maximum(i * (rows // CONV_HALO) - 1, 0), col))
    const2 = lambda i: (0, 0)
    u, wk, qg, kd, qk, gl = pl.pallas_call(
        functools.partial(_gdn_prep_kernel, blocks_per_seq=seq // rows),
        grid=(t // rows,),
        in_specs=[pl.BlockSpec((rows, w), tok(0)),
                  pl.BlockSpec((rows, w), tok(1)),
                  pl.BlockSpec((rows, w), tok(2)),
                  pl.BlockSpec((CONV_HALO, w), halo(0)),
                  pl.BlockSpec((CONV_HALO, w), halo(1)),
                  pl.BlockSpec((CONV_HALO, w), halo(2)),
                  pl.BlockSpec((rows, LANES), tok(0)),
                  pl.BlockSpec((3, n_conv, w), lambda i: (0, 0, 0)),
                  pl.BlockSpec((1, LANES), const2),
                  pl.BlockSpec((1, LANES), const2),
                  pl.BlockSpec((LANES, w), const2),
                  pl.BlockSpec((LANES, w), const2),
                  pl.BlockSpec((LANES, GDN_CAT), const2)],
        out_specs=[pl.BlockSpec((rows, w), tok(0)),
                   pl.BlockSpec((rows, w), tok(0)),
                   pl.BlockSpec((rows, w), tok(0)),
                   pl.BlockSpec((rows, w), tok(0)),
                   pl.BlockSpec((rows, GDN_CAT), tok(0)),
                   pl.BlockSpec((GDN_PREP_CHUNKS, 1, w), lambda i: (i, 0, 0))],
        out_shape=[jax.ShapeDtypeStruct((t, w), F32),
                   jax.ShapeDtypeStruct((t, w), BF16),
                   jax.ShapeDtypeStruct((t, w), BF16),
                   jax.ShapeDtypeStruct((t, w), BF16),
                   jax.ShapeDtypeStruct((t, GDN_CAT), BF16),
                   jax.ShapeDtypeStruct((t // c, 1, w), F32)],
        compiler_params=_cparams("parallel"),
        name="gdn_prep",
    )(gdn, gdn, gdn, gdn, gdn, gdn, small, conv_w3, alog_row, dtb_row,
      jnp.asarray(e_g128, BF16), jnp.asarray(e_b128, BF16), jnp.asarray(e_g64, BF16))

    seq3 = lambda a: a.reshape(batch, seq, a.shape[-1])
    blk = lambda width, col=0: pl.BlockSpec((batch, c, width), lambda n: (0, n, col))
    o = pl.pallas_call(
        _gdn_scan_kernel,
        grid=(seq // c,),
        in_specs=[blk(w), blk(w), blk(w), blk(w), blk(GDN_CAT),
                  pl.BlockSpec((batch, 1, 1, w), lambda n: (0, n, 0, 0)),
                  blk(w, 3),
                  pl.BlockSpec((1, GDN_HEAD_DIM), lambda n: (0, 0))],
        out_specs=blk(w),
        out_shape=jax.ShapeDtypeStruct((batch, seq, w), BF16),
        scratch_shapes=[pltpu.VMEM((batch, GDN_HEADS, GDN_HEAD_DIM, GDN_HEAD_DIM), F32)],
        compiler_params=_cparams("arbitrary"),
        name="gdn_scan",
    )(seq3(u), seq3(wk), seq3(qg), seq3(kd), seq3(qk), gl.reshape(batch, seq // c, 1, w),
      seq3(gdn), nw_row)
    return o.reshape(t, w)


def _merge_out_kernel(x_ref, oa_ref, ob_ref, ga_ref, gb_ref, wa_ref, wb_ref, wo_ref, o_ref):
    ya = _dot(oa_ref[...], wa_ref[...])
    yb = _dot(ob_ref[...], wb_ref[...])
    y = (_sigmoid(ga_ref[...].astype(F32)) * ya + _sigmoid(gb_ref[...].astype(F32)) * yb)
    o_ref[...] = x_ref[...] + _dot(y.astype(BF16), wo_ref[...])


def _merge_out(x, o_a, o_b, mg, w_a, w_b, w_o, tm):
    t, d = x.shape
    tokd = pl.BlockSpec((tm, d), lambda i: (i, 0))
    wspec = pl.BlockSpec((d, d), lambda i: (0, 0))
    return pl.pallas_call(
        _merge_out_kernel,
        grid=(t // tm,),
        in_specs=[tokd, tokd, tokd,
                  pl.BlockSpec((tm, d), lambda i: (i, 0)),
                  pl.BlockSpec((tm, d), lambda i: (i, 1)),
                  wspec, wspec, wspec],
        out_specs=tokd,
        out_shape=jax.ShapeDtypeStruct((t, d), F32),
        compiler_params=_cparams("parallel"),
        name="merge_out",
    )(x, o_a, o_b, mg, mg, w_a, w_b, w_o)


def _ffn_kernel(x_ref, nw_ref, wg_ref, wu_ref, wd_ref, fw_ref, o_ref, *, final_norm):
    x = x_ref[...]
    h = _rms_norm(x, nw_ref[...]).astype(BF16)
    a = (_silu(_dot(h, wg_ref[...])) * _dot(h, wu_ref[...])).astype(BF16)
    y = x + _dot(a, wd_ref[...])
    if final_norm:
        y = _rms_norm(y, fw_ref[...])
    o_ref[...] = y


def _ffn(x, nw, w_g, w_u, w_d, fw, final_norm, tm):
    t, d = x.shape
    f = w_g.shape[1]
    tokd = pl.BlockSpec((tm, d), lambda i: (i, 0))
    rowd = pl.BlockSpec((1, d), lambda i: (0, 0))
    return pl.pallas_call(
        functools.partial(_ffn_kernel, final_norm=final_norm),
        grid=(t // tm,),
        in_specs=[tokd, rowd,
                  pl.BlockSpec((d, f), lambda i: (0, 0), pipeline_mode=pl.Buffered(1)),
                  pl.BlockSpec((d, f), lambda i: (0, 0), pipeline_mode=pl.Buffered(1)),
                  pl.BlockSpec((f, d), lambda i: (0, 0), pipeline_mode=pl.Buffered(1)),
                  rowd],
        out_specs=tokd,
        out_shape=jax.ShapeDtypeStruct((t, d), F32),
        compiler_params=_cparams("parallel"),
        name="ffn",
    )(x, nw, w_g, w_u, w_d, fw)


def _pad_row(v, offset):
    return jnp.zeros((1, LANES), F32).at[0, offset:offset + v.shape[0]].set(v.astype(F32))


def _layer(x, batch, seq, norm_mix_w, w_in, conv_w, a_log, dt_bias, gdn_norm_w, fox_f_bias,
           w_branch_a, w_branch_b, w_out, norm_ffn_w, w_gate, w_up, w_down, final_w, final_norm):
    d = x.shape[1]
    gw, fw = GDN_WIDTH, FOX_HEADS * FOX_HEAD_DIM
    sizes = (gw, gw, gw, gw, GDN_HEADS, GDN_HEADS, fw, fw, fw, FOX_HEADS, fw, d, d)
    offs = np.concatenate([[0], np.cumsum(sizes)])
    col = lambda i: w_in[:, offs[i]:offs[i + 1]]
    w_gdn = jnp.concatenate([col(0), col(1), col(2), col(3)], axis=1).astype(BF16)
    w_fox = jnp.concatenate([col(6), col(7), col(10)], axis=1).astype(BF16)
    w_fv_t = col(8).T.astype(BF16)
    w_mg = jnp.concatenate([col(11), col(12)], axis=1).astype(BF16)
    n_small = 2 * GDN_HEADS + FOX_HEADS
    w_small = jnp.concatenate([col(4), col(5), col(9), jnp.zeros((d, LANES - n_small), F32)],
                              axis=1).astype(BF16)
    nw = norm_mix_w.reshape(1, d)

    tm = min(seq, TOKEN_TILE)
    (gdn,) = _norm_proj(x, nw, [(w_gdn, F32)], [], tm)
    fox, vt = _norm_proj(x, nw, [(w_fox, BF16)], [(w_fv_t, BF16)], tm)
    mg, small = _norm_proj(x, nw, [(w_mg, BF16), (w_small, F32)], [], tm)

    eq, ek = _fox_bias(small, _pad_row(fox_f_bias, SMALL_FF), batch, seq, tm)
    o_b = _fox_attention(fox, vt, eq, ek, batch, seq, tq=tm)

    conv_w3 = conv_w.reshape(conv_w.shape[0], 3, gw).transpose(1, 0, 2)
    o_a = _gdn(gdn, small, conv_w3, _pad_row(a_log, SMALL_GA), _pad_row(dt_bias, SMALL_GA),
               gdn_norm_w.reshape(1, GDN_HEAD_DIM), batch, seq)

    x1 = _merge_out(x, o_a, o_b, mg, w_branch_a.astype(BF16), w_branch_b.astype(BF16),
                    w_out.astype(BF16), tm)
    return _ffn(x1, norm_ffn_w.reshape(1, d), w_gate.astype(BF16), w_up.astype(BF16),
                w_down.astype(BF16), final_w.reshape(1, d), final_norm, tm)


def kernel(x, norm_mix_w, w_in, conv_w, a_log, dt_bias, gdn_norm_w, fox_f_bias, w_branch_a,
           w_branch_b, w_out, norm_ffn_w, w_gate, w_up, w_down, norm_final_w):
    batch, seq, d = x.shape
    depth = w_in.shape[0]
    h = x.reshape(batch * seq, d)
    for l in range(depth):
        h = _layer(h, batch, seq, norm_mix_w[l], w_in[l], conv_w[l], a_log[l], dt_bias[l],
                   gdn_norm_w[l], fox_f_bias[l], w_branch_a[l], w_branch_b[l], w_out[l],
                   norm_ffn_w[l], w_gate[l], w_up[l], w_down[l], norm_final_w,
                   final_norm=(l == depth - 1))
    return h.reshape(batch, seq, d)
```

```python
import functools

import jax
import jax.numpy as jnp
import numpy as np
from jax import lax
from jax.experimental import pallas as pl
from jax.experimental.pallas import tpu as pltpu

F32 = jnp.float32
BF16 = jnp.bfloat16

EPS = 1e-6
GDN_HEADS = 8
GDN_HEAD_DIM = 128
GDN_CHUNK = 64
FOX_HEADS = 16
FOX_HEAD_DIM = 64
LANES = 128
VMEM_LIMIT_BYTES = 56 * 1024 * 1024
TOKEN_TILE = 512
NEG_BIG = -1e30


def _cparams(*semantics):
    return pltpu.CompilerParams(dimension_semantics=semantics,
                                vmem_limit_bytes=VMEM_LIMIT_BYTES)


def _split2(x):
    hi = x.astype(BF16)
    lo = (x - hi.astype(F32)).astype(BF16)
    return hi, lo


def _split3(x):
    hi = x.astype(BF16)
    r = x - hi.astype(F32)
    mid = r.astype(BF16)
    lo = (r - mid.astype(F32)).astype(BF16)
    return hi, mid, lo


def _dot(a, b):
    return jnp.dot(a, b, preferred_element_type=F32)


def _dot_exact_rhs(x, m_bf16):
    n = x.shape[0]
    y = _dot(jnp.concatenate(_split3(x), axis=0), m_bf16)
    return y[:n] + y[n:2 * n] + y[2 * n:]


def _dot_exact_lhs(m_bf16, x):
    n = x.shape[1]
    y = _dot(m_bf16, jnp.concatenate(_split3(x), axis=1))
    return y[:, :n] + y[:, n:2 * n] + y[:, 2 * n:]


def _div_pow2(x, n):
    assert n & (n - 1) == 0
    return jnp.right_shift(x, n.bit_length() - 1)


def _mod_pow2(x, n):
    assert n & (n - 1) == 0
    return jnp.bitwise_and(x, n - 1)


def _softplus(y):
    return jnp.maximum(y, 0.0) + jnp.log(1.0 + jnp.exp(-jnp.abs(y)))


def _sigmoid(y):
    return 0.5 * jnp.tanh(0.5 * y) + 0.5


def _silu(y):
    half = 0.5 * y
    return half + half * jnp.tanh(half)


def _rms_norm(x, w):
    ms = jnp.mean(x * x, axis=-1, keepdims=True)
    return x * lax.rsqrt(ms + EPS) * w


PROJ_CHUNK = 1024


def _norm_proj_kernel(x_ref, nw_ref, *refs, n_plain):
    n_w = len(refs) // 2
    h = _rms_norm(x_ref[...], nw_ref[...]).astype(BF16)
    for idx in range(n_w):
        w_ref, o_ref = refs[idx], refs[n_w + idx]
        if idx < n_plain:
            n = o_ref.shape[1]
            for c in range(0, n, PROJ_CHUNK):
                sl = slice(c, min(c + PROJ_CHUNK, n))
                o_ref[:, sl] = _dot(h, w_ref[:, sl]).astype(o_ref.dtype)
        else:
            n = o_ref.shape[0]
            for c in range(0, n, PROJ_CHUNK):
                sl = slice(c, min(c + PROJ_CHUNK, n))
                o_ref[sl, :] = lax.dot_general(w_ref[sl, :], h, (((1,), (1,)), ((), ())),
                                               preferred_element_type=F32).astype(o_ref.dtype)


def _norm_proj(x, nw, plain, transposed, tm):
    t, d = x.shape
    in_specs = [pl.BlockSpec((tm, d), lambda i: (i, 0)), pl.BlockSpec((1, d), lambda i: (0, 0))]
    out_specs, out_shape = [], []
    for w, dtype in plain:
        n = w.shape[1]
        in_specs.append(pl.BlockSpec((d, n), lambda i: (0, 0)))
        out_specs.append(pl.BlockSpec((tm, n), lambda i: (i, 0)))
        out_shape.append(jax.ShapeDtypeStruct((t, n), dtype))
    for wt, dtype in transposed:
        n = wt.shape[0]
        in_specs.append(pl.BlockSpec((n, d), lambda i: (0, 0)))
        out_specs.append(pl.BlockSpec((n, tm), lambda i: (0, i)))
        out_shape.append(jax.ShapeDtypeStruct((n, t), dtype))
    return pl.pallas_call(
        functools.partial(_norm_proj_kernel, n_plain=len(plain)),
        grid=(t // tm,),
        in_specs=in_specs,
        out_specs=out_specs,
        out_shape=out_shape,
        compiler_params=_cparams("parallel"),
        name="norm_proj",
    )(x, nw, *[w for w, _ in plain], *[w for w, _ in transposed])


SMALL_GA = 0
SMALL_GB = GDN_HEADS
SMALL_FF = 2 * GDN_HEADS
BIAS_TERMS = 3
BIAS_HEAD_STRIDE = 8


def _bias_placements():
    n_out = (FOX_HEADS // 2) * LANES
    pq = np.zeros((LANES, n_out), np.float32)
    pk = np.zeros((LANES, n_out), np.float32)
    ones_q = np.zeros((1, n_out), np.float32)
    ones_k = np.zeros((1, n_out), np.float32)
    for h in range(FOX_HEADS):
        base = (h // 2) * LANES + (h % 2) * BIAS_HEAD_STRIDE
        for t in range(BIAS_TERMS):
            pq[SMALL_FF + t * FOX_HEADS + h, base + t] = 1.0
            pk[SMALL_FF + t * FOX_HEADS + h, base + BIAS_TERMS + t] = -1.0
            ones_q[0, base + BIAS_TERMS + t] = 1.0
            ones_k[0, base + t] = 1.0
    return pq, pk, ones_q, ones_k


def _fox_bias_kernel(small_ref, fb_ref, pq_ref, pk_ref, oq_ref, ok_ref, eq_ref, ek_ref,
                     carry_ref):
    tm = small_ref.shape[0]

    @pl.when(pl.program_id(1) == 0)
    def _():
        carry_ref[...] = jnp.zeros_like(carry_ref)

    z = small_ref[...] + fb_ref[...]
    log_f = -_softplus(-z)
    row = lax.broadcasted_iota(jnp.int32, (tm, tm), 0)
    col = lax.broadcasted_iota(jnp.int32, (tm, tm), 1)
    tril = jnp.where(row >= col, 1.0, 0.0).astype(BF16)
    cum = _dot_exact_lhs(tril, log_f) + carry_ref[0:1, :]
    carry_ref[...] = jnp.broadcast_to(cum[tm - 1:tm, :], carry_ref.shape)
    lane = lax.broadcasted_iota(jnp.int32, (1, LANES), 1)
    is_ff = jnp.logical_and(lane >= SMALL_FF, lane < SMALL_FF + FOX_HEADS)
    packed = jnp.zeros_like(cum)
    for t, term in enumerate(_split3(cum)):
        part = jnp.where(is_ff, term.astype(F32), 0.0)
        packed = packed + (pltpu.roll(part, t * FOX_HEADS, axis=1) if t else part)
    terms = packed.astype(BF16)
    eq_ref[...] = (oq_ref[...] + _dot(terms, pq_ref[...])).astype(BF16)
    ek_ref[...] = (ok_ref[...] + _dot(terms, pk_ref[...])).astype(BF16)


def _fox_bias(small, fb_row, batch, seq, tm):
    t = small.shape[0]
    pq, pk, ones_q, ones_k = _bias_placements()
    n_out = pq.shape[1]
    nt = seq // tm
    const2 = lambda b, i: (0, 0)
    return pl.pallas_call(
        _fox_bias_kernel,
        grid=(batch, nt),
        in_specs=[pl.BlockSpec((tm, LANES), lambda b, i: (b * nt + i, 0)),
                  pl.BlockSpec((1, LANES), const2),
                  pl.BlockSpec((LANES, n_out), const2),
                  pl.BlockSpec((LANES, n_out), const2),
                  pl.BlockSpec((1, n_out), const2),
                  pl.BlockSpec((1, n_out), const2)],
        out_specs=[pl.BlockSpec((tm, n_out), lambda b, i: (b * nt + i, 0)),
                   pl.BlockSpec((tm, n_out), lambda b, i: (b * nt + i, 0))],
        out_shape=[jax.ShapeDtypeStruct((t, n_out), BF16),
                   jax.ShapeDtypeStruct((t, n_out), BF16)],
        scratch_shapes=[pltpu.VMEM((8, LANES), F32)],
        compiler_params=_cparams("parallel", "arbitrary"),
        name="fox_bias",
    )(small, fb_row, jnp.asarray(pq, BF16), jnp.asarray(pk, BF16),
      jnp.asarray(ones_q), jnp.asarray(ones_k))


FOX_PAIRS_PER_STEP = 4
FOX_SUM_ROWS = 16
FOX_SCORES_AHEAD = 3


def _fox_attention_kernel(q_ref, eq_ref, k_ref, ek_ref, vt_ref, fo_ref, o_ref, *, tq):
    i = pl.program_id(2)
    n_pairs = q_ref.shape[1] // LANES
    n_heads = 2 * n_pairs
    lane = lax.broadcasted_iota(jnp.int32, (1, LANES), 1)
    head_a = lane < FOX_HEAD_DIM
    bias_a = lane < BIAS_HEAD_STRIDE
    zero = jnp.zeros((), BF16)
    qm = []
    for p in range(n_pairs):
        psl = slice(p * LANES, (p + 1) * LANES)
        q = q_ref[:, psl] * jnp.asarray(FOX_HEAD_DIM ** -0.5, BF16)
        eq = eq_ref[:, psl]
        qm.append(jnp.concatenate([jnp.where(head_a, q, zero), jnp.where(bias_a, eq, zero)], axis=1))
        qm.append(jnp.concatenate([jnp.where(head_a, zero, q), jnp.where(bias_a, zero, eq)], axis=1))

    def tile(j, carry, masked):
        start = pl.multiple_of(j * tq, tq)

        def score(idx):
            psl = slice((idx // 2) * LANES, (idx // 2 + 1) * LANES)
            kk = jnp.concatenate([k_ref[pl.ds(start, tq), psl], ek_ref[pl.ds(start, tq), psl]], axis=1)
            return lax.dot_general(kk, qm[idx], (((1,), (1,)), ((), ())),
                                   preferred_element_type=F32)

        def value(idx):
            vt = vt_ref[idx * FOX_HEAD_DIM:(idx + 1) * FOX_HEAD_DIM, pl.ds(start, tq)]
            return jnp.concatenate([vt, jnp.ones((FOX_SUM_ROWS, tq), BF16)], axis=0)

        out = []
        ahead = [score(idx) for idx in range(min(FOX_SCORES_AHEAD, n_heads))]
        for idx in range(n_heads):
            s = ahead.pop(0)
            if idx + FOX_SCORES_AHEAD < n_heads:
                ahead.append(score(idx + FOX_SCORES_AHEAD))
            m, acc = carry[idx]
            if masked:
                key = lax.broadcasted_iota(jnp.int32, (tq, tq), 0)
                qry = lax.broadcasted_iota(jnp.int32, (tq, tq), 1)
                s = jnp.where(key <= qry, s, NEG_BIG)
            m_new = jnp.maximum(m, jnp.max(s, axis=0, keepdims=True))
            alpha = jnp.exp(m - m_new)
            prob = jnp.exp(s - m_new).astype(BF16)
            acc = alpha * acc + _dot(value(idx), prob)
            out.append((m_new, acc))
        return tuple(out)

    init = tuple((jnp.full((1, tq), NEG_BIG, F32), jnp.zeros((FOX_HEAD_DIM + FOX_SUM_ROWS, tq), F32))
                 for _ in range(n_heads))
    carry = lax.fori_loop(0, i, lambda j, c: tile(j, c, False), init)
    final = tile(i, carry, True)
    for p in range(n_pairs):
        psl = slice(p * LANES, (p + 1) * LANES)
        heads = []
        for h in range(2):
            acc = final[2 * p + h][1]
            heads.append(acc[:FOX_HEAD_DIM] / acc[FOX_HEAD_DIM:FOX_HEAD_DIM + 1])
        gate = _sigmoid(fo_ref[:, psl].astype(F32))
        o_ref[:, psl] = (jnp.concatenate(heads, axis=0).T * gate).astype(o_ref.dtype)


def _fox_attention(fox, vt, eq, ek, batch, seq, tq):
    t = fox.shape[0]
    pairs = FOX_HEADS // 2
    steps = pairs // FOX_PAIRS_PER_STEP
    width = FOX_PAIRS_PER_STEP * LANES
    nq = seq // tq
    return pl.pallas_call(
        functools.partial(_fox_attention_kernel, tq=tq),
        grid=(batch, steps, nq),
        in_specs=[pl.BlockSpec((tq, width), lambda b, p, i: (b * nq + i, p)),
                  pl.BlockSpec((tq, width), lambda b, p, i: (b * nq + i, p)),
                  pl.BlockSpec((seq, width), lambda b, p, i: (b, steps + p)),
                  pl.BlockSpec((seq, width), lambda b, p, i: (b, p)),
                  pl.BlockSpec((width, seq), lambda b, p, i: (p, b)),
                  pl.BlockSpec((tq, width), lambda b, p, i: (b * nq + i, 2 * steps + p))],
        out_specs=pl.BlockSpec((tq, width), lambda b, p, i: (b * nq + i, p)),
        out_shape=jax.ShapeDtypeStruct((t, pairs * LANES), BF16),
        compiler_params=_cparams("parallel", "parallel", "arbitrary"),
        name="fox_attention",
    )(fox, eq, fox, ek, vt, fox)


GDN_WIDTH = GDN_HEADS * GDN_HEAD_DIM
GDN_GROUP = 4
GDN_CAT = GDN_HEADS * GDN_CHUNK
GDN_PAIRS = GDN_HEADS // 2
GDN_PAIR = 2 * GDN_HEAD_DIM
GDN_SCAN_CHUNKS = 2
GDN_PREP_CHUNKS = 4
CONV_HALO = 8


def _gdn_expanders():
    e_g128 = np.zeros((LANES, GDN_WIDTH), np.float32)
    e_b128 = np.zeros((LANES, GDN_WIDTH), np.float32)
    e_g64 = np.zeros((LANES, GDN_CAT), np.float32)
    for h in range(GDN_HEADS):
        e_g128[SMALL_GA + h, h * GDN_HEAD_DIM:(h + 1) * GDN_HEAD_DIM] = 1.0
        e_b128[SMALL_GB + h, h * GDN_HEAD_DIM:(h + 1) * GDN_HEAD_DIM] = 1.0
        e_g64[SMALL_GA + h, h * GDN_CHUNK:(h + 1) * GDN_CHUNK] = 1.0
    return e_g128, e_b128, e_g64


def _block_diag(x, n_blocks):
    r, total = x.shape
    w = total // n_blocks
    tile_w = max(w, LANES)
    per_tile = tile_w // w
    zeros = jnp.zeros((r, tile_w), x.dtype)
    lane_block = _div_pow2(lax.broadcasted_iota(jnp.int32, (1, tile_w), 1), w)
    rows = []
    for h in range(n_blocks):
        t = h // per_tile
        tile = x[:, t * tile_w:(t + 1) * tile_w]
        if per_tile > 1:
            tile = tile * jnp.where(lane_block == h % per_tile, 1.0, 0.0).astype(x.dtype)
        rows.append(jnp.concatenate([tile if i == t else zeros for i in range(total // tile_w)], axis=1))
    return jnp.concatenate(rows, axis=0)


def _rows(x, i, n):
    return x[i * n:(i + 1) * n]


def _headwise_products(lhs, b):
    c = b.shape[0]
    n = len(lhs)
    b_hi, b_lo = _split2(b)
    bd_hi = _block_diag(b_hi, GDN_GROUP)
    bd_lo = _block_diag(b_lo, GDN_GROUP)
    parts = [_split2(x) for x in lhs]
    his = [p[0] for p in parts]
    los = [p[1] for p in parts]
    top = _dot(jnp.concatenate(his + los, axis=0), bd_hi)
    bot = _dot(jnp.concatenate(his, axis=0), bd_lo)
    return [_rows(top, i, c) + _rows(top, n + i, c) + _rows(bot, i, c) for i in range(n)]


def _unit_lower_inverses(l_cats):
    c, n = l_cats[0].shape
    r = lax.broadcasted_iota(jnp.int32, (c, n), 0)
    j = _mod_pow2(lax.broadcasted_iota(jnp.int32, (c, n), 1), c)
    eye = jnp.where(r == j, 1.0, 0.0)
    ss = [eye - l for l in l_cats]
    ps = [_headwise_products([l], l)[0] for l in l_cats]
    k = 2
    while 2 * k < c:
        stage = [_headwise_products([p, s], p) for p, s in zip(ps, ss)]
        ps = [st[0] for st in stage]
        ss = [s + st[1] for s, st in zip(ss, stage)]
        k *= 2
    return [s + _headwise_products([s], p)[0] for p, s in zip(ps, ss)]


def _gdn_prep_kernel(q_ref, k_ref, v_ref, hq_ref, hk_ref, hv_ref, small_ref, cw_ref, alog_ref,
                     dtb_ref, eg128_ref, eb128_ref, eg64_ref,
                     u_ref, w_ref, qg_ref, kd_ref, qk_ref, gl_ref, *, blocks_per_seq):
    c = GDN_CHUNK
    n_conv = cw_ref.shape[1]
    seq_start = lax.rem(pl.program_id(0), blocks_per_seq) == 0

    sublane = lax.broadcasted_iota(jnp.int32, (1, CONV_HALO, 1), 1)
    conv = []
    for s, (ref, halo_ref) in enumerate(((q_ref, hq_ref), (k_ref, hk_ref), (v_ref, hv_ref))):
        x = ref[...]
        halo = jnp.where(seq_start, 0.0, halo_ref[...])
        groups = x.reshape(-1, CONV_HALO, x.shape[1])
        prev = jnp.concatenate([halo[None], groups[:-1]], axis=0)
        acc = x * cw_ref[s, n_conv - 1:n_conv, :]
        for shift in range(1, n_conv):
            mixed = jnp.where(sublane >= CONV_HALO - shift, prev, groups)
            moved = pltpu.roll(mixed, shift, axis=1).reshape(x.shape)
            acc = acc + moved * cw_ref[s, n_conv - 1 - shift:n_conv - shift, :]
        conv.append(_silu(acc))
    cq, ck, cv = conv
    rows = cq.shape[0]

    def l2n(x):
        parts = []
        for h in range(GDN_HEADS):
            xh = x[:, h * GDN_HEAD_DIM:(h + 1) * GDN_HEAD_DIM]
            ss = jnp.sum(xh * xh, axis=-1, keepdims=True)
            parts.append(xh * lax.rsqrt(ss + EPS))
        return jnp.concatenate(parts, axis=1)

    qn = l2n(cq)
    kn = l2n(ck)

    small = small_ref[...]
    g_tok = -jnp.exp(alog_ref[...]) * _softplus(small + dtb_ref[...])
    beta_tok = _sigmoid(small)
    row = lax.broadcasted_iota(jnp.int32, (rows, rows), 0)
    col = lax.broadcasted_iota(jnp.int32, (rows, rows), 1)
    same_chunk = _div_pow2(row, c) == _div_pow2(col, c)
    tril = jnp.where(row >= col, jnp.where(same_chunk, 1.0, 0.0), 0.0).astype(BF16)
    gc_tok = _dot_exact_lhs(tril, g_tok)
    gc128 = _dot_exact_rhs(gc_tok, eg128_ref[...])
    gc64 = _dot_exact_rhs(gc_tok, eg64_ref[...])
    beta128 = _dot_exact_rhs(beta_tok, eb128_ref[...])

    exp_gc = jnp.exp(gc128)
    kb = kn * beta128
    vb = cv * beta128
    kbg = kb * exp_gc
    qs = qn * (GDN_HEAD_DIM ** -0.5)
    qg_ref[...] = (qs * exp_gc).astype(BF16)
    kb16 = kb.astype(BF16)
    qs16 = qs.astype(BF16)
    kn16 = kn.astype(BF16)

    r_cat = lax.broadcasted_iota(jnp.int32, (c, GDN_CAT), 0)
    j_cat = _mod_pow2(lax.broadcasted_iota(jnp.int32, (c, GDN_CAT), 1), c)
    tri_cat = r_cat >= j_cat
    strict_cat = r_cat > j_cat
    group_w = GDN_GROUP * GDN_HEAD_DIM
    group_c = GDN_GROUP * c
    contract_last = (((1,), (1,)), ((), ()))

    n_chunks = rows // c
    n_groups = GDN_HEADS // GDN_GROUP
    l_cats = []
    for ch in range(n_chunks):
        rs = slice(ch * c, (ch + 1) * c)
        gc64_c = gc64[rs]
        gc_row = jnp.sum(jnp.where(r_cat == j_cat, gc64_c, 0.0), axis=0, keepdims=True)
        decay = jnp.where(tri_cat, jnp.exp(jnp.where(tri_cat, gc64_c - gc_row, 0.0)), 0.0)
        g_last = gc128[ch * c + c - 1:ch * c + c, :]
        gl_ref[ch] = jnp.exp(g_last)
        kd_ref[rs, :] = (kn[rs] * jnp.exp(g_last - gc128[rs])).astype(BF16)

        for gidx in range(n_groups):
            ksl = slice(gidx * group_w, (gidx + 1) * group_w)
            csl = slice(gidx * group_c, (gidx + 1) * group_c)
            bd_k = _block_diag(kn16[rs, ksl], GDN_GROUP)
            both = lax.dot_general(jnp.concatenate([kb16[rs, ksl], qs16[rs, ksl]], axis=0), bd_k,
                                   contract_last, preferred_element_type=F32)
            dec = decay[:, csl]
            l_cats.append(jnp.where(strict_cat[:, csl], both[:c] * dec, 0.0))
            qk_ref[rs, csl] = (both[c:] * dec).astype(BF16)

    invs = _unit_lower_inverses(l_cats)

    for ch in range(n_chunks):
        rs = slice(ch * c, (ch + 1) * c)
        inv_cat = jnp.concatenate(invs[ch * n_groups:(ch + 1) * n_groups], axis=1)
        for p in range(GDN_PAIRS):
            wsl = slice(p * GDN_PAIR, (p + 1) * GDN_PAIR)
            inv_p = inv_cat[:, p * 2 * c:(p + 1) * 2 * c]
            vb_p, kbg_p = vb[rs, wsl], kbg[rs, wsl]
            rhs = jnp.concatenate([_block_diag(vb_p, 2), _block_diag(kbg_p, 2)], axis=1)
            i_hi, i_lo = _split2(inv_p)
            r_hi, r_lo = _split2(rhs)
            top = _dot(jnp.concatenate([i_hi, i_lo], axis=0), r_hi)
            sol = top[:c] + top[c:] + _dot(i_hi, r_lo)
            u_ref[rs, wsl] = sol[:, :GDN_PAIR]
            w_ref[rs, wsl] = sol[:, GDN_PAIR:].astype(BF16)


def _gdn_scan_kernel(u_ref, w_ref, qg_ref, kd_ref, qk_ref, gl_ref, z_ref, nw_ref, o_ref, state_ref):
    c = GDN_CHUNK

    @pl.when(pl.program_id(0) == 0)
    def _():
        state_ref[...] = jnp.zeros_like(state_ref)

    nw = nw_ref[...]
    zblock = jnp.zeros((GDN_HEAD_DIM, GDN_HEAD_DIM), BF16)
    hsl = lambda h: slice(h * GDN_HEAD_DIM, (h + 1) * GDN_HEAD_DIM)
    wsl = lambda p: slice(p * GDN_PAIR, (p + 1) * GDN_PAIR)
    cells = [(b, p) for b in range(u_ref.shape[0]) for p in range(GDN_PAIRS)]
    states = [[state_ref[b, 2 * p], state_ref[b, 2 * p + 1]] for b, p in cells]
    for ci in range(u_ref.shape[1] // c):
        rs = slice(ci * c, (ci + 1) * c)
        boths = []
        for (b, p), (sa, sb) in zip(cells, states):
            s_bd = jnp.concatenate([jnp.concatenate([sa.astype(BF16), zblock], axis=1),
                                    jnp.concatenate([zblock, sb.astype(BF16)], axis=1)], axis=0)
            boths.append(_dot(jnp.concatenate([w_ref[b, rs, wsl(p)], qg_ref[b, rs, wsl(p)]], axis=0),
                              s_bd))
        v16s = [(u_ref[b, rs, wsl(p)] - both[:c]).astype(BF16) for (b, p), both in zip(cells, boths)]
        outs, upds = [], []
        for (b, p), both, v16 in zip(cells, boths, v16s):
            v_bd = _block_diag(v16, 2)
            outs.append(both[c:] + _dot(qk_ref[b, rs, p * 2 * c:(p + 1) * 2 * c], v_bd))
            upds.append([lax.dot_general(kd_ref[b, rs, hsl(2 * p + h)], v16[:, hsl(h)],
                                         (((0,), (0,)), ((), ())), preferred_element_type=F32)
                         for h in range(2)])
        for n, ((b, p), o, upd) in enumerate(zip(cells, outs, upds)):
            halves = []
            for h in range(2):
                head = 2 * p + h
                states[n][h] = states[n][h] * gl_ref[b, ci][:, hsl(head)] + upd[h]
                oh = o[:, hsl(h)]
                ms = jnp.mean(oh * oh, axis=-1, keepdims=True)
                halves.append(oh * lax.rsqrt(ms + EPS) * nw)
            o_ref[b, rs, wsl(p)] = (jnp.concatenate(halves, axis=1)
                                    * _silu(z_ref[b, rs, wsl(p)])).astype(o_ref.dtype)
    for (b, p), st in zip(cells, states):
        for h in range(2):
            state_ref[b, 2 * p + h] = st[h]


def _gdn(gdn, small, conv_w3, alog_row, dtb_row, nw_row, batch, seq):
    t = gdn.shape[0]
    c = GDN_CHUNK
    w = GDN_WIDTH
    rows = GDN_PREP_CHUNKS * c
    e_g128, e_b128, e_g64 = _gdn_expanders()
    n_conv = conv_w3.shape[1]
    tok = lambda col: (lambda i: (i, col))
    halo = lambda col: (lambda i: (jnp.maximum(i * (rows // CONV_HALO) - 1, 0), col))
    const2 = lambda i: (0, 0)
    u, wk, qg, kd, qk, gl = pl.pallas_call(
        functools.partial(_gdn_prep_kernel, blocks_per_seq=seq // rows),
        grid=(t // rows,),
        in_specs=[pl.BlockSpec((rows, w), tok(0)),
                  pl.BlockSpec((rows, w), tok(1)),
                  pl.BlockSpec((rows, w), tok(2)),
                  pl.BlockSpec((CONV_HALO, w), halo(0)),
                  pl.BlockSpec((CONV_HALO, w), halo(1)),
                  pl.BlockSpec((CONV_HALO, w), halo(2)),
                  pl.BlockSpec((rows, LANES), tok(0)),
                  pl.BlockSpec((3, n_conv, w), lambda i: (0, 0, 0)),
                  pl.BlockSpec((1, LANES), const2),
                  pl.BlockSpec((1, LANES), const2),
                  pl.BlockSpec((LANES, w), const2),
                  pl.BlockSpec((LANES, w), const2),
                  pl.BlockSpec((LANES, GDN_CAT), const2)],
        out_specs=[pl.BlockSpec((rows, w), tok(0)),
                   pl.BlockSpec((rows, w), tok(0)),
                   pl.BlockSpec((rows, w), tok(0)),
                   pl.BlockSpec((rows, w), tok(0)),
                   pl.BlockSpec((rows, GDN_CAT), tok(0)),
                   pl.BlockSpec((GDN_PREP_CHUNKS, 1, w), lambda i: (i, 0, 0))],
        out_shape=[jax.ShapeDtypeStruct((t, w), F32),
                   jax.ShapeDtypeStruct((t, w), BF16),
                   jax.ShapeDtypeStruct((t, w), BF16),
                   jax.ShapeDtypeStruct((t, w), BF16),
                   jax.ShapeDtypeStruct((t, GDN_CAT), BF16),
                   jax.ShapeDtypeStruct((t // c, 1, w), F32)],
        compiler_params=_cparams("parallel"),
        name="gdn_prep",
    )(gdn, gdn, gdn, gdn, gdn, gdn, small, conv_w3, alog_row, dtb_row,
      jnp.asarray(e_g128, BF16), jnp.asarray(e_b128, BF16), jnp.asarray(e_g64, BF16))

    seq3 = lambda a: a.reshape(batch, seq, a.shape[-1])
    scan_rows = GDN_SCAN_CHUNKS * c
    blk = lambda width, col=0: pl.BlockSpec((batch, scan_rows, width), lambda n: (0, n, col))
    o = pl.pallas_call(
        _gdn_scan_kernel,
        grid=(seq // scan_rows,),
        in_specs=[blk(w), blk(w), blk(w), blk(w), blk(GDN_CAT),
                  pl.BlockSpec((batch, GDN_SCAN_CHUNKS, 1, w), lambda n: (0, n, 0, 0)),
                  blk(w, 3),
                  pl.BlockSpec((1, GDN_HEAD_DIM), lambda n: (0, 0))],
        out_specs=blk(w),
        out_shape=jax.ShapeDtypeStruct((batch, seq, w), BF16),
        scratch_shapes=[pltpu.VMEM((batch, GDN_HEADS, GDN_HEAD_DIM, GDN_HEAD_DIM), F32)],
        compiler_params=_cparams("arbitrary"),
        name="gdn_scan",
    )(seq3(u), seq3(wk), seq3(qg), seq3(kd), seq3(qk), gl.reshape(batch, seq // c, 1, w),
      seq3(gdn), nw_row)
    return o.reshape(t, w)


def _merge_out_kernel(x_ref, oa_ref, ob_ref, ga_ref, gb_ref, wa_ref, wb_ref, wo_ref, o_ref):
    ya = _dot(oa_ref[...], wa_ref[...])
    yb = _dot(ob_ref[...], wb_ref[...])
    y = (_sigmoid(ga_ref[...].astype(F32)) * ya + _sigmoid(gb_ref[...].astype(F32)) * yb)
    o_ref[...] = x_ref[...] + _dot(y.astype(BF16), wo_ref[...])


def _merge_out(x, o_a, o_b, mg, w_a, w_b, w_o, tm):
    t, d = x.shape
    tokd = pl.BlockSpec((tm, d), lambda i: (i, 0))
    wspec = pl.BlockSpec((d, d), lambda i: (0, 0))
    return pl.pallas_call(
        _merge_out_kernel,
        grid=(t // tm,),
        in_specs=[tokd, tokd, tokd,
                  pl.BlockSpec((tm, d), lambda i: (i, 0)),
                  pl.BlockSpec((tm, d), lambda i: (i, 1)),
                  wspec, wspec, wspec],
        out_specs=tokd,
        out_shape=jax.ShapeDtypeStruct((t, d), F32),
        compiler_params=_cparams("parallel"),
        name="merge_out",
    )(x, o_a, o_b, mg, mg, w_a, w_b, w_o)


def _ffn_kernel(x_ref, nw_ref, wg_ref, wu_ref, wd_ref, fw_ref, o_ref, *, final_norm):
    x = x_ref[...]
    h = _rms_norm(x, nw_ref[...]).astype(BF16)
    a = (_silu(_dot(h, wg_ref[...])) * _dot(h, wu_ref[...])).astype(BF16)
    y = x + _dot(a, wd_ref[...])
    if final_norm:
        y = _rms_norm(y, fw_ref[...])
    o_ref[...] = y


def _ffn(x, nw, w_g, w_u, w_d, fw, final_norm, tm):
    t, d = x.shape
    f = w_g.shape[1]
    tokd = pl.BlockSpec((tm, d), lambda i: (i, 0))
    rowd = pl.BlockSpec((1, d), lambda i: (0, 0))
    return pl.pallas_call(
        functools.partial(_ffn_kernel, final_norm=final_norm),
        grid=(t // tm,),
        in_specs=[tokd, rowd,
                  pl.BlockSpec((d, f), lambda i: (0, 0), pipeline_mode=pl.Buffered(1)),
                  pl.BlockSpec((d, f), lambda i: (0, 0), pipeline_mode=pl.Buffered(1)),
                  pl.BlockSpec((f, d), lambda i: (0, 0), pipeline_mode=pl.Buffered(1)),
                  rowd],
        out_specs=tokd,
        out_shape=jax.ShapeDtypeStruct((t, d), F32),
        compiler_params=_cparams("parallel"),
        name="ffn",
    )(x, nw, w_g, w_u, w_d, fw)


def _pad_row(v, offset):
    return jnp.zeros((1, LANES), F32).at[0, offset:offset + v.shape[0]].set(v.astype(F32))


def _layer(x, batch, seq, norm_mix_w, w_in, conv_w, a_log, dt_bias, gdn_norm_w, fox_f_bias,
           w_branch_a, w_branch_b, w_out, norm_ffn_w, w_gate, w_up, w_down, final_w, final_norm):
    d = x.shape[1]
    gw, fw = GDN_WIDTH, FOX_HEADS * FOX_HEAD_DIM
    sizes = (gw, gw, gw, gw, GDN_HEADS, GDN_HEADS, fw, fw, fw, FOX_HEADS, fw, d, d)
    offs = np.concatenate([[0], np.cumsum(sizes)])
    col = lambda i: w_in[:, offs[i]:offs[i + 1]]
    w_gdn = jnp.concatenate([col(0), col(1), col(2), col(3)], axis=1).astype(BF16)
    w_fox = jnp.concatenate([col(6), col(7), col(10)], axis=1).astype(BF16)
    w_fv_t = col(8).T.astype(BF16)
    w_mg = jnp.concatenate([col(11), col(12)], axis=1).astype(BF16)
    n_small = 2 * GDN_HEADS + FOX_HEADS
    w_small = jnp.concatenate([col(4), col(5), col(9), jnp.zeros((d, LANES - n_small), F32)],
                              axis=1).astype(BF16)
    nw = norm_mix_w.reshape(1, d)

    tm = min(seq, TOKEN_TILE)
    (gdn,) = _norm_proj(x, nw, [(w_gdn, F32)], [], tm)
    fox, vt = _norm_proj(x, nw, [(w_fox, BF16)], [(w_fv_t, BF16)], tm)
    mg, small = _norm_proj(x, nw, [(w_mg, BF16), (w_small, F32)], [], tm)

    eq, ek = _fox_bias(small, _pad_row(fox_f_bias, SMALL_FF), batch, seq, tm)
    o_b = _fox_attention(fox, vt, eq, ek, batch, seq, tq=tm)

    conv_w3 = conv_w.reshape(conv_w.shape[0], 3, gw).transpose(1, 0, 2)
    o_a = _gdn(gdn, small, conv_w3, _pad_row(a_log, SMALL_GA), _pad_row(dt_bias, SMALL_GA),
               gdn_norm_w.reshape(1, GDN_HEAD_DIM), batch, seq)

    x1 = _merge_out(x, o_a, o_b, mg, w_branch_a.astype(BF16), w_branch_b.astype(BF16),
                    w_out.astype(BF16), tm)
    return _ffn(x1, norm_ffn_w.reshape(1, d), w_gate.astype(BF16), w_up.astype(BF16),
                w_down.astype(BF16), final_w.reshape(1, d), final_norm, tm)


def kernel(x, norm_mix_w, w_in, conv_w, a_log, dt_bias, gdn_norm_w, fox_f_bias, w_branch_a,
           w_branch_b, w_out, norm_ffn_w, w_gate, w_up, w_down, norm_final_w):
    batch, seq, d = x.shape
    depth = w_in.shape[0]
    assert seq % min(seq, TOKEN_TILE) == 0 and seq % (GDN_PREP_CHUNKS * GDN_CHUNK) == 0
    assert d % LANES == 0 and w_in.shape[2] == 4 * GDN_WIDTH + 4 * FOX_HEADS * FOX_HEAD_DIM \
        + 2 * GDN_HEADS + FOX_HEADS + 2 * d
    h = x.reshape(batch * seq, d)
    for l in range(depth):
        h = _layer(h, batch, seq, norm_mix_w[l], w_in[l], conv_w[l], a_log[l], dt_bias[l],
                   gdn_norm_w[l], fox_f_bias[l], w_branch_a[l], w_branch_b[l], w_out[l],
                   norm_ffn_w[l], w_gate[l], w_up[l], w_down[l], norm_final_w,
                   final_norm=(l == depth - 1))
    return h.reshape(batch, seq, d)
```

```python
import functools

import jax
import jax.numpy as jnp
import numpy as np
from jax import lax
from jax.experimental import pallas as pl
from jax.experimental.pallas import tpu as pltpu

F32 = jnp.float32
BF16 = jnp.bfloat16

EPS = 1e-6
GDN_HEADS = 8
GDN_HEAD_DIM = 128
GDN_CHUNK = 64
FOX_HEADS = 16
FOX_HEAD_DIM = 64
LANES = 128
VMEM_LIMIT_BYTES = 56 * 1024 * 1024
TOKEN_TILE = 512
NEG_BIG = -1e30


def _cparams(*semantics):
    return pltpu.CompilerParams(dimension_semantics=semantics,
                                vmem_limit_bytes=VMEM_LIMIT_BYTES)


def _split2(x):
    hi = x.astype(BF16)
    lo = (x - hi.astype(F32)).astype(BF16)
    return hi, lo


def _split3(x):
    hi = x.astype(BF16)
    r = x - hi.astype(F32)
    mid = r.astype(BF16)
    lo = (r - mid.astype(F32)).astype(BF16)
    return hi, mid, lo


def _dot(a, b):
    return jnp.dot(a, b, preferred_element_type=F32)


def _dot_exact_rhs(x, m_bf16):
    n = x.shape[0]
    y = _dot(jnp.concatenate(_split3(x), axis=0), m_bf16)
    return y[:n] + y[n:2 * n] + y[2 * n:]


def _dot_exact_lhs(m_bf16, x):
    n = x.shape[1]
    y = _dot(m_bf16, jnp.concatenate(_split3(x), axis=1))
    return y[:, :n] + y[:, n:2 * n] + y[:, 2 * n:]


def _div_pow2(x, n):
    assert n & (n - 1) == 0
    return jnp.right_shift(x, n.bit_length() - 1)


def _mod_pow2(x, n):
    assert n & (n - 1) == 0
    return jnp.bitwise_and(x, n - 1)


def _softplus(y):
    return jnp.maximum(y, 0.0) + jnp.log(1.0 + jnp.exp(-jnp.abs(y)))


def _sigmoid(y):
    return 0.5 * jnp.tanh(0.5 * y) + 0.5


def _silu(y):
    half = 0.5 * y
    return half + half * jnp.tanh(half)


def _rms_norm(x, w):
    ms = jnp.mean(x * x, axis=-1, keepdims=True)
    return x * lax.rsqrt(ms + EPS) * w


PROJ_CHUNK = 1024


def _norm_proj_kernel(x_ref, nw_ref, *refs, n_plain):
    n_w = len(refs) // 2
    h = _rms_norm(x_ref[...], nw_ref[...]).astype(BF16)
    for idx in range(n_w):
        w_ref, o_ref = refs[idx], refs[n_w + idx]
        if idx < n_plain:
            n = o_ref.shape[1]
            for c in range(0, n, PROJ_CHUNK):
                sl = slice(c, min(c + PROJ_CHUNK, n))
                o_ref[:, sl] = _dot(h, w_ref[:, sl]).astype(o_ref.dtype)
        else:
            n = o_ref.shape[0]
            for c in range(0, n, PROJ_CHUNK):
                sl = slice(c, min(c + PROJ_CHUNK, n))
                o_ref[sl, :] = lax.dot_general(w_ref[sl, :], h, (((1,), (1,)), ((), ())),
                                               preferred_element_type=F32).astype(o_ref.dtype)


def _norm_proj(x, nw, plain, transposed, tm):
    t, d = x.shape
    in_specs = [pl.BlockSpec((tm, d), lambda i: (i, 0)), pl.BlockSpec((1, d), lambda i: (0, 0))]
    out_specs, out_shape = [], []
    for w, dtype in plain:
        n = w.shape[1]
        in_specs.append(pl.BlockSpec((d, n), lambda i: (0, 0)))
        out_specs.append(pl.BlockSpec((tm, n), lambda i: (i, 0)))
        out_shape.append(jax.ShapeDtypeStruct((t, n), dtype))
    for wt, dtype in transposed:
        n = wt.shape[0]
        in_specs.append(pl.BlockSpec((n, d), lambda i: (0, 0)))
        out_specs.append(pl.BlockSpec((n, tm), lambda i: (0, i)))
        out_shape.append(jax.ShapeDtypeStruct((n, t), dtype))
    return pl.pallas_call(
        functools.partial(_norm_proj_kernel, n_plain=len(plain)),
        grid=(t // tm,),
        in_specs=in_specs,
        out_specs=out_specs,
        out_shape=out_shape,
        compiler_params=_cparams("parallel"),
        name="norm_proj",
    )(x, nw, *[w for w, _ in plain], *[w for w, _ in transposed])


SMALL_GA = 0
SMALL_GB = GDN_HEADS
SMALL_FF = 2 * GDN_HEADS
BIAS_TERMS = 3
BIAS_HEAD_STRIDE = 8


def _bias_placements():
    n_out = (FOX_HEADS // 2) * LANES
    pq = np.zeros((LANES, n_out), np.float32)
    pk = np.zeros((LANES, n_out), np.float32)
    ones_q = np.zeros((1, n_out), np.float32)
    ones_k = np.zeros((1, n_out), np.float32)
    for h in range(FOX_HEADS):
        base = (h // 2) * LANES + (h % 2) * BIAS_HEAD_STRIDE
        for t in range(BIAS_TERMS):
            pq[SMALL_FF + t * FOX_HEADS + h, base + t] = 1.0
            pk[SMALL_FF + t * FOX_HEADS + h, base + BIAS_TERMS + t] = -1.0
            ones_q[0, base + BIAS_TERMS + t] = 1.0
            ones_k[0, base + t] = 1.0
    return pq, pk, ones_q, ones_k


def _fox_bias_kernel(small_ref, fb_ref, pq_ref, pk_ref, oq_ref, ok_ref, eq_ref, ek_ref,
                     carry_ref):
    tm = small_ref.shape[0]

    @pl.when(pl.program_id(1) == 0)
    def _():
        carry_ref[...] = jnp.zeros_like(carry_ref)

    z = small_ref[...] + fb_ref[...]
    log_f = -_softplus(-z)
    row = lax.broadcasted_iota(jnp.int32, (tm, tm), 0)
    col = lax.broadcasted_iota(jnp.int32, (tm, tm), 1)
    tril = jnp.where(row >= col, 1.0, 0.0).astype(BF16)
    cum = _dot_exact_lhs(tril, log_f) + carry_ref[0:1, :]
    carry_ref[...] = jnp.broadcast_to(cum[tm - 1:tm, :], carry_ref.shape)
    lane = lax.broadcasted_iota(jnp.int32, (1, LANES), 1)
    is_ff = jnp.logical_and(lane >= SMALL_FF, lane < SMALL_FF + FOX_HEADS)
    packed = jnp.zeros_like(cum)
    for t, term in enumerate(_split3(cum)):
        part = jnp.where(is_ff, term.astype(F32), 0.0)
        packed = packed + (pltpu.roll(part, t * FOX_HEADS, axis=1) if t else part)
    terms = packed.astype(BF16)
    eq_ref[...] = (oq_ref[...] + _dot(terms, pq_ref[...])).astype(BF16)
    ek_ref[...] = (ok_ref[...] + _dot(terms, pk_ref[...])).astype(BF16)


def _fox_bias(small, fb_row, batch, seq, tm):
    t = small.shape[0]
    pq, pk, ones_q, ones_k = _bias_placements()
    n_out = pq.shape[1]
    nt = seq // tm
    const2 = lambda b, i: (0, 0)
    return pl.pallas_call(
        _fox_bias_kernel,
        grid=(batch, nt),
        in_specs=[pl.BlockSpec((tm, LANES), lambda b, i: (b * nt + i, 0)),
                  pl.BlockSpec((1, LANES), const2),
                  pl.BlockSpec((LANES, n_out), const2),
                  pl.BlockSpec((LANES, n_out), const2),
                  pl.BlockSpec((1, n_out), const2),
                  pl.BlockSpec((1, n_out), const2)],
        out_specs=[pl.BlockSpec((tm, n_out), lambda b, i: (b * nt + i, 0)),
                   pl.BlockSpec((tm, n_out), lambda b, i: (b * nt + i, 0))],
        out_shape=[jax.ShapeDtypeStruct((t, n_out), BF16),
                   jax.ShapeDtypeStruct((t, n_out), BF16)],
        scratch_shapes=[pltpu.VMEM((8, LANES), F32)],
        compiler_params=_cparams("parallel", "arbitrary"),
        name="fox_bias",
    )(small, fb_row, jnp.asarray(pq, BF16), jnp.asarray(pk, BF16),
      jnp.asarray(ones_q), jnp.asarray(ones_k))


FOX_PAIRS_PER_STEP = 4
FOX_SUM_ROWS = 16
FOX_SCORES_AHEAD = 3


def _fox_attention_kernel(q_ref, eq_ref, k_ref, ek_ref, vt_ref, fo_ref, o_ref, *, tq):
    i = pl.program_id(2)
    n_pairs = q_ref.shape[1] // LANES
    n_heads = 2 * n_pairs
    lane = lax.broadcasted_iota(jnp.int32, (1, LANES), 1)
    head_a = lane < FOX_HEAD_DIM
    bias_a = lane < BIAS_HEAD_STRIDE
    zero = jnp.zeros((), BF16)
    qm = []
    for p in range(n_pairs):
        psl = slice(p * LANES, (p + 1) * LANES)
        q = q_ref[:, psl] * jnp.asarray(FOX_HEAD_DIM ** -0.5, BF16)
        eq = eq_ref[:, psl]
        qm.append(jnp.concatenate([jnp.where(head_a, q, zero), jnp.where(bias_a, eq, zero)], axis=1))
        qm.append(jnp.concatenate([jnp.where(head_a, zero, q), jnp.where(bias_a, zero, eq)], axis=1))

    def tile(j, carry, masked):
        start = pl.multiple_of(j * tq, tq)

        def score(idx):
            psl = slice((idx // 2) * LANES, (idx // 2 + 1) * LANES)
            kk = jnp.concatenate([k_ref[pl.ds(start, tq), psl], ek_ref[pl.ds(start, tq), psl]], axis=1)
            return lax.dot_general(kk, qm[idx], (((1,), (1,)), ((), ())),
                                   preferred_element_type=F32)

        def value(idx):
            vt = vt_ref[idx * FOX_HEAD_DIM:(idx + 1) * FOX_HEAD_DIM, pl.ds(start, tq)]
            return jnp.concatenate([vt, jnp.ones((FOX_SUM_ROWS, tq), BF16)], axis=0)

        out = []
        ahead = [score(idx) for idx in range(min(FOX_SCORES_AHEAD, n_heads))]
        for idx in range(n_heads):
            s = ahead.pop(0)
            if idx + FOX_SCORES_AHEAD < n_heads:
                ahead.append(score(idx + FOX_SCORES_AHEAD))
            m, acc = carry[idx]
            if masked:
                key = lax.broadcasted_iota(jnp.int32, (tq, tq), 0)
                qry = lax.broadcasted_iota(jnp.int32, (tq, tq), 1)
                s = jnp.where(key <= qry, s, NEG_BIG)
            m_new = jnp.maximum(m, jnp.max(s, axis=0, keepdims=True))
            alpha = jnp.exp(m - m_new)
            prob = jnp.exp(s - m_new).astype(BF16)
            acc = alpha * acc + _dot(value(idx), prob)
            out.append((m_new, acc))
        return tuple(out)

    init = tuple((jnp.full((1, tq), NEG_BIG, F32), jnp.zeros((FOX_HEAD_DIM + FOX_SUM_ROWS, tq), F32))
                 for _ in range(n_heads))
    carry = lax.fori_loop(0, i, lambda j, c: tile(j, c, False), init)
    final = tile(i, carry, True)
    for p in range(n_pairs):
        psl = slice(p * LANES, (p + 1) * LANES)
        heads = []
        for h in range(2):
            acc = final[2 * p + h][1]
            heads.append(acc[:FOX_HEAD_DIM] / acc[FOX_HEAD_DIM:FOX_HEAD_DIM + 1])
        gate = _sigmoid(fo_ref[:, psl].astype(F32))
        o_ref[:, psl] = (jnp.concatenate(heads, axis=0).T * gate).astype(o_ref.dtype)


def _fox_attention(fox, vt, eq, ek, batch, seq, tq):
    t = fox.shape[0]
    pairs = FOX_HEADS // 2
    steps = pairs // FOX_PAIRS_PER_STEP
    width = FOX_PAIRS_PER_STEP * LANES
    nq = seq // tq
    return pl.pallas_call(
        functools.partial(_fox_attention_kernel, tq=tq),
        grid=(batch, steps, nq),
        in_specs=[pl.BlockSpec((tq, width), lambda b, p, i: (b * nq + i, p)),
                  pl.BlockSpec((tq, width), lambda b, p, i: (b * nq + i, p)),
                  pl.BlockSpec((seq, width), lambda b, p, i: (b, steps + p)),
                  pl.BlockSpec((seq, width), lambda b, p, i: (b, p)),
                  pl.BlockSpec((width, seq), lambda b, p, i: (p, b)),
                  pl.BlockSpec((tq, width), lambda b, p, i: (b * nq + i, 2 * steps + p))],
        out_specs=pl.BlockSpec((tq, width), lambda b, p, i: (b * nq + i, p)),
        out_shape=jax.ShapeDtypeStruct((t, pairs * LANES), BF16),
        compiler_params=_cparams("parallel", "parallel", "arbitrary"),
        name="fox_attention",
    )(fox, eq, fox, ek, vt, fox)


GDN_WIDTH = GDN_HEADS * GDN_HEAD_DIM
GDN_GROUP = 4
GDN_CAT = GDN_HEADS * GDN_CHUNK
GDN_PAIRS = GDN_HEADS // 2
GDN_PAIR = 2 * GDN_HEAD_DIM
GDN_SCAN_CHUNKS = 4
GDN_PREP_CHUNKS = 4
CONV_HALO = 8


def _gdn_expanders():
    e_g128 = np.zeros((LANES, GDN_WIDTH), np.float32)
    e_b128 = np.zeros((LANES, GDN_WIDTH), np.float32)
    e_g64 = np.zeros((LANES, GDN_CAT), np.float32)
    for h in range(GDN_HEADS):
        e_g128[SMALL_GA + h, h * GDN_HEAD_DIM:(h + 1) * GDN_HEAD_DIM] = 1.0
        e_b128[SMALL_GB + h, h * GDN_HEAD_DIM:(h + 1) * GDN_HEAD_DIM] = 1.0
        e_g64[SMALL_GA + h, h * GDN_CHUNK:(h + 1) * GDN_CHUNK] = 1.0
    return e_g128, e_b128, e_g64


def _block_diag(x, n_blocks):
    r, total = x.shape
    w = total // n_blocks
    tile_w = max(w, LANES)
    per_tile = tile_w // w
    zeros = jnp.zeros((r, tile_w), x.dtype)
    lane_block = _div_pow2(lax.broadcasted_iota(jnp.int32, (1, tile_w), 1), w)
    rows = []
    for h in range(n_blocks):
        t = h // per_tile
        tile = x[:, t * tile_w:(t + 1) * tile_w]
        if per_tile > 1:
            tile = tile * jnp.where(lane_block == h % per_tile, 1.0, 0.0).astype(x.dtype)
        rows.append(jnp.concatenate([tile if i == t else zeros for i in range(total // tile_w)], axis=1))
    return jnp.concatenate(rows, axis=0)


def _rows(x, i, n):
    return x[i * n:(i + 1) * n]


def _headwise_products(lhs, b):
    c = b.shape[0]
    n = len(lhs)
    b_hi, b_lo = _split2(b)
    bd_hi = _block_diag(b_hi, GDN_GROUP)
    bd_lo = _block_diag(b_lo, GDN_GROUP)
    parts = [_split2(x) for x in lhs]
    his = [p[0] for p in parts]
    los = [p[1] for p in parts]
    top = _dot(jnp.concatenate(his + los, axis=0), bd_hi)
    bot = _dot(jnp.concatenate(his, axis=0), bd_lo)
    return [_rows(top, i, c) + _rows(top, n + i, c) + _rows(bot, i, c) for i in range(n)]


def _unit_lower_inverses(l_cats):
    c, n = l_cats[0].shape
    r = lax.broadcasted_iota(jnp.int32, (c, n), 0)
    j = _mod_pow2(lax.broadcasted_iota(jnp.int32, (c, n), 1), c)
    eye = jnp.where(r == j, 1.0, 0.0)
    ss = [eye - l for l in l_cats]
    ps = [_headwise_products([l], l)[0] for l in l_cats]
    k = 2
    while 2 * k < c:
        stage = [_headwise_products([p, s], p) for p, s in zip(ps, ss)]
        ps = [st[0] for st in stage]
        ss = [s + st[1] for s, st in zip(ss, stage)]
        k *= 2
    return [s + _headwise_products([s], p)[0] for p, s in zip(ps, ss)]


def _gdn_prep_kernel(q_ref, k_ref, v_ref, hq_ref, hk_ref, hv_ref, small_ref, cw_ref, alog_ref,
                     dtb_ref, eg128_ref, eb128_ref, eg64_ref,
                     u_ref, w_ref, qg_ref, kd_ref, qk_ref, gl_ref, *, blocks_per_seq):
    c = GDN_CHUNK
    n_conv = cw_ref.shape[1]
    seq_start = lax.rem(pl.program_id(0), blocks_per_seq) == 0

    sublane = lax.broadcasted_iota(jnp.int32, (1, CONV_HALO, 1), 1)
    conv = []
    for s, (ref, halo_ref) in enumerate(((q_ref, hq_ref), (k_ref, hk_ref), (v_ref, hv_ref))):
        x = ref[...]
        halo = jnp.where(seq_start, 0.0, halo_ref[...])
        groups = x.reshape(-1, CONV_HALO, x.shape[1])
        prev = jnp.concatenate([halo[None], groups[:-1]], axis=0)
        acc = x * cw_ref[s, n_conv - 1:n_conv, :]
        for shift in range(1, n_conv):
            mixed = jnp.where(sublane >= CONV_HALO - shift, prev, groups)
            moved = pltpu.roll(mixed, shift, axis=1).reshape(x.shape)
            acc = acc + moved * cw_ref[s, n_conv - 1 - shift:n_conv - shift, :]
        conv.append(_silu(acc))
    cq, ck, cv = conv
    rows = cq.shape[0]

    def l2n(x):
        parts = []
        for h in range(GDN_HEADS):
            xh = x[:, h * GDN_HEAD_DIM:(h + 1) * GDN_HEAD_DIM]
            ss = jnp.sum(xh * xh, axis=-1, keepdims=True)
            parts.append(xh * lax.rsqrt(ss + EPS))
        return jnp.concatenate(parts, axis=1)

    qn = l2n(cq)
    kn = l2n(ck)

    small = small_ref[...]
    g_tok = -jnp.exp(alog_ref[...]) * _softplus(small + dtb_ref[...])
    beta_tok = _sigmoid(small)
    row = lax.broadcasted_iota(jnp.int32, (rows, rows), 0)
    col = lax.broadcasted_iota(jnp.int32, (rows, rows), 1)
    same_chunk = _div_pow2(row, c) == _div_pow2(col, c)
    tril = jnp.where(row >= col, jnp.where(same_chunk, 1.0, 0.0), 0.0).astype(BF16)
    gc_tok = _dot_exact_lhs(tril, g_tok)
    gc128 = _dot_exact_rhs(gc_tok, eg128_ref[...])
    gc64 = _dot_exact_rhs(gc_tok, eg64_ref[...])
    beta128 = _dot_exact_rhs(beta_tok, eb128_ref[...])

    exp_gc = jnp.exp(gc128)
    kb = kn * beta128
    vb = cv * beta128
    kbg = kb * exp_gc
    qs = qn * (GDN_HEAD_DIM ** -0.5)
    qg_ref[...] = (qs * exp_gc).astype(BF16)
    kb16 = kb.astype(BF16)
    qs16 = qs.astype(BF16)
    kn16 = kn.astype(BF16)

    r_cat = lax.broadcasted_iota(jnp.int32, (c, GDN_CAT), 0)
    j_cat = _mod_pow2(lax.broadcasted_iota(jnp.int32, (c, GDN_CAT), 1), c)
    tri_cat = r_cat >= j_cat
    strict_cat = r_cat > j_cat
    group_w = GDN_GROUP * GDN_HEAD_DIM
    group_c = GDN_GROUP * c
    contract_last = (((1,), (1,)), ((), ()))

    n_chunks = rows // c
    n_groups = GDN_HEADS // GDN_GROUP
    l_cats = []
    for ch in range(n_chunks):
        rs = slice(ch * c, (ch + 1) * c)
        gc64_c = gc64[rs]
        gc_row = jnp.sum(jnp.where(r_cat == j_cat, gc64_c, 0.0), axis=0, keepdims=True)
        decay = jnp.where(tri_cat, jnp.exp(jnp.where(tri_cat, gc64_c - gc_row, 0.0)), 0.0)
        g_last = gc128[ch * c + c - 1:ch * c + c, :]
        gl_ref[ch] = jnp.exp(g_last)
        kd_ref[rs, :] = (kn[rs] * jnp.exp(g_last - gc128[rs])).astype(BF16)

        for gidx in range(n_groups):
            ksl = slice(gidx * group_w, (gidx + 1) * group_w)
            csl = slice(gidx * group_c, (gidx + 1) * group_c)
            bd_k = _block_diag(kn16[rs, ksl], GDN_GROUP)
            both = lax.dot_general(jnp.concatenate([kb16[rs, ksl], qs16[rs, ksl]], axis=0), bd_k,
                                   contract_last, preferred_element_type=F32)
            dec = decay[:, csl]
            l_cats.append(jnp.where(strict_cat[:, csl], both[:c] * dec, 0.0))
            qk_ref[rs, csl] = (both[c:] * dec).astype(BF16)

    invs = _unit_lower_inverses(l_cats)

    for ch in range(n_chunks):
        rs = slice(ch * c, (ch + 1) * c)
        inv_cat = jnp.concatenate(invs[ch * n_groups:(ch + 1) * n_groups], axis=1)
        for p in range(GDN_PAIRS):
            wsl = slice(p * GDN_PAIR, (p + 1) * GDN_PAIR)
            inv_p = inv_cat[:, p * 2 * c:(p + 1) * 2 * c]
            vb_p, kbg_p = vb[rs, wsl], kbg[rs, wsl]
            rhs = jnp.concatenate([_block_diag(vb_p, 2), _block_diag(kbg_p, 2)], axis=1)
            i_hi, i_lo = _split2(inv_p)
            r_hi, r_lo = _split2(rhs)
            top = _dot(jnp.concatenate([i_hi, i_lo], axis=0), r_hi)
            sol = top[:c] + top[c:] + _dot(i_hi, r_lo)
            u_ref[rs, wsl] = sol[:, :GDN_PAIR]
            w_ref[rs, wsl] = sol[:, GDN_PAIR:].astype(BF16)


def _gdn_scan_kernel(u_ref, w_ref, qg_ref, kd_ref, qk_ref, gl_ref, z_ref, nw_ref, o_ref, state_ref):
    c = GDN_CHUNK

    @pl.when(pl.program_id(0) == 0)
    def _():
        state_ref[...] = jnp.zeros_like(state_ref)

    nw = nw_ref[...]
    zblock = jnp.zeros((GDN_HEAD_DIM, GDN_HEAD_DIM), BF16)
    hsl = lambda h: slice(h * GDN_HEAD_DIM, (h + 1) * GDN_HEAD_DIM)
    wsl = lambda p: slice(p * GDN_PAIR, (p + 1) * GDN_PAIR)
    cells = [(b, p) for b in range(u_ref.shape[0]) for p in range(GDN_PAIRS)]
    states = [[state_ref[b, 2 * p], state_ref[b, 2 * p + 1]] for b, p in cells]
    for ci in range(u_ref.shape[1] // c):
        rs = slice(ci * c, (ci + 1) * c)
        boths = []
        for (b, p), (sa, sb) in zip(cells, states):
            s_bd = jnp.concatenate([jnp.concatenate([sa.astype(BF16), zblock], axis=1),
                                    jnp.concatenate([zblock, sb.astype(BF16)], axis=1)], axis=0)
            boths.append(_dot(jnp.concatenate([w_ref[b, rs, wsl(p)], qg_ref[b, rs, wsl(p)]], axis=0),
                              s_bd))
        v16s = [(u_ref[b, rs, wsl(p)] - both[:c]).astype(BF16) for (b, p), both in zip(cells, boths)]
        outs, upds = [], []
        for (b, p), both, v16 in zip(cells, boths, v16s):
            v_bd = _block_diag(v16, 2)
            outs.append(both[c:] + _dot(qk_ref[b, rs, p * 2 * c:(p + 1) * 2 * c], v_bd))
            upds.append([lax.dot_general(kd_ref[b, rs, hsl(2 * p + h)], v16[:, hsl(h)],
                                         (((0,), (0,)), ((), ())), preferred_element_type=F32)
                         for h in range(2)])
        for n, ((b, p), o, upd) in enumerate(zip(cells, outs, upds)):
            halves = []
            for h in range(2):
                head = 2 * p + h
                states[n][h] = states[n][h] * gl_ref[b, ci][:, hsl(head)] + upd[h]
                oh = o[:, hsl(h)]
                ms = jnp.mean(oh * oh, axis=-1, keepdims=True)
                halves.append(oh * lax.rsqrt(ms + EPS) * nw)
            o_ref[b, rs, wsl(p)] = (jnp.concatenate(halves, axis=1)
                                    * _silu(z_ref[b, rs, wsl(p)])).astype(o_ref.dtype)
    for (b, p), st in zip(cells, states):
        for h in range(2):
            state_ref[b, 2 * p + h] = st[h]


def _gdn(gdn, small, conv_w3, alog_row, dtb_row, nw_row, batch, seq):
    t = gdn.shape[0]
    c = GDN_CHUNK
    w = GDN_WIDTH
    rows = GDN_PREP_CHUNKS * c
    e_g128, e_b128, e_g64 = _gdn_expanders()
    n_conv = conv_w3.shape[1]
    tok = lambda col: (lambda i: (i, col))
    halo = lambda col: (lambda i: (jnp.maximum(i * (rows // CONV_HALO) - 1, 0), col))
    const2 = lambda i: (0, 0)
    u, wk, qg, kd, qk, gl = pl.pallas_call(
        functools.partial(_gdn_prep_kernel, blocks_per_seq=seq // rows),
        grid=(t // rows,),
        in_specs=[pl.BlockSpec((rows, w), tok(0)),
                  pl.BlockSpec((rows, w), tok(1)),
                  pl.BlockSpec((rows, w), tok(2)),
                  pl.BlockSpec((CONV_HALO, w), halo(0)),
                  pl.BlockSpec((CONV_HALO, w), halo(1)),
                  pl.BlockSpec((CONV_HALO, w), halo(2)),
                  pl.BlockSpec((rows, LANES), tok(0)),
                  pl.BlockSpec((3, n_conv, w), lambda i: (0, 0, 0)),
                  pl.BlockSpec((1, LANES), const2),
                  pl.BlockSpec((1, LANES), const2),
                  pl.BlockSpec((LANES, w), const2),
                  pl.BlockSpec((LANES, w), const2),
                  pl.BlockSpec((LANES, GDN_CAT), const2)],
        out_specs=[pl.BlockSpec((rows, w), tok(0)),
                   pl.BlockSpec((rows, w), tok(0)),
                   pl.BlockSpec((rows, w), tok(0)),
                   pl.BlockSpec((rows, w), tok(0)),
                   pl.BlockSpec((rows, GDN_CAT), tok(0)),
                   pl.BlockSpec((GDN_PREP_CHUNKS, 1, w), lambda i: (i, 0, 0))],
        out_shape=[jax.ShapeDtypeStruct((t, w), F32),
                   jax.ShapeDtypeStruct((t, w), BF16),
                   jax.ShapeDtypeStruct((t, w), BF16),
                   jax.ShapeDtypeStruct((t, w), BF16),
                   jax.ShapeDtypeStruct((t, GDN_CAT), BF16),
                   jax.ShapeDtypeStruct((t // c, 1, w), F32)],
        compiler_params=_cparams("parallel"),
        name="gdn_prep",
    )(gdn, gdn, gdn, gdn, gdn, gdn, small, conv_w3, alog_row, dtb_row,
      jnp.asarray(e_g128, BF16), jnp.asarray(e_b128, BF16), jnp.asarray(e_g64, BF16))

    seq3 = lambda a: a.reshape(batch, seq, a.shape[-1])
    scan_rows = GDN_SCAN_CHUNKS * c
    blk = lambda width, col=0: pl.BlockSpec((batch, scan_rows, width), lambda n: (0, n, col))
    o = pl.pallas_call(
        _gdn_scan_kernel,
        grid=(seq // scan_rows,),
        in_specs=[blk(w), blk(w), blk(w), blk(w), blk(GDN_CAT),
                  pl.BlockSpec((batch, GDN_SCAN_CHUNKS, 1, w), lambda n: (0, n, 0, 0)),
                  blk(w, 3),
                  pl.BlockSpec((1, GDN_HEAD_DIM), lambda n: (0, 0))],
        out_specs=blk(w),
        out_shape=jax.ShapeDtypeStruct((batch, seq, w), BF16),
        scratch_shapes=[pltpu.VMEM((batch, GDN_HEADS, GDN_HEAD_DIM, GDN_HEAD_DIM), F32)],
        compiler_params=_cparams("arbitrary"),
        name="gdn_scan",
    )(seq3(u), seq3(wk), seq3(qg), seq3(kd), seq3(qk), gl.reshape(batch, seq // c, 1, w),
      seq3(gdn), nw_row)
    return o.reshape(t, w)


def _merge_out_kernel(x_ref, oa_ref, ob_ref, ga_ref, gb_ref, wa_ref, wb_ref, wo_ref, o_ref):
    ya = _dot(oa_ref[...], wa_ref[...])
    yb = _dot(ob_ref[...], wb_ref[...])
    y = (_sigmoid(ga_ref[...].astype(F32)) * ya + _sigmoid(gb_ref[...].astype(F32)) * yb)
    o_ref[...] = x_ref[...] + _dot(y.astype(BF16), wo_ref[...])


def _merge_out(x, o_a, o_b, mg, w_a, w_b, w_o, tm):
    t, d = x.shape
    tokd = pl.BlockSpec((tm, d), lambda i: (i, 0))
    wspec = pl.BlockSpec((d, d), lambda i: (0, 0))
    return pl.pallas_call(
        _merge_out_kernel,
        grid=(t // tm,),
        in_specs=[tokd, tokd, tokd,
                  pl.BlockSpec((tm, d), lambda i: (i, 0)),
                  pl.BlockSpec((tm, d), lambda i: (i, 1)),
                  wspec, wspec, wspec],
        out_specs=tokd,
        out_shape=jax.ShapeDtypeStruct((t, d), F32),
        compiler_params=_cparams("parallel"),
        name="merge_out",
    )(x, o_a, o_b, mg, mg, w_a, w_b, w_o)


def _ffn_kernel(x_ref, nw_ref, wg_ref, wu_ref, wd_ref, fw_ref, o_ref, *, final_norm):
    x = x_ref[...]
    h = _rms_norm(x, nw_ref[...]).astype(BF16)
    a = (_silu(_dot(h, wg_ref[...])) * _dot(h, wu_ref[...])).astype(BF16)
    y = x + _dot(a, wd_ref[...])
    if final_norm:
        y = _rms_norm(y, fw_ref[...])
    o_ref[...] = y


def _ffn(x, nw, w_g, w_u, w_d, fw, final_norm, tm):
    t, d = x.shape
    f = w_g.shape[1]
    tokd = pl.BlockSpec((tm, d), lambda i: (i, 0))
    rowd = pl.BlockSpec((1, d), lambda i: (0, 0))
    return pl.pallas_call(
        functools.partial(_ffn_kernel, final_norm=final_norm),
        grid=(t // tm,),
        in_specs=[tokd, rowd,
                  pl.BlockSpec((d, f), lambda i: (0, 0), pipeline_mode=pl.Buffered(1)),
                  pl.BlockSpec((d, f), lambda i: (0, 0), pipeline_mode=pl.Buffered(1)),
                  pl.BlockSpec((f, d), lambda i: (0, 0), pipeline_mode=pl.Buffered(1)),
                  rowd],
        out_specs=tokd,
        out_shape=jax.ShapeDtypeStruct((t, d), F32),
        compiler_params=_cparams("parallel"),
        name="ffn",
    )(x, nw, w_g, w_u, w_d, fw)


def _pad_row(v, offset):
    return jnp.zeros((1, LANES), F32).at[0, offset:offset + v.shape[0]].set(v.astype(F32))


def _layer(x, batch, seq, norm_mix_w, w_in, conv_w, a_log, dt_bias, gdn_norm_w, fox_f_bias,
           w_branch_a, w_branch_b, w_out, norm_ffn_w, w_gate, w_up, w_down, final_w, final_norm):
    d = x.shape[1]
    gw, fw = GDN_WIDTH, FOX_HEADS * FOX_HEAD_DIM
    sizes = (gw, gw, gw, gw, GDN_HEADS, GDN_HEADS, fw, fw, fw, FOX_HEADS, fw, d, d)
    offs = np.concatenate([[0], np.cumsum(sizes)])
    col = lambda i: w_in[:, offs[i]:offs[i + 1]]
    w_gdn = jnp.concatenate([col(0), col(1), col(2), col(3)], axis=1).astype(BF16)
    w_fox = jnp.concatenate([col(6), col(7), col(10)], axis=1).astype(BF16)
    w_fv_t = col(8).T.astype(BF16)
    w_mg = jnp.concatenate([col(11), col(12)], axis=1).astype(BF16)
    n_small = 2 * GDN_HEADS + FOX_HEADS
    w_small = jnp.concatenate([col(4), col(5), col(9), jnp.zeros((d, LANES - n_small), F32)],
                              axis=1).astype(BF16)
    nw = norm_mix_w.reshape(1, d)

    tm = min(seq, TOKEN_TILE)
    (gdn,) = _norm_proj(x, nw, [(w_gdn, F32)], [], tm)
    fox, vt = _norm_proj(x, nw, [(w_fox, BF16)], [(w_fv_t, BF16)], tm)
    mg, small = _norm_proj(x, nw, [(w_mg, BF16), (w_small, F32)], [], tm)

    eq, ek = _fox_bias(small, _pad_row(fox_f_bias, SMALL_FF), batch, seq, tm)
    o_b = _fox_attention(fox, vt, eq, ek, batch, seq, tq=tm)

    conv_w3 = conv_w.reshape(conv_w.shape[0], 3, gw).transpose(1, 0, 2)
    o_a = _gdn(gdn, small, conv_w3, _pad_row(a_log, SMALL_GA), _pad_row(dt_bias, SMALL_GA),
               gdn_norm_w.reshape(1, GDN_HEAD_DIM), batch, seq)

    x1 = _merge_out(x, o_a, o_b, mg, w_branch_a.astype(BF16), w_branch_b.astype(BF16),
                    w_out.astype(BF16), tm)
    return _ffn(x1, norm_ffn_w.reshape(1, d), w_gate.astype(BF16), w_up.astype(BF16),
                w_down.astype(BF16), final_w.reshape(1, d), final_norm, tm)


def kernel(x, norm_mix_w, w_in, conv_w, a_log, dt_bias, gdn_norm_w, fox_f_bias, w_branch_a,
           w_branch_b, w_out, norm_ffn_w, w_gate, w_up, w_down, norm_final_w):
    batch, seq, d = x.shape
    depth = w_in.shape[0]
    assert seq % min(seq, TOKEN_TILE) == 0 and seq % (GDN_PREP_CHUNKS * GDN_CHUNK) == 0
    assert d % LANES == 0 and w_in.shape[2] == 4 * GDN_WIDTH + 4 * FOX_HEADS * FOX_HEAD_DIM \
        + 2 * GDN_HEADS + FOX_HEADS + 2 * d
    h = x.reshape(batch * seq, d)
    for l in range(depth):
        h = _layer(h, batch, seq, norm_mix_w[l], w_in[l], conv_w[l], a_log[l], dt_bias[l],
                   gdn_norm_w[l], fox_f_bias[l], w_branch_a[l], w_branch_b[l], w_out[l],
                   norm_ffn_w[l], w_gate[l], w_up[l], w_down[l], norm_final_w,
                   final_norm=(l == depth - 1))
    return h.reshape(batch, seq, d)
```

```python
import functools

import jax
import jax.numpy as jnp
import numpy as np
from jax import lax
from jax.experimental import pallas as pl
from jax.experimental.pallas import tpu as pltpu

F32 = jnp.float32
BF16 = jnp.bfloat16

EPS = 1e-6
GDN_HEADS = 8
GDN_HEAD_DIM = 128
GDN_CHUNK = 64
FOX_HEADS = 16
FOX_HEAD_DIM = 64
LANES = 128
VMEM_LIMIT_BYTES = 56 * 1024 * 1024
TOKEN_TILE = 512
NEG_BIG = -1e30


def _cparams(*semantics):
    return pltpu.CompilerParams(dimension_semantics=semantics,
                                vmem_limit_bytes=VMEM_LIMIT_BYTES)


def _split2(x):
    hi = x.astype(BF16)
    lo = (x - hi.astype(F32)).astype(BF16)
    return hi, lo


def _split3(x):
    hi = x.astype(BF16)
    r = x - hi.astype(F32)
    mid = r.astype(BF16)
    lo = (r - mid.astype(F32)).astype(BF16)
    return hi, mid, lo


def _dot(a, b):
    return jnp.dot(a, b, preferred_element_type=F32)


def _dot_exact_rhs(x, m_bf16):
    n = x.shape[0]
    y = _dot(jnp.concatenate(_split3(x), axis=0), m_bf16)
    return y[:n] + y[n:2 * n] + y[2 * n:]


def _dot_exact_lhs(m_bf16, x):
    n = x.shape[1]
    y = _dot(m_bf16, jnp.concatenate(_split3(x), axis=1))
    return y[:, :n] + y[:, n:2 * n] + y[:, 2 * n:]


def _div_pow2(x, n):
    assert n & (n - 1) == 0
    return jnp.right_shift(x, n.bit_length() - 1)


def _mod_pow2(x, n):
    assert n & (n - 1) == 0
    return jnp.bitwise_and(x, n - 1)


def _softplus(y):
    return jnp.maximum(y, 0.0) + jnp.log(1.0 + jnp.exp(-jnp.abs(y)))


def _sigmoid(y):
    return 0.5 * jnp.tanh(0.5 * y) + 0.5


def _silu(y):
    half = 0.5 * y
    return half + half * jnp.tanh(half)


def _rms_norm(x, w):
    ms = jnp.mean(x * x, axis=-1, keepdims=True)
    return x * lax.rsqrt(ms + EPS) * w


PROJ_CHUNK = 1024


def _norm_proj_kernel(x_ref, nw_ref, *refs, n_plain):
    n_w = len(refs) // 2
    h = _rms_norm(x_ref[...], nw_ref[...]).astype(BF16)
    for idx in range(n_w):
        w_ref, o_ref = refs[idx], refs[n_w + idx]
        if idx < n_plain:
            n = o_ref.shape[1]
            for c in range(0, n, PROJ_CHUNK):
                sl = slice(c, min(c + PROJ_CHUNK, n))
                o_ref[:, sl] = _dot(h, w_ref[:, sl]).astype(o_ref.dtype)
        else:
            n = o_ref.shape[0]
            for c in range(0, n, PROJ_CHUNK):
                sl = slice(c, min(c + PROJ_CHUNK, n))
                o_ref[sl, :] = lax.dot_general(w_ref[sl, :], h, (((1,), (1,)), ((), ())),
                                               preferred_element_type=F32).astype(o_ref.dtype)


def _norm_proj(x, nw, plain, transposed, tm):
    t, d = x.shape
    in_specs = [pl.BlockSpec((tm, d), lambda i: (i, 0)), pl.BlockSpec((1, d), lambda i: (0, 0))]
    out_specs, out_shape = [], []
    for w, dtype in plain:
        n = w.shape[1]
        in_specs.append(pl.BlockSpec((d, n), lambda i: (0, 0), pipeline_mode=pl.Buffered(1)))
        out_specs.append(pl.BlockSpec((tm, n), lambda i: (i, 0)))
        out_shape.append(jax.ShapeDtypeStruct((t, n), dtype))
    for wt, dtype in transposed:
        n = wt.shape[0]
        in_specs.append(pl.BlockSpec((n, d), lambda i: (0, 0), pipeline_mode=pl.Buffered(1)))
        out_specs.append(pl.BlockSpec((n, tm), lambda i: (0, i)))
        out_shape.append(jax.ShapeDtypeStruct((n, t), dtype))
    return pl.pallas_call(
        functools.partial(_norm_proj_kernel, n_plain=len(plain)),
        grid=(t // tm,),
        in_specs=in_specs,
        out_specs=out_specs,
        out_shape=out_shape,
        compiler_params=_cparams("parallel"),
        name="norm_proj",
    )(x, nw, *[w for w, _ in plain], *[w for w, _ in transposed])


SMALL_GA = 0
SMALL_GB = GDN_HEADS
SMALL_FF = 2 * GDN_HEADS
BIAS_TERMS = 3
BIAS_HEAD_STRIDE = 8


def _bias_placements():
    n_out = (FOX_HEADS // 2) * LANES
    pq = np.zeros((LANES, n_out), np.float32)
    pk = np.zeros((LANES, n_out), np.float32)
    ones_q = np.zeros((1, n_out), np.float32)
    ones_k = np.zeros((1, n_out), np.float32)
    for h in range(FOX_HEADS):
        base = (h // 2) * LANES + (h % 2) * BIAS_HEAD_STRIDE
        for t in range(BIAS_TERMS):
            pq[SMALL_FF + t * FOX_HEADS + h, base + t] = 1.0
            pk[SMALL_FF + t * FOX_HEADS + h, base + BIAS_TERMS + t] = -1.0
            ones_q[0, base + BIAS_TERMS + t] = 1.0
            ones_k[0, base + t] = 1.0
    return pq, pk, ones_q, ones_k


def _fox_bias_kernel(small_ref, fb_ref, pq_ref, pk_ref, oq_ref, ok_ref, eq_ref, ek_ref,
                     carry_ref):
    tm = small_ref.shape[0]

    @pl.when(pl.program_id(1) == 0)
    def _():
        carry_ref[...] = jnp.zeros_like(carry_ref)

    z = small_ref[...] + fb_ref[...]
    log_f = -_softplus(-z)
    row = lax.broadcasted_iota(jnp.int32, (tm, tm), 0)
    col = lax.broadcasted_iota(jnp.int32, (tm, tm), 1)
    tril = jnp.where(row >= col, 1.0, 0.0).astype(BF16)
    cum = _dot_exact_lhs(tril, log_f) + carry_ref[0:1, :]
    carry_ref[...] = jnp.broadcast_to(cum[tm - 1:tm, :], carry_ref.shape)
    lane = lax.broadcasted_iota(jnp.int32, (1, LANES), 1)
    is_ff = jnp.logical_and(lane >= SMALL_FF, lane < SMALL_FF + FOX_HEADS)
    packed = jnp.zeros_like(cum)
    for t, term in enumerate(_split3(cum)):
        part = jnp.where(is_ff, term.astype(F32), 0.0)
        packed = packed + (pltpu.roll(part, t * FOX_HEADS, axis=1) if t else part)
    terms = packed.astype(BF16)
    eq_ref[...] = (oq_ref[...] + _dot(terms, pq_ref[...])).astype(BF16)
    ek_ref[...] = (ok_ref[...] + _dot(terms, pk_ref[...])).astype(BF16)


def _fox_bias(small, fb_row, batch, seq, tm):
    t = small.shape[0]
    pq, pk, ones_q, ones_k = _bias_placements()
    n_out = pq.shape[1]
    nt = seq // tm
    const2 = lambda b, i: (0, 0)
    return pl.pallas_call(
        _fox_bias_kernel,
        grid=(batch, nt),
        in_specs=[pl.BlockSpec((tm, LANES), lambda b, i: (b * nt + i, 0)),
                  pl.BlockSpec((1, LANES), const2),
                  pl.BlockSpec((LANES, n_out), const2),
                  pl.BlockSpec((LANES, n_out), const2),
                  pl.BlockSpec((1, n_out), const2),
                  pl.BlockSpec((1, n_out), const2)],
        out_specs=[pl.BlockSpec((tm, n_out), lambda b, i: (b * nt + i, 0)),
                   pl.BlockSpec((tm, n_out), lambda b, i: (b * nt + i, 0))],
        out_shape=[jax.ShapeDtypeStruct((t, n_out), BF16),
                   jax.ShapeDtypeStruct((t, n_out), BF16)],
        scratch_shapes=[pltpu.VMEM((8, LANES), F32)],
        compiler_params=_cparams("parallel", "arbitrary"),
        name="fox_bias",
    )(small, fb_row, jnp.asarray(pq, BF16), jnp.asarray(pk, BF16),
      jnp.asarray(ones_q), jnp.asarray(ones_k))


FOX_PAIRS_PER_STEP = 4
FOX_SUM_ROWS = 16
FOX_SCORES_AHEAD = 3


def _fox_attention_kernel(q_ref, eq_ref, k_ref, ek_ref, vt_ref, fo_ref, o_ref, *, tq):
    i = pl.program_id(2)
    n_pairs = q_ref.shape[1] // LANES
    n_heads = 2 * n_pairs
    lane = lax.broadcasted_iota(jnp.int32, (1, LANES), 1)
    head_a = lane < FOX_HEAD_DIM
    bias_a = lane < BIAS_HEAD_STRIDE
    zero = jnp.zeros((), BF16)
    qm = []
    for p in range(n_pairs):
        psl = slice(p * LANES, (p + 1) * LANES)
        q = q_ref[:, psl] * jnp.asarray(FOX_HEAD_DIM ** -0.5, BF16)
        eq = eq_ref[:, psl]
        qm.append(jnp.concatenate([jnp.where(head_a, q, zero), jnp.where(bias_a, eq, zero)], axis=1))
        qm.append(jnp.concatenate([jnp.where(head_a, zero, q), jnp.where(bias_a, zero, eq)], axis=1))

    def tile(j, carry, masked):
        start = pl.multiple_of(j * tq, tq)

        def score(idx):
            psl = slice((idx // 2) * LANES, (idx // 2 + 1) * LANES)
            kk = jnp.concatenate([k_ref[pl.ds(start, tq), psl], ek_ref[pl.ds(start, tq), psl]], axis=1)
            return lax.dot_general(kk, qm[idx], (((1,), (1,)), ((), ())),
                                   preferred_element_type=F32)

        def value(idx):
            vt = vt_ref[idx * FOX_HEAD_DIM:(idx + 1) * FOX_HEAD_DIM, pl.ds(start, tq)]
            return jnp.concatenate([vt, jnp.ones((FOX_SUM_ROWS, tq), BF16)], axis=0)

        out = []
        ahead = [score(idx) for idx in range(min(FOX_SCORES_AHEAD, n_heads))]
        for idx in range(n_heads):
            s = ahead.pop(0)
            if idx + FOX_SCORES_AHEAD < n_heads:
                ahead.append(score(idx + FOX_SCORES_AHEAD))
            m, acc = carry[idx]
            if masked:
                key = lax.broadcasted_iota(jnp.int32, (tq, tq), 0)
                qry = lax.broadcasted_iota(jnp.int32, (tq, tq), 1)
                s = jnp.where(key <= qry, s, NEG_BIG)
            m_new = jnp.maximum(m, jnp.max(s, axis=0, keepdims=True))
            alpha = jnp.exp(m - m_new)
            prob = jnp.exp(s - m_new).astype(BF16)
            acc = alpha * acc + _dot(value(idx), prob)
            out.append((m_new, acc))
        return tuple(out)

    init = tuple((jnp.full((1, tq), NEG_BIG, F32), jnp.zeros((FOX_HEAD_DIM + FOX_SUM_ROWS, tq), F32))
                 for _ in range(n_heads))
    carry = lax.fori_loop(0, i, lambda j, c: tile(j, c, False), init)
    final = tile(i, carry, True)
    for p in range(n_pairs):
        psl = slice(p * LANES, (p + 1) * LANES)
        heads = []
        for h in range(2):
            acc = final[2 * p + h][1]
            heads.append(acc[:FOX_HEAD_DIM] / acc[FOX_HEAD_DIM:FOX_HEAD_DIM + 1])
        gate = _sigmoid(fo_ref[:, psl].astype(F32))
        o_ref[:, psl] = (jnp.concatenate(heads, axis=0).T * gate).astype(o_ref.dtype)


def _fox_attention(fox, vt, eq, ek, batch, seq, tq):
    t = fox.shape[0]
    pairs = FOX_HEADS // 2
    steps = pairs // FOX_PAIRS_PER_STEP
    width = FOX_PAIRS_PER_STEP * LANES
    nq = seq // tq
    return pl.pallas_call(
        functools.partial(_fox_attention_kernel, tq=tq),
        grid=(batch, steps, nq),
        in_specs=[pl.BlockSpec((tq, width), lambda b, p, i: (b * nq + i, p)),
                  pl.BlockSpec((tq, width), lambda b, p, i: (b * nq + i, p)),
                  pl.BlockSpec((seq, width), lambda b, p, i: (b, steps + p)),
                  pl.BlockSpec((seq, width), lambda b, p, i: (b, p)),
                  pl.BlockSpec((width, seq), lambda b, p, i: (p, b)),
                  pl.BlockSpec((tq, width), lambda b, p, i: (b * nq + i, 2 * steps + p))],
        out_specs=pl.BlockSpec((tq, width), lambda b, p, i: (b * nq + i, p)),
        out_shape=jax.ShapeDtypeStruct((t, pairs * LANES), BF16),
        compiler_params=_cparams("parallel", "parallel", "arbitrary"),
        name="fox_attention",
    )(fox, eq, fox, ek, vt, fox)


GDN_WIDTH = GDN_HEADS * GDN_HEAD_DIM
GDN_GROUP = 4
GDN_CAT = GDN_HEADS * GDN_CHUNK
GDN_PAIRS = GDN_HEADS // 2
GDN_PAIR = 2 * GDN_HEAD_DIM
GDN_SCAN_CHUNKS = 4
GDN_PREP_CHUNKS = 4
CONV_HALO = 8


def _gdn_expanders():
    e_g128 = np.zeros((LANES, GDN_WIDTH), np.float32)
    e_b128 = np.zeros((LANES, GDN_WIDTH), np.float32)
    e_g64 = np.zeros((LANES, GDN_CAT), np.float32)
    for h in range(GDN_HEADS):
        e_g128[SMALL_GA + h, h * GDN_HEAD_DIM:(h + 1) * GDN_HEAD_DIM] = 1.0
        e_b128[SMALL_GB + h, h * GDN_HEAD_DIM:(h + 1) * GDN_HEAD_DIM] = 1.0
        e_g64[SMALL_GA + h, h * GDN_CHUNK:(h + 1) * GDN_CHUNK] = 1.0
    return e_g128, e_b128, e_g64


def _block_diag(x, n_blocks):
    r, total = x.shape
    w = total // n_blocks
    tile_w = max(w, LANES)
    per_tile = tile_w // w
    zeros = jnp.zeros((r, tile_w), x.dtype)
    lane_block = _div_pow2(lax.broadcasted_iota(jnp.int32, (1, tile_w), 1), w)
    rows = []
    for h in range(n_blocks):
        t = h // per_tile
        tile = x[:, t * tile_w:(t + 1) * tile_w]
        if per_tile > 1:
            tile = tile * jnp.where(lane_block == h % per_tile, 1.0, 0.0).astype(x.dtype)
        rows.append(jnp.concatenate([tile if i == t else zeros for i in range(total // tile_w)], axis=1))
    return jnp.concatenate(rows, axis=0)


def _rows(x, i, n):
    return x[i * n:(i + 1) * n]


def _headwise_products(lhs, b):
    c = b.shape[0]
    n = len(lhs)
    b_hi, b_lo = _split2(b)
    bd_hi = _block_diag(b_hi, GDN_GROUP)
    bd_lo = _block_diag(b_lo, GDN_GROUP)
    parts = [_split2(x) for x in lhs]
    his = [p[0] for p in parts]
    los = [p[1] for p in parts]
    top = _dot(jnp.concatenate(his + los, axis=0), bd_hi)
    bot = _dot(jnp.concatenate(his, axis=0), bd_lo)
    return [_rows(top, i, c) + _rows(top, n + i, c) + _rows(bot, i, c) for i in range(n)]


def _unit_lower_inverses(l_cats):
    c, n = l_cats[0].shape
    r = lax.broadcasted_iota(jnp.int32, (c, n), 0)
    j = _mod_pow2(lax.broadcasted_iota(jnp.int32, (c, n), 1), c)
    eye = jnp.where(r == j, 1.0, 0.0)
    ss = [eye - l for l in l_cats]
    ps = [_headwise_products([l], l)[0] for l in l_cats]
    k = 2
    while 2 * k < c:
        stage = [_headwise_products([p, s], p) for p, s in zip(ps, ss)]
        ps = [st[0] for st in stage]
        ss = [s + st[1] for s, st in zip(ss, stage)]
        k *= 2
    return [s + _headwise_products([s], p)[0] for p, s in zip(ps, ss)]


def _gdn_prep_kernel(q_ref, k_ref, v_ref, hq_ref, hk_ref, hv_ref, small_ref, cw_ref, alog_ref,
                     dtb_ref, eg128_ref, eb128_ref, eg64_ref,
                     u_ref, w_ref, qg_ref, kd_ref, qk_ref, gl_ref, *, blocks_per_seq):
    c = GDN_CHUNK
    n_conv = cw_ref.shape[1]
    seq_start = lax.rem(pl.program_id(0), blocks_per_seq) == 0

    sublane = lax.broadcasted_iota(jnp.int32, (1, CONV_HALO, 1), 1)
    conv = []
    for s, (ref, halo_ref) in enumerate(((q_ref, hq_ref), (k_ref, hk_ref), (v_ref, hv_ref))):
        x = ref[...]
        halo = jnp.where(seq_start, 0.0, halo_ref[...])
        groups = x.reshape(-1, CONV_HALO, x.shape[1])
        prev = jnp.concatenate([halo[None], groups[:-1]], axis=0)
        acc = x * cw_ref[s, n_conv - 1:n_conv, :]
        for shift in range(1, n_conv):
            mixed = jnp.where(sublane >= CONV_HALO - shift, prev, groups)
            moved = pltpu.roll(mixed, shift, axis=1).reshape(x.shape)
            acc = acc + moved * cw_ref[s, n_conv - 1 - shift:n_conv - shift, :]
        conv.append(_silu(acc))
    cq, ck, cv = conv
    rows = cq.shape[0]

    def l2n(x):
        parts = []
        for h in range(GDN_HEADS):
            xh = x[:, h * GDN_HEAD_DIM:(h + 1) * GDN_HEAD_DIM]
            ss = jnp.sum(xh * xh, axis=-1, keepdims=True)
            parts.append(xh * lax.rsqrt(ss + EPS))
        return jnp.concatenate(parts, axis=1)

    qn = l2n(cq)
    kn = l2n(ck)

    small = small_ref[...]
    g_tok = -jnp.exp(alog_ref[...]) * _softplus(small + dtb_ref[...])
    beta_tok = _sigmoid(small)
    row = lax.broadcasted_iota(jnp.int32, (rows, rows), 0)
    col = lax.broadcasted_iota(jnp.int32, (rows, rows), 1)
    same_chunk = _div_pow2(row, c) == _div_pow2(col, c)
    tril = jnp.where(row >= col, jnp.where(same_chunk, 1.0, 0.0), 0.0).astype(BF16)
    gc_tok = _dot_exact_lhs(tril, g_tok)
    gc128 = _dot_exact_rhs(gc_tok, eg128_ref[...])
    gc64 = _dot_exact_rhs(gc_tok, eg64_ref[...])
    beta128 = _dot_exact_rhs(beta_tok, eb128_ref[...])

    exp_gc = jnp.exp(gc128)
    kb = kn * beta128
    vb = cv * beta128
    kbg = kb * exp_gc
    qs = qn * (GDN_HEAD_DIM ** -0.5)
    qg_ref[...] = (qs * exp_gc).astype(BF16)
    kb16 = kb.astype(BF16)
    qs16 = qs.astype(BF16)
    kn16 = kn.astype(BF16)

    r_cat = lax.broadcasted_iota(jnp.int32, (c, GDN_CAT), 0)
    j_cat = _mod_pow2(lax.broadcasted_iota(jnp.int32, (c, GDN_CAT), 1), c)
    tri_cat = r_cat >= j_cat
    strict_cat = r_cat > j_cat
    group_w = GDN_GROUP * GDN_HEAD_DIM
    group_c = GDN_GROUP * c
    contract_last = (((1,), (1,)), ((), ()))

    n_chunks = rows // c
    n_groups = GDN_HEADS // GDN_GROUP
    l_cats = []
    for ch in range(n_chunks):
        rs = slice(ch * c, (ch + 1) * c)
        gc64_c = gc64[rs]
        gc_row = jnp.sum(jnp.where(r_cat == j_cat, gc64_c, 0.0), axis=0, keepdims=True)
        decay = jnp.where(tri_cat, jnp.exp(jnp.where(tri_cat, gc64_c - gc_row, 0.0)), 0.0)
        g_last = gc128[ch * c + c - 1:ch * c + c, :]
        gl_ref[ch] = jnp.exp(g_last)
        kd_ref[rs, :] = (kn[rs] * jnp.exp(g_last - gc128[rs])).astype(BF16)

        for gidx in range(n_groups):
            ksl = slice(gidx * group_w, (gidx + 1) * group_w)
            csl = slice(gidx * group_c, (gidx + 1) * group_c)
            bd_k = _block_diag(kn16[rs, ksl], GDN_GROUP)
            both = lax.dot_general(jnp.concatenate([kb16[rs, ksl], qs16[rs, ksl]], axis=0), bd_k,
                                   contract_last, preferred_element_type=F32)
            dec = decay[:, csl]
            l_cats.append(jnp.where(strict_cat[:, csl], both[:c] * dec, 0.0))
            qk_ref[rs, csl] = (both[c:] * dec).astype(BF16)

    invs = _unit_lower_inverses(l_cats)

    for ch in range(n_chunks):
        rs = slice(ch * c, (ch + 1) * c)
        inv_cat = jnp.concatenate(invs[ch * n_groups:(ch + 1) * n_groups], axis=1)
        for p in range(GDN_PAIRS):
            wsl = slice(p * GDN_PAIR, (p + 1) * GDN_PAIR)
            inv_p = inv_cat[:, p * 2 * c:(p + 1) * 2 * c]
            vb_p, kbg_p = vb[rs, wsl], kbg[rs, wsl]
            rhs = jnp.concatenate([_block_diag(vb_p, 2), _block_diag(kbg_p, 2)], axis=1)
            i_hi, i_lo = _split2(inv_p)
            r_hi, r_lo = _split2(rhs)
            top = _dot(jnp.concatenate([i_hi, i_lo], axis=0), r_hi)
            sol = top[:c] + top[c:] + _dot(i_hi, r_lo)
            u_ref[rs, wsl] = sol[:, :GDN_PAIR]
            w_ref[rs, wsl] = sol[:, GDN_PAIR:].astype(BF16)


def _gdn_scan_kernel(u_ref, w_ref, qg_ref, kd_ref, qk_ref, gl_ref, z_ref, nw_ref, o_ref, state_ref):
    c = GDN_CHUNK

    @pl.when(pl.program_id(0) == 0)
    def _():
        state_ref[...] = jnp.zeros_like(state_ref)

    nw = nw_ref[...]
    zblock = jnp.zeros((GDN_HEAD_DIM, GDN_HEAD_DIM), BF16)
    hsl = lambda h: slice(h * GDN_HEAD_DIM, (h + 1) * GDN_HEAD_DIM)
    wsl = lambda p: slice(p * GDN_PAIR, (p + 1) * GDN_PAIR)
    cells = [(b, p) for b in range(u_ref.shape[0]) for p in range(GDN_PAIRS)]
    states = [[state_ref[b, 2 * p], state_ref[b, 2 * p + 1]] for b, p in cells]
    for ci in range(u_ref.shape[1] // c):
        rs = slice(ci * c, (ci + 1) * c)
        boths = []
        for (b, p), (sa, sb) in zip(cells, states):
            s_bd = jnp.concatenate([jnp.concatenate([sa.astype(BF16), zblock], axis=1),
                                    jnp.concatenate([zblock, sb.astype(BF16)], axis=1)], axis=0)
            boths.append(_dot(jnp.concatenate([w_ref[b, rs, wsl(p)], qg_ref[b, rs, wsl(p)]], axis=0),
                              s_bd))
        v16s = [(u_ref[b, rs, wsl(p)] - both[:c]).astype(BF16) for (b, p), both in zip(cells, boths)]
        outs, upds = [], []
        for (b, p), both, v16 in zip(cells, boths, v16s):
            v_bd = _block_diag(v16, 2)
            outs.append(both[c:] + _dot(qk_ref[b, rs, p * 2 * c:(p + 1) * 2 * c], v_bd))
            upds.append([lax.dot_general(kd_ref[b, rs, hsl(2 * p + h)], v16[:, hsl(h)],
                                         (((0,), (0,)), ((), ())), preferred_element_type=F32)
                         for h in range(2)])
        for n, ((b, p), o, upd) in enumerate(zip(cells, outs, upds)):
            halves = []
            for h in range(2):
                head = 2 * p + h
                states[n][h] = states[n][h] * gl_ref[b, ci][:, hsl(head)] + upd[h]
                oh = o[:, hsl(h)]
                ms = jnp.mean(oh * oh, axis=-1, keepdims=True)
                halves.append(oh * lax.rsqrt(ms + EPS) * nw)
            o_ref[b, rs, wsl(p)] = (jnp.concatenate(halves, axis=1)
                                    * _silu(z_ref[b, rs, wsl(p)])).astype(o_ref.dtype)
    for (b, p), st in zip(cells, states):
        for h in range(2):
            state_ref[b, 2 * p + h] = st[h]


def _gdn(gdn, small, conv_w3, alog_row, dtb_row, nw_row, batch, seq):
    t = gdn.shape[0]
    c = GDN_CHUNK
    w = GDN_WIDTH
    rows = GDN_PREP_CHUNKS * c
    e_g128, e_b128, e_g64 = _gdn_expanders()
    n_conv = conv_w3.shape[1]
    tok = lambda col: (lambda i: (i, col))
    halo = lambda col: (lambda i: (jnp.maximum(i * (rows // CONV_HALO) - 1, 0), col))
    const2 = lambda i: (0, 0)
    u, wk, qg, kd, qk, gl = pl.pallas_call(
        functools.partial(_gdn_prep_kernel, blocks_per_seq=seq // rows),
        grid=(t // rows,),
        in_specs=[pl.BlockSpec((rows, w), tok(0)),
                  pl.BlockSpec((rows, w), tok(1)),
                  pl.BlockSpec((rows, w), tok(2)),
                  pl.BlockSpec((CONV_HALO, w), halo(0)),
                  pl.BlockSpec((CONV_HALO, w), halo(1)),
                  pl.BlockSpec((CONV_HALO, w), halo(2)),
                  pl.BlockSpec((rows, LANES), tok(0)),
                  pl.BlockSpec((3, n_conv, w), lambda i: (0, 0, 0)),
                  pl.BlockSpec((1, LANES), const2),
                  pl.BlockSpec((1, LANES), const2),
                  pl.BlockSpec((LANES, w), const2),
                  pl.BlockSpec((LANES, w), const2),
                  pl.BlockSpec((LANES, GDN_CAT), const2)],
        out_specs=[pl.BlockSpec((rows, w), tok(0)),
                   pl.BlockSpec((rows, w), tok(0)),
                   pl.BlockSpec((rows, w), tok(0)),
                   pl.BlockSpec((rows, w), tok(0)),
                   pl.BlockSpec((rows, GDN_CAT), tok(0)),
                   pl.BlockSpec((GDN_PREP_CHUNKS, 1, w), lambda i: (i, 0, 0))],
        out_shape=[jax.ShapeDtypeStruct((t, w), F32),
                   jax.ShapeDtypeStruct((t, w), BF16),
                   jax.ShapeDtypeStruct((t, w), BF16),
                   jax.ShapeDtypeStruct((t, w), BF16),
                   jax.ShapeDtypeStruct((t, GDN_CAT), BF16),
                   jax.ShapeDtypeStruct((t // c, 1, w), F32)],
        compiler_params=_cparams("parallel"),
        name="gdn_prep",
    )(gdn, gdn, gdn, gdn, gdn, gdn, small, conv_w3, alog_row, dtb_row,
      jnp.asarray(e_g128, BF16), jnp.asarray(e_b128, BF16), jnp.asarray(e_g64, BF16))

    seq3 = lambda a: a.reshape(batch, seq, a.shape[-1])
    scan_rows = GDN_SCAN_CHUNKS * c
    blk = lambda width, col=0: pl.BlockSpec((batch, scan_rows, width), lambda n: (0, n, col))
    o = pl.pallas_call(
        _gdn_scan_kernel,
        grid=(seq // scan_rows,),
        in_specs=[blk(w), blk(w), blk(w), blk(w), blk(GDN_CAT),
                  pl.BlockSpec((batch, GDN_SCAN_CHUNKS, 1, w), lambda n: (0, n, 0, 0)),
                  blk(w, 3),
                  pl.BlockSpec((1, GDN_HEAD_DIM), lambda n: (0, 0))],
        out_specs=blk(w),
        out_shape=jax.ShapeDtypeStruct((batch, seq, w), BF16),
        scratch_shapes=[pltpu.VMEM((batch, GDN_HEADS, GDN_HEAD_DIM, GDN_HEAD_DIM), F32)],
        compiler_params=_cparams("arbitrary"),
        name="gdn_scan",
    )(seq3(u), seq3(wk), seq3(qg), seq3(kd), seq3(qk), gl.reshape(batch, seq // c, 1, w),
      seq3(gdn), nw_row)
    return o.reshape(t, w)


def _merge_out_kernel(x_ref, oa_ref, ob_ref, ga_ref, gb_ref, wa_ref, wb_ref, wo_ref, o_ref):
    ya = _dot(oa_ref[...], wa_ref[...])
    yb = _dot(ob_ref[...], wb_ref[...])
    y = (_sigmoid(ga_ref[...].astype(F32)) * ya + _sigmoid(gb_ref[...].astype(F32)) * yb)
    o_ref[...] = x_ref[...] + _dot(y.astype(BF16), wo_ref[...])


def _merge_out(x, o_a, o_b, mg, w_a, w_b, w_o, tm):
    t, d = x.shape
    tokd = pl.BlockSpec((tm, d), lambda i: (i, 0))
    wspec = pl.BlockSpec((d, d), lambda i: (0, 0), pipeline_mode=pl.Buffered(1))
    return pl.pallas_call(
        _merge_out_kernel,
        grid=(t // tm,),
        in_specs=[tokd, tokd, tokd,
                  pl.BlockSpec((tm, d), lambda i: (i, 0)),
                  pl.BlockSpec((tm, d), lambda i: (i, 1)),
                  wspec, wspec, wspec],
        out_specs=tokd,
        out_shape=jax.ShapeDtypeStruct((t, d), F32),
        compiler_params=_cparams("parallel"),
        name="merge_out",
    )(x, o_a, o_b, mg, mg, w_a, w_b, w_o)


def _ffn_kernel(x_ref, nw_ref, wg_ref, wu_ref, wd_ref, fw_ref, o_ref, *, final_norm):
    x = x_ref[...]
    h = _rms_norm(x, nw_ref[...]).astype(BF16)
    a = (_silu(_dot(h, wg_ref[...])) * _dot(h, wu_ref[...])).astype(BF16)
    y = x + _dot(a, wd_ref[...])
    if final_norm:
        y = _rms_norm(y, fw_ref[...])
    o_ref[...] = y


def _ffn(x, nw, w_g, w_u, w_d, fw, final_norm, tm):
    t, d = x.shape
    f = w_g.shape[1]
    tokd = pl.BlockSpec((tm, d), lambda i: (i, 0))
    rowd = pl.BlockSpec((1, d), lambda i: (0, 0))
    return pl.pallas_call(
        functools.partial(_ffn_kernel, final_norm=final_norm),
        grid=(t // tm,),
        in_specs=[tokd, rowd,
                  pl.BlockSpec((d, f), lambda i: (0, 0), pipeline_mode=pl.Buffered(1)),
                  pl.BlockSpec((d, f), lambda i: (0, 0), pipeline_mode=pl.Buffered(1)),
                  pl.BlockSpec((f, d), lambda i: (0, 0), pipeline_mode=pl.Buffered(1)),
                  rowd],
        out_specs=tokd,
        out_shape=jax.ShapeDtypeStruct((t, d), F32),
        compiler_params=_cparams("parallel"),
        name="ffn",
    )(x, nw, w_g, w_u, w_d, fw)


def _pad_row(v, offset):
    return jnp.zeros((1, LANES), F32).at[0, offset:offset + v.shape[0]].set(v.astype(F32))


def _layer(x, batch, seq, norm_mix_w, w_in, conv_w, a_log, dt_bias, gdn_norm_w, fox_f_bias,
           w_branch_a, w_branch_b, w_out, norm_ffn_w, w_gate, w_up, w_down, final_w, final_norm):
    d = x.shape[1]
    gw, fw = GDN_WIDTH, FOX_HEADS * FOX_HEAD_DIM
    sizes = (gw, gw, gw, gw, GDN_HEADS, GDN_HEADS, fw, fw, fw, FOX_HEADS, fw, d, d)
    offs = np.concatenate([[0], np.cumsum(sizes)])
    col = lambda i: w_in[:, offs[i]:offs[i + 1]]
    w_gdn = jnp.concatenate([col(0), col(1), col(2), col(3)], axis=1).astype(BF16)
    w_fox = jnp.concatenate([col(6), col(7), col(10)], axis=1).astype(BF16)
    w_fv_t = col(8).T.astype(BF16)
    w_mg = jnp.concatenate([col(11), col(12)], axis=1).astype(BF16)
    n_small = 2 * GDN_HEADS + FOX_HEADS
    w_small = jnp.concatenate([col(4), col(5), col(9), jnp.zeros((d, LANES - n_small), F32)],
                              axis=1).astype(BF16)
    nw = norm_mix_w.reshape(1, d)

    tm = min(seq, TOKEN_TILE)
    tm_wide = min(seq, 2 * TOKEN_TILE)
    (gdn,) = _norm_proj(x, nw, [(w_gdn, F32)], [], tm)
    fox, vt = _norm_proj(x, nw, [(w_fox, BF16)], [(w_fv_t, BF16)], tm_wide)
    mg, small = _norm_proj(x, nw, [(w_mg, BF16), (w_small, F32)], [], tm_wide)

    eq, ek = _fox_bias(small, _pad_row(fox_f_bias, SMALL_FF), batch, seq, tm)
    o_b = _fox_attention(fox, vt, eq, ek, batch, seq, tq=tm)

    conv_w3 = conv_w.reshape(conv_w.shape[0], 3, gw).transpose(1, 0, 2)
    o_a = _gdn(gdn, small, conv_w3, _pad_row(a_log, SMALL_GA), _pad_row(dt_bias, SMALL_GA),
               gdn_norm_w.reshape(1, GDN_HEAD_DIM), batch, seq)

    x1 = _merge_out(x, o_a, o_b, mg, w_branch_a.astype(BF16), w_branch_b.astype(BF16),
                    w_out.astype(BF16), tm_wide)
    return _ffn(x1, norm_ffn_w.reshape(1, d), w_gate.astype(BF16), w_up.astype(BF16),
                w_down.astype(BF16), final_w.reshape(1, d), final_norm, tm)


def kernel(x, norm_mix_w, w_in, conv_w, a_log, dt_bias, gdn_norm_w, fox_f_bias, w_branch_a,
           w_branch_b, w_out, norm_ffn_w, w_gate, w_up, w_down, norm_final_w):
    batch, seq, d = x.shape
    depth = w_in.shape[0]
    assert seq % min(seq, TOKEN_TILE) == 0 and seq % (GDN_PREP_CHUNKS * GDN_CHUNK) == 0
    assert d % LANES == 0 and w_in.shape[2] == 4 * GDN_WIDTH + 4 * FOX_HEADS * FOX_HEAD_DIM \
        + 2 * GDN_HEADS + FOX_HEADS + 2 * d
    h = x.reshape(batch * seq, d)
    for l in range(depth):
        h = _layer(h, batch, seq, norm_mix_w[l], w_in[l], conv_w[l], a_log[l], dt_bias[l],
                   gdn_norm_w[l], fox_f_bias[l], w_branch_a[l], w_branch_b[l], w_out[l],
                   norm_ffn_w[l], w_gate[l], w_up[l], w_down[l], norm_final_w,
                   final_norm=(l == depth - 1))
    return h.reshape(batch, seq, d)
```

```python
import functools

import jax
import jax.numpy as jnp
import numpy as np
from jax import lax
from jax.experimental import pallas as pl
from jax.experimental.pallas import tpu as pltpu

F32 = jnp.float32
BF16 = jnp.bfloat16

EPS = 1e-6
GDN_HEADS = 8
GDN_HEAD_DIM = 128
GDN_CHUNK = 64
FOX_HEADS = 16
FOX_HEAD_DIM = 64
LANES = 128
VMEM_LIMIT_BYTES = 56 * 1024 * 1024
TOKEN_TILE = 512
NEG_BIG = -1e30


def _cparams(*semantics):
    return pltpu.CompilerParams(dimension_semantics=semantics,
                                vmem_limit_bytes=VMEM_LIMIT_BYTES)


def _split2(x):
    hi = x.astype(BF16)
    lo = (x - hi.astype(F32)).astype(BF16)
    return hi, lo


def _split3(x):
    hi = x.astype(BF16)
    r = x - hi.astype(F32)
    mid = r.astype(BF16)
    lo = (r - mid.astype(F32)).astype(BF16)
    return hi, mid, lo


def _dot(a, b):
    return jnp.dot(a, b, preferred_element_type=F32)


def _dot_exact_rhs(x, m_bf16):
    n = x.shape[0]
    y = _dot(jnp.concatenate(_split3(x), axis=0), m_bf16)
    return y[:n] + y[n:2 * n] + y[2 * n:]


def _dot_exact_lhs(m_bf16, x):
    n = x.shape[1]
    y = _dot(m_bf16, jnp.concatenate(_split3(x), axis=1))
    return y[:, :n] + y[:, n:2 * n] + y[:, 2 * n:]


def _div_pow2(x, n):
    assert n & (n - 1) == 0
    return jnp.right_shift(x, n.bit_length() - 1)


def _mod_pow2(x, n):
    assert n & (n - 1) == 0
    return jnp.bitwise_and(x, n - 1)


def _softplus(y):
    return jnp.maximum(y, 0.0) + jnp.log(1.0 + jnp.exp(-jnp.abs(y)))


def _sigmoid(y):
    return 0.5 * jnp.tanh(0.5 * y) + 0.5


def _silu(y):
    half = 0.5 * y
    return half + half * jnp.tanh(half)


def _rms_norm(x, w):
    ms = jnp.mean(x * x, axis=-1, keepdims=True)
    return x * lax.rsqrt(ms + EPS) * w


PROJ_CHUNK = 1024


def _norm_proj_kernel(x_ref, nw_ref, *refs, n_plain):
    n_w = len(refs) // 2
    h = _rms_norm(x_ref[...], nw_ref[...]).astype(BF16)
    for idx in range(n_w):
        w_ref, o_ref = refs[idx], refs[n_w + idx]
        if idx < n_plain:
            n = o_ref.shape[1]
            for c in range(0, n, PROJ_CHUNK):
                sl = slice(c, min(c + PROJ_CHUNK, n))
                o_ref[:, sl] = _dot(h, w_ref[:, sl]).astype(o_ref.dtype)
        else:
            n = o_ref.shape[0]
            for c in range(0, n, PROJ_CHUNK):
                sl = slice(c, min(c + PROJ_CHUNK, n))
                o_ref[sl, :] = lax.dot_general(w_ref[sl, :], h, (((1,), (1,)), ((), ())),
                                               preferred_element_type=F32).astype(o_ref.dtype)


def _norm_proj(x, nw, plain, transposed, tm):
    t, d = x.shape
    in_specs = [pl.BlockSpec((tm, d), lambda i: (i, 0)), pl.BlockSpec((1, d), lambda i: (0, 0))]
    out_specs, out_shape = [], []
    for w, dtype in plain:
        n = w.shape[1]
        in_specs.append(pl.BlockSpec((d, n), lambda i: (0, 0), pipeline_mode=pl.Buffered(1)))
        out_specs.append(pl.BlockSpec((tm, n), lambda i: (i, 0)))
        out_shape.append(jax.ShapeDtypeStruct((t, n), dtype))
    for wt, dtype in transposed:
        n = wt.shape[0]
        in_specs.append(pl.BlockSpec((n, d), lambda i: (0, 0), pipeline_mode=pl.Buffered(1)))
        out_specs.append(pl.BlockSpec((n, tm), lambda i: (0, i)))
        out_shape.append(jax.ShapeDtypeStruct((n, t), dtype))
    return pl.pallas_call(
        functools.partial(_norm_proj_kernel, n_plain=len(plain)),
        grid=(t // tm,),
        in_specs=in_specs,
        out_specs=out_specs,
        out_shape=out_shape,
        compiler_params=_cparams("parallel"),
        name="norm_proj",
    )(x, nw, *[w for w, _ in plain], *[w for w, _ in transposed])


SMALL_GA = 0
SMALL_GB = GDN_HEADS
SMALL_FF = 2 * GDN_HEADS
BIAS_TERMS = 3
BIAS_HEAD_STRIDE = 8


def _bias_placements():
    n_out = (FOX_HEADS // 2) * LANES
    pq = np.zeros((LANES, n_out), np.float32)
    pk = np.zeros((LANES, n_out), np.float32)
    ones_q = np.zeros((1, n_out), np.float32)
    ones_k = np.zeros((1, n_out), np.float32)
    for h in range(FOX_HEADS):
        base = (h // 2) * LANES + (h % 2) * BIAS_HEAD_STRIDE
        for t in range(BIAS_TERMS):
            pq[SMALL_FF + t * FOX_HEADS + h, base + t] = 1.0
            pk[SMALL_FF + t * FOX_HEADS + h, base + BIAS_TERMS + t] = -1.0
            ones_q[0, base + BIAS_TERMS + t] = 1.0
            ones_k[0, base + t] = 1.0
    return pq, pk, ones_q, ones_k


def _fox_bias_kernel(small_ref, fb_ref, pq_ref, pk_ref, oq_ref, ok_ref, eq_ref, ek_ref,
                     carry_ref):
    tm = small_ref.shape[0]

    @pl.when(pl.program_id(1) == 0)
    def _():
        carry_ref[...] = jnp.zeros_like(carry_ref)

    z = small_ref[...] + fb_ref[...]
    log_f = -_softplus(-z)
    row = lax.broadcasted_iota(jnp.int32, (tm, tm), 0)
    col = lax.broadcasted_iota(jnp.int32, (tm, tm), 1)
    tril = jnp.where(row >= col, 1.0, 0.0).astype(BF16)
    cum = _dot_exact_lhs(tril, log_f) + carry_ref[0:1, :]
    carry_ref[...] = jnp.broadcast_to(cum[tm - 1:tm, :], carry_ref.shape)
    lane = lax.broadcasted_iota(jnp.int32, (1, LANES), 1)
    is_ff = jnp.logical_and(lane >= SMALL_FF, lane < SMALL_FF + FOX_HEADS)
    packed = jnp.zeros_like(cum)
    for t, term in enumerate(_split3(cum)):
        part = jnp.where(is_ff, term.astype(F32), 0.0)
        packed = packed + (pltpu.roll(part, t * FOX_HEADS, axis=1) if t else part)
    terms = packed.astype(BF16)
    eq_ref[...] = (oq_ref[...] + _dot(terms, pq_ref[...])).astype(BF16)
    ek_ref[...] = (ok_ref[...] + _dot(terms, pk_ref[...])).astype(BF16)


def _fox_bias(small, fb_row, batch, seq, tm):
    t = small.shape[0]
    pq, pk, ones_q, ones_k = _bias_placements()
    n_out = pq.shape[1]
    nt = seq // tm
    const2 = lambda b, i: (0, 0)
    return pl.pallas_call(
        _fox_bias_kernel,
        grid=(batch, nt),
        in_specs=[pl.BlockSpec((tm, LANES), lambda b, i: (b * nt + i, 0)),
                  pl.BlockSpec((1, LANES), const2),
                  pl.BlockSpec((LANES, n_out), const2),
                  pl.BlockSpec((LANES, n_out), const2),
                  pl.BlockSpec((1, n_out), const2),
                  pl.BlockSpec((1, n_out), const2)],
        out_specs=[pl.BlockSpec((tm, n_out), lambda b, i: (b * nt + i, 0)),
                   pl.BlockSpec((tm, n_out), lambda b, i: (b * nt + i, 0))],
        out_shape=[jax.ShapeDtypeStruct((t, n_out), BF16),
                   jax.ShapeDtypeStruct((t, n_out), BF16)],
        scratch_shapes=[pltpu.VMEM((8, LANES), F32)],
        compiler_params=_cparams("parallel", "arbitrary"),
        name="fox_bias",
    )(small, fb_row, jnp.asarray(pq, BF16), jnp.asarray(pk, BF16),
      jnp.asarray(ones_q), jnp.asarray(ones_k))


FOX_PAIRS_PER_STEP = 4
FOX_SUM_ROWS = 16
FOX_SCORES_AHEAD = 3


def _fox_attention_kernel(q_ref, eq_ref, k_ref, ek_ref, vt_ref, fo_ref, o_ref, *, tq):
    i = pl.program_id(2)
    n_pairs = q_ref.shape[1] // LANES
    n_heads = 2 * n_pairs
    lane = lax.broadcasted_iota(jnp.int32, (1, LANES), 1)
    head_a = lane < FOX_HEAD_DIM
    bias_a = lane < BIAS_HEAD_STRIDE
    zero = jnp.zeros((), BF16)
    qm = []
    for p in range(n_pairs):
        psl = slice(p * LANES, (p + 1) * LANES)
        q = q_ref[:, psl] * jnp.asarray(FOX_HEAD_DIM ** -0.5, BF16)
        eq = eq_ref[:, psl]
        qm.append(jnp.concatenate([jnp.where(head_a, q, zero), jnp.where(bias_a, eq, zero)], axis=1))
        qm.append(jnp.concatenate([jnp.where(head_a, zero, q), jnp.where(bias_a, zero, eq)], axis=1))

    def tiles(js, carry, masked):
        cells = [(j, idx) for j in js for idx in range(n_heads)]

        def score(cell):
            j, idx = cell
            start = pl.multiple_of(j * tq, tq)
            psl = slice((idx // 2) * LANES, (idx // 2 + 1) * LANES)
            kk = jnp.concatenate([k_ref[pl.ds(start, tq), psl], ek_ref[pl.ds(start, tq), psl]], axis=1)
            return lax.dot_general(kk, qm[idx], (((1,), (1,)), ((), ())),
                                   preferred_element_type=F32)

        def value(cell):
            j, idx = cell
            start = pl.multiple_of(j * tq, tq)
            vt = vt_ref[idx * FOX_HEAD_DIM:(idx + 1) * FOX_HEAD_DIM, pl.ds(start, tq)]
            return jnp.concatenate([vt, jnp.ones((FOX_SUM_ROWS, tq), BF16)], axis=0)

        state = list(carry)
        ahead = [score(cell) for cell in cells[:FOX_SCORES_AHEAD]]
        for n, cell in enumerate(cells):
            s = ahead.pop(0)
            if n + FOX_SCORES_AHEAD < len(cells):
                ahead.append(score(cells[n + FOX_SCORES_AHEAD]))
            m, acc = state[cell[1]]
            if masked:
                key = lax.broadcasted_iota(jnp.int32, (tq, tq), 0)
                qry = lax.broadcasted_iota(jnp.int32, (tq, tq), 1)
                s = jnp.where(key <= qry, s, NEG_BIG)
            m_new = jnp.maximum(m, jnp.max(s, axis=0, keepdims=True))
            alpha = jnp.exp(m - m_new)
            prob = jnp.exp(s - m_new).astype(BF16)
            state[cell[1]] = (m_new, alpha * acc + _dot(value(cell), prob))
        return tuple(state)

    init = tuple((jnp.full((1, tq), NEG_BIG, F32), jnp.zeros((FOX_HEAD_DIM + FOX_SUM_ROWS, tq), F32))
                 for _ in range(n_heads))
    carry = lax.fori_loop(0, i // 2, lambda t, c: tiles((2 * t, 2 * t + 1), c, False), init)
    carry = lax.cond(i % 2 == 1, lambda c: tiles((i - 1,), c, False), lambda c: c, carry)
    final = tiles((i,), carry, True)
    for p in range(n_pairs):
        psl = slice(p * LANES, (p + 1) * LANES)
        heads = []
        for h in range(2):
            acc = final[2 * p + h][1]
            heads.append(acc[:FOX_HEAD_DIM] / acc[FOX_HEAD_DIM:FOX_HEAD_DIM + 1])
        gate = _sigmoid(fo_ref[:, psl].astype(F32))
        o_ref[:, psl] = (jnp.concatenate(heads, axis=0).T * gate).astype(o_ref.dtype)


def _fox_attention(fox, vt, eq, ek, batch, seq, tq):
    t = fox.shape[0]
    pairs = FOX_HEADS // 2
    steps = pairs // FOX_PAIRS_PER_STEP
    width = FOX_PAIRS_PER_STEP * LANES
    nq = seq // tq
    return pl.pallas_call(
        functools.partial(_fox_attention_kernel, tq=tq),
        grid=(batch, steps, nq),
        in_specs=[pl.BlockSpec((tq, width), lambda b, p, i: (b * nq + i, p)),
                  pl.BlockSpec((tq, width), lambda b, p, i: (b * nq + i, p)),
                  pl.BlockSpec((seq, width), lambda b, p, i: (b, steps + p)),
                  pl.BlockSpec((seq, width), lambda b, p, i: (b, p)),
                  pl.BlockSpec((width, seq), lambda b, p, i: (p, b)),
                  pl.BlockSpec((tq, width), lambda b, p, i: (b * nq + i, 2 * steps + p))],
        out_specs=pl.BlockSpec((tq, width), lambda b, p, i: (b * nq + i, p)),
        out_shape=jax.ShapeDtypeStruct((t, pairs * LANES), BF16),
        compiler_params=_cparams("parallel", "parallel", "arbitrary"),
        name="fox_attention",
    )(fox, eq, fox, ek, vt, fox)


GDN_WIDTH = GDN_HEADS * GDN_HEAD_DIM
GDN_GROUP = 4
GDN_CAT = GDN_HEADS * GDN_CHUNK
GDN_PAIRS = GDN_HEADS // 2
GDN_PAIR = 2 * GDN_HEAD_DIM
GDN_SCAN_CHUNKS = 4
GDN_PREP_CHUNKS = 4
CONV_HALO = 8


def _gdn_expanders():
    e_g128 = np.zeros((LANES, GDN_WIDTH), np.float32)
    e_b128 = np.zeros((LANES, GDN_WIDTH), np.float32)
    e_g64 = np.zeros((LANES, GDN_CAT), np.float32)
    for h in range(GDN_HEADS):
        e_g128[SMALL_GA + h, h * GDN_HEAD_DIM:(h + 1) * GDN_HEAD_DIM] = 1.0
        e_b128[SMALL_GB + h, h * GDN_HEAD_DIM:(h + 1) * GDN_HEAD_DIM] = 1.0
        e_g64[SMALL_GA + h, h * GDN_CHUNK:(h + 1) * GDN_CHUNK] = 1.0
    return e_g128, e_b128, e_g64


def _block_diag(x, n_blocks):
    r, total = x.shape
    w = total // n_blocks
    tile_w = max(w, LANES)
    per_tile = tile_w // w
    zeros = jnp.zeros((r, tile_w), x.dtype)
    lane_block = _div_pow2(lax.broadcasted_iota(jnp.int32, (1, tile_w), 1), w)
    rows = []
    for h in range(n_blocks):
        t = h // per_tile
        tile = x[:, t * tile_w:(t + 1) * tile_w]
        if per_tile > 1:
            tile = tile * jnp.where(lane_block == h % per_tile, 1.0, 0.0).astype(x.dtype)
        rows.append(jnp.concatenate([tile if i == t else zeros for i in range(total // tile_w)], axis=1))
    return jnp.concatenate(rows, axis=0)


def _rows(x, i, n):
    return x[i * n:(i + 1) * n]


def _headwise_products(lhs, b):
    c = b.shape[0]
    n = len(lhs)
    b_hi, b_lo = _split2(b)
    bd_hi = _block_diag(b_hi, GDN_GROUP)
    bd_lo = _block_diag(b_lo, GDN_GROUP)
    parts = [_split2(x) for x in lhs]
    his = [p[0] for p in parts]
    los = [p[1] for p in parts]
    top = _dot(jnp.concatenate(his + los, axis=0), bd_hi)
    bot = _dot(jnp.concatenate(his, axis=0), bd_lo)
    return [_rows(top, i, c) + _rows(top, n + i, c) + _rows(bot, i, c) for i in range(n)]


def _unit_lower_inverses(l_cats):
    c, n = l_cats[0].shape
    r = lax.broadcasted_iota(jnp.int32, (c, n), 0)
    j = _mod_pow2(lax.broadcasted_iota(jnp.int32, (c, n), 1), c)
    eye = jnp.where(r == j, 1.0, 0.0)
    ss = [eye - l for l in l_cats]
    ps = [_headwise_products([l], l)[0] for l in l_cats]
    k = 2
    while 2 * k < c:
        stage = [_headwise_products([p, s], p) for p, s in zip(ps, ss)]
        ps = [st[0] for st in stage]
        ss = [s + st[1] for s, st in zip(ss, stage)]
        k *= 2
    return [s + _headwise_products([s], p)[0] for p, s in zip(ps, ss)]


def _gdn_prep_kernel(q_ref, k_ref, v_ref, hq_ref, hk_ref, hv_ref, small_ref, cw_ref, alog_ref,
                     dtb_ref, eg128_ref, eb128_ref, eg64_ref,
                     u_ref, w_ref, qg_ref, kd_ref, qk_ref, gl_ref, *, blocks_per_seq):
    c = GDN_CHUNK
    n_conv = cw_ref.shape[1]
    seq_start = lax.rem(pl.program_id(0), blocks_per_seq) == 0

    sublane = lax.broadcasted_iota(jnp.int32, (1, CONV_HALO, 1), 1)
    conv = []
    for s, (ref, halo_ref) in enumerate(((q_ref, hq_ref), (k_ref, hk_ref), (v_ref, hv_ref))):
        x = ref[...]
        halo = jnp.where(seq_start, 0.0, halo_ref[...])
        groups = x.reshape(-1, CONV_HALO, x.shape[1])
        prev = jnp.concatenate([halo[None], groups[:-1]], axis=0)
        acc = x * cw_ref[s, n_conv - 1:n_conv, :]
        for shift in range(1, n_conv):
            mixed = jnp.where(sublane >= CONV_HALO - shift, prev, groups)
            moved = pltpu.roll(mixed, shift, axis=1).reshape(x.shape)
            acc = acc + moved * cw_ref[s, n_conv - 1 - shift:n_conv - shift, :]
        conv.append(_silu(acc))
    cq, ck, cv = conv
    rows = cq.shape[0]

    def l2n(x):
        parts = []
        for h in range(GDN_HEADS):
            xh = x[:, h * GDN_HEAD_DIM:(h + 1) * GDN_HEAD_DIM]
            ss = jnp.sum(xh * xh, axis=-1, keepdims=True)
            parts.append(xh * lax.rsqrt(ss + EPS))
        return jnp.concatenate(parts, axis=1)

    qn = l2n(cq)
    kn = l2n(ck)

    small = small_ref[...]
    g_tok = -jnp.exp(alog_ref[...]) * _softplus(small + dtb_ref[...])
    beta_tok = _sigmoid(small)
    row = lax.broadcasted_iota(jnp.int32, (rows, rows), 0)
    col = lax.broadcasted_iota(jnp.int32, (rows, rows), 1)
    same_chunk = _div_pow2(row, c) == _div_pow2(col, c)
    tril = jnp.where(row >= col, jnp.where(same_chunk, 1.0, 0.0), 0.0).astype(BF16)
    gc_tok = _dot_exact_lhs(tril, g_tok)
    gc128 = _dot_exact_rhs(gc_tok, eg128_ref[...])
    gc64 = _dot_exact_rhs(gc_tok, eg64_ref[...])
    beta128 = _dot_exact_rhs(beta_tok, eb128_ref[...])

    exp_gc = jnp.exp(gc128)
    kb = kn * beta128
    vb = cv * beta128
    kbg = kb * exp_gc
    qs = qn * (GDN_HEAD_DIM ** -0.5)
    qg_ref[...] = (qs * exp_gc).astype(BF16)
    kb16 = kb.astype(BF16)
    qs16 = qs.astype(BF16)
    kn16 = kn.astype(BF16)

    r_cat = lax.broadcasted_iota(jnp.int32, (c, GDN_CAT), 0)
    j_cat = _mod_pow2(lax.broadcasted_iota(jnp.int32, (c, GDN_CAT), 1), c)
    tri_cat = r_cat >= j_cat
    strict_cat = r_cat > j_cat
    group_w = GDN_GROUP * GDN_HEAD_DIM
    group_c = GDN_GROUP * c
    contract_last = (((1,), (1,)), ((), ()))

    n_chunks = rows // c
    n_groups = GDN_HEADS // GDN_GROUP
    l_cats = []
    for ch in range(n_chunks):
        rs = slice(ch * c, (ch + 1) * c)
        gc64_c = gc64[rs]
        gc_row = jnp.sum(jnp.where(r_cat == j_cat, gc64_c, 0.0), axis=0, keepdims=True)
        decay = jnp.where(tri_cat, jnp.exp(jnp.where(tri_cat, gc64_c - gc_row, 0.0)), 0.0)
        g_last = gc128[ch * c + c - 1:ch * c + c, :]
        gl_ref[ch] = jnp.exp(g_last)
        kd_ref[rs, :] = (kn[rs] * jnp.exp(g_last - gc128[rs])).astype(BF16)

        for gidx in range(n_groups):
            ksl = slice(gidx * group_w, (gidx + 1) * group_w)
            csl = slice(gidx * group_c, (gidx + 1) * group_c)
            bd_k = _block_diag(kn16[rs, ksl], GDN_GROUP)
            both = lax.dot_general(jnp.concatenate([kb16[rs, ksl], qs16[rs, ksl]], axis=0), bd_k,
                                   contract_last, preferred_element_type=F32)
            dec = decay[:, csl]
            l_cats.append(jnp.where(strict_cat[:, csl], both[:c] * dec, 0.0))
            qk_ref[rs, csl] = (both[c:] * dec).astype(BF16)

    invs = _unit_lower_inverses(l_cats)

    for ch in range(n_chunks):
        rs = slice(ch * c, (ch + 1) * c)
        inv_cat = jnp.concatenate(invs[ch * n_groups:(ch + 1) * n_groups], axis=1)
        for p in range(GDN_PAIRS):
            wsl = slice(p * GDN_PAIR, (p + 1) * GDN_PAIR)
            inv_p = inv_cat[:, p * 2 * c:(p + 1) * 2 * c]
            vb_p, kbg_p = vb[rs, wsl], kbg[rs, wsl]
            rhs = jnp.concatenate([_block_diag(vb_p, 2), _block_diag(kbg_p, 2)], axis=1)
            i_hi, i_lo = _split2(inv_p)
            r_hi, r_lo = _split2(rhs)
            top = _dot(jnp.concatenate([i_hi, i_lo], axis=0), r_hi)
            sol = top[:c] + top[c:] + _dot(i_hi, r_lo)
            u_ref[rs, wsl] = sol[:, :GDN_PAIR]
            w_ref[rs, wsl] = sol[:, GDN_PAIR:].astype(BF16)


def _gdn_scan_kernel(u_ref, w_ref, qg_ref, kd_ref, qk_ref, gl_ref, z_ref, nw_ref, o_ref, state_ref):
    c = GDN_CHUNK

    @pl.when(pl.program_id(0) == 0)
    def _():
        state_ref[...] = jnp.zeros_like(state_ref)

    nw = nw_ref[...]
    zblock = jnp.zeros((GDN_HEAD_DIM, GDN_HEAD_DIM), BF16)
    hsl = lambda h: slice(h * GDN_HEAD_DIM, (h + 1) * GDN_HEAD_DIM)
    wsl = lambda p: slice(p * GDN_PAIR, (p + 1) * GDN_PAIR)
    cells = [(b, p) for b in range(u_ref.shape[0]) for p in range(GDN_PAIRS)]
    states = [[state_ref[b, 2 * p], state_ref[b, 2 * p + 1]] for b, p in cells]
    for ci in range(u_ref.shape[1] // c):
        rs = slice(ci * c, (ci + 1) * c)
        boths = []
        for (b, p), (sa, sb) in zip(cells, states):
            s_bd = jnp.concatenate([jnp.concatenate([sa.astype(BF16), zblock], axis=1),
                                    jnp.concatenate([zblock, sb.astype(BF16)], axis=1)], axis=0)
            boths.append(_dot(jnp.concatenate([w_ref[b, rs, wsl(p)], qg_ref[b, rs, wsl(p)]], axis=0),
                              s_bd))
        v16s = [(u_ref[b, rs, wsl(p)] - both[:c]).astype(BF16) for (b, p), both in zip(cells, boths)]
        outs, upds = [], []
        for (b, p), both, v16 in zip(cells, boths, v16s):
            v_bd = _block_diag(v16, 2)
            outs.append(both[c:] + _dot(qk_ref[b, rs, p * 2 * c:(p + 1) * 2 * c], v_bd))
            upds.append([lax.dot_general(kd_ref[b, rs, hsl(2 * p + h)], v16[:, hsl(h)],
                                         (((0,), (0,)), ((), ())), preferred_element_type=F32)
                         for h in range(2)])
        for n, ((b, p), o, upd) in enumerate(zip(cells, outs, upds)):
            halves = []
            for h in range(2):
                head = 2 * p + h
                states[n][h] = states[n][h] * gl_ref[b, ci][:, hsl(head)] + upd[h]
                oh = o[:, hsl(h)]
                ms = jnp.mean(oh * oh, axis=-1, keepdims=True)
                halves.append(oh * lax.rsqrt(ms + EPS) * nw)
            o_ref[b, rs, wsl(p)] = (jnp.concatenate(halves, axis=1)
                                    * _silu(z_ref[b, rs, wsl(p)])).astype(o_ref.dtype)
    for (b, p), st in zip(cells, states):
        for h in range(2):
            state_ref[b, 2 * p + h] = st[h]


def _gdn(gdn, small, conv_w3, alog_row, dtb_row, nw_row, batch, seq):
    t = gdn.shape[0]
    c = GDN_CHUNK
    w = GDN_WIDTH
    rows = GDN_PREP_CHUNKS * c
    e_g128, e_b128, e_g64 = _gdn_expanders()
    n_conv = conv_w3.shape[1]
    tok = lambda col: (lambda i: (i, col))
    halo = lambda col: (lambda i: (jnp.maximum(i * (rows // CONV_HALO) - 1, 0), col))
    const2 = lambda i: (0, 0)
    u, wk, qg, kd, qk, gl = pl.pallas_call(
        functools.partial(_gdn_prep_kernel, blocks_per_seq=seq // rows),
        grid=(t // rows,),
        in_specs=[pl.BlockSpec((rows, w), tok(0)),
                  pl.BlockSpec((rows, w), tok(1)),
                  pl.BlockSpec((rows, w), tok(2)),
                  pl.BlockSpec((CONV_HALO, w), halo(0)),
                  pl.BlockSpec((CONV_HALO, w), halo(1)),
                  pl.BlockSpec((CONV_HALO, w), halo(2)),
                  pl.BlockSpec((rows, LANES), tok(0)),
                  pl.BlockSpec((3, n_conv, w), lambda i: (0, 0, 0)),
                  pl.BlockSpec((1, LANES), const2),
                  pl.BlockSpec((1, LANES), const2),
                  pl.BlockSpec((LANES, w), const2),
                  pl.BlockSpec((LANES, w), const2),
                  pl.BlockSpec((LANES, GDN_CAT), const2)],
        out_specs=[pl.BlockSpec((rows, w), tok(0)),
                   pl.BlockSpec((rows, w), tok(0)),
                   pl.BlockSpec((rows, w), tok(0)),
                   pl.BlockSpec((rows, w), tok(0)),
                   pl.BlockSpec((rows, GDN_CAT), tok(0)),
                   pl.BlockSpec((GDN_PREP_CHUNKS, 1, w), lambda i: (i, 0, 0))],
        out_shape=[jax.ShapeDtypeStruct((t, w), F32),
                   jax.ShapeDtypeStruct((t, w), BF16),
                   jax.ShapeDtypeStruct((t, w), BF16),
                   jax.ShapeDtypeStruct((t, w), BF16),
                   jax.ShapeDtypeStruct((t, GDN_CAT), BF16),
                   jax.ShapeDtypeStruct((t // c, 1, w), F32)],
        compiler_params=_cparams("parallel"),
        name="gdn_prep",
    )(gdn, gdn, gdn, gdn, gdn, gdn, small, conv_w3, alog_row, dtb_row,
      jnp.asarray(e_g128, BF16), jnp.asarray(e_b128, BF16), jnp.asarray(e_g64, BF16))

    seq3 = lambda a: a.reshape(batch, seq, a.shape[-1])
    scan_rows = GDN_SCAN_CHUNKS * c
    blk = lambda width, col=0: pl.BlockSpec((batch, scan_rows, width), lambda n: (0, n, col))
    o = pl.pallas_call(
        _gdn_scan_kernel,
        grid=(seq // scan_rows,),
        in_specs=[blk(w), blk(w), blk(w), blk(w), blk(GDN_CAT),
                  pl.BlockSpec((batch, GDN_SCAN_CHUNKS, 1, w), lambda n: (0, n, 0, 0)),
                  blk(w, 3),
                  pl.BlockSpec((1, GDN_HEAD_DIM), lambda n: (0, 0))],
        out_specs=blk(w),
        out_shape=jax.ShapeDtypeStruct((batch, seq, w), BF16),
        scratch_shapes=[pltpu.VMEM((batch, GDN_HEADS, GDN_HEAD_DIM, GDN_HEAD_DIM), F32)],
        compiler_params=_cparams("arbitrary"),
        name="gdn_scan",
    )(seq3(u), seq3(wk), seq3(qg), seq3(kd), seq3(qk), gl.reshape(batch, seq // c, 1, w),
      seq3(gdn), nw_row)
    return o.reshape(t, w)


def _merge_out_kernel(x_ref, oa_ref, ob_ref, ga_ref, gb_ref, wa_ref, wb_ref, wo_ref, o_ref):
    ya = _dot(oa_ref[...], wa_ref[...])
    yb = _dot(ob_ref[...], wb_ref[...])
    y = (_sigmoid(ga_ref[...].astype(F32)) * ya + _sigmoid(gb_ref[...].astype(F32)) * yb)
    o_ref[...] = x_ref[...] + _dot(y.astype(BF16), wo_ref[...])


def _merge_out(x, o_a, o_b, mg, w_a, w_b, w_o, tm):
    t, d = x.shape
    tokd = pl.BlockSpec((tm, d), lambda i: (i, 0))
    wspec = pl.BlockSpec((d, d), lambda i: (0, 0), pipeline_mode=pl.Buffered(1))
    return pl.pallas_call(
        _merge_out_kernel,
        grid=(t // tm,),
        in_specs=[tokd, tokd, tokd,
                  pl.BlockSpec((tm, d), lambda i: (i, 0)),
                  pl.BlockSpec((tm, d), lambda i: (i, 1)),
                  wspec, wspec, wspec],
        out_specs=tokd,
        out_shape=jax.ShapeDtypeStruct((t, d), F32),
        compiler_params=_cparams("parallel"),
        name="merge_out",
    )(x, o_a, o_b, mg, mg, w_a, w_b, w_o)


def _ffn_kernel(x_ref, nw_ref, wg_ref, wu_ref, wd_ref, fw_ref, o_ref, *, final_norm):
    x = x_ref[...]
    h = _rms_norm(x, nw_ref[...]).astype(BF16)
    a = (_silu(_dot(h, wg_ref[...])) * _dot(h, wu_ref[...])).astype(BF16)
    y = x + _dot(a, wd_ref[...])
    if final_norm:
        y = _rms_norm(y, fw_ref[...])
    o_ref[...] = y


def _ffn(x, nw, w_g, w_u, w_d, fw, final_norm, tm):
    t, d = x.shape
    f = w_g.shape[1]
    tokd = pl.BlockSpec((tm, d), lambda i: (i, 0))
    rowd = pl.BlockSpec((1, d), lambda i: (0, 0))
    return pl.pallas_call(
        functools.partial(_ffn_kernel, final_norm=final_norm),
        grid=(t // tm,),
        in_specs=[tokd, rowd,
                  pl.BlockSpec((d, f), lambda i: (0, 0), pipeline_mode=pl.Buffered(1)),
                  pl.BlockSpec((d, f), lambda i: (0, 0), pipeline_mode=pl.Buffered(1)),
                  pl.BlockSpec((f, d), lambda i: (0, 0), pipeline_mode=pl.Buffered(1)),
                  rowd],
        out_specs=tokd,
        out_shape=jax.ShapeDtypeStruct((t, d), F32),
        compiler_params=_cparams("parallel"),
        name="ffn",
    )(x, nw, w_g, w_u, w_d, fw)


def _pad_row(v, offset):
    return jnp.zeros((1, LANES), F32).at[0, offset:offset + v.shape[0]].set(v.astype(F32))


def _layer(x, batch, seq, norm_mix_w, w_in, conv_w, a_log, dt_bias, gdn_norm_w, fox_f_bias,
           w_branch_a, w_branch_b, w_out, norm_ffn_w, w_gate, w_up, w_down, final_w, final_norm):
    d = x.shape[1]
    gw, fw = GDN_WIDTH, FOX_HEADS * FOX_HEAD_DIM
    sizes = (gw, gw, gw, gw, GDN_HEADS, GDN_HEADS, fw, fw, fw, FOX_HEADS, fw, d, d)
    offs = np.concatenate([[0], np.cumsum(sizes)])
    col = lambda i: w_in[:, offs[i]:offs[i + 1]]
    w_gdn = jnp.concatenate([col(0), col(1), col(2), col(3)], axis=1).astype(BF16)
    w_fox = jnp.concatenate([col(6), col(7), col(10)], axis=1).astype(BF16)
    w_fv_t = col(8).T.astype(BF16)
    w_mg = jnp.concatenate([col(11), col(12)], axis=1).astype(BF16)
    n_small = 2 * GDN_HEADS + FOX_HEADS
    w_small = jnp.concatenate([col(4), col(5), col(9), jnp.zeros((d, LANES - n_small), F32)],
                              axis=1).astype(BF16)
    nw = norm_mix_w.reshape(1, d)

    tm = min(seq, TOKEN_TILE)
    tm_wide = min(seq, 2 * TOKEN_TILE)
    (gdn,) = _norm_proj(x, nw, [(w_gdn, F32)], [], tm)
    fox, vt = _norm_proj(x, nw, [(w_fox, BF16)], [(w_fv_t, BF16)], tm_wide)
    mg, small = _norm_proj(x, nw, [(w_mg, BF16), (w_small, F32)], [], tm_wide)

    eq, ek = _fox_bias(small, _pad_row(fox_f_bias, SMALL_FF), batch, seq, tm)
    o_b = _fox_attention(fox, vt, eq, ek, batch, seq, tq=tm)

    conv_w3 = conv_w.reshape(conv_w.shape[0], 3, gw).transpose(1, 0, 2)
    o_a = _gdn(gdn, small, conv_w3, _pad_row(a_log, SMALL_GA), _pad_row(dt_bias, SMALL_GA),
               gdn_norm_w.reshape(1, GDN_HEAD_DIM), batch, seq)

    x1 = _merge_out(x, o_a, o_b, mg, w_branch_a.astype(BF16), w_branch_b.astype(BF16),
                    w_out.astype(BF16), tm_wide)
    return _ffn(x1, norm_ffn_w.reshape(1, d), w_gate.astype(BF16), w_up.astype(BF16),
                w_down.astype(BF16), final_w.reshape(1, d), final_norm, tm)


def kernel(x, norm_mix_w, w_in, conv_w, a_log, dt_bias, gdn_norm_w, fox_f_bias, w_branch_a,
           w_branch_b, w_out, norm_ffn_w, w_gate, w_up, w_down, norm_final_w):
    batch, seq, d = x.shape
    depth = w_in.shape[0]
    assert seq % min(seq, TOKEN_TILE) == 0 and seq % (GDN_PREP_CHUNKS * GDN_CHUNK) == 0
    assert d % LANES == 0 and w_in.shape[2] == 4 * GDN_WIDTH + 4 * FOX_HEADS * FOX_HEAD_DIM \
        + 2 * GDN_HEADS + FOX_HEADS + 2 * d
    h = x.reshape(batch * seq, d)
    for l in range(depth):
        h = _layer(h, batch, seq, norm_mix_w[l], w_in[l], conv_w[l], a_log[l], dt_bias[l],
                   gdn_norm_w[l], fox_f_bias[l], w_branch_a[l], w_branch_b[l], w_out[l],
                   norm_ffn_w[l], w_gate[l], w_up[l], w_down[l], norm_final_w,
                   final_norm=(l == depth - 1))
    return h.reshape(batch, seq, d)
```
